```python
import math
import jax
import jax.numpy as jnp
from jax import lax
import numpy as np

D_MODEL = 2048
BATCH = 4
SEQ = 2048
DEPTH = 1
DEC_BATCH = 32
DEC_SEQ = 8
PAST_LEN = 8192
PAGE_SIZE = 128

ATT_WIDTH = D_MODEL // 2
ATT_HEAD_DIM = 128
ATT_HEADS = ATT_WIDTH // ATT_HEAD_DIM
MOBA_BLOCK = 256
MOBA_TOPK = 3
QROWS = 128
ML_WIDTH = D_MODEL - ATT_WIDTH
ML_HEADS = 4
ML_HEAD_DIM = ML_WIDTH // ML_HEADS
ML_CHUNK = 64
IN_WIDTH = 3 * ATT_WIDTH + 4 * ML_WIDTH + 2 * ML_HEADS
N_GROUPS = 4
EXPERTS_PER_GROUP = 4
N_EXPERTS = N_GROUPS * EXPERTS_PER_GROUP
EXPERT_TOPK = 2
EXPERT_FF = D_MODEL // 4
N_MOD = 6
EPS = 1e-6

kernel_name = 'hymba_moba_mlstm_hmoe_step'


def rms_norm(x, g):
    xf = x.astype(jnp.float32)
    y = xf * lax.rsqrt(jnp.mean(xf * xf, axis=-1, keepdims=True) + EPS)
    return (y * g.astype(jnp.float32)).astype(x.dtype)


def alibi_slopes():
    return 2.0 ** (-8.0 * jnp.arange(1, ATT_HEADS + 1, dtype=jnp.float32) / ATT_HEADS)


def to_blocks(parts):
    B, _, H, hd = parts[0].shape
    L = sum(p.shape[1] for p in parts)
    pad = (-L) % MOBA_BLOCK
    if pad:
        parts = parts + [jnp.zeros((B, pad, H, hd), parts[0].dtype)]
    kb = jnp.concatenate(parts, axis=1) if len(parts) > 1 else parts[0]
    return kb.reshape(B, -1, MOBA_BLOCK, H, hd)


def moba_attention(q, k_blocks, v_blocks, q_pos, slopes):
    B, T, H, hd = q.shape
    NB = k_blocks.shape[1]
    topk = min(MOBA_TOPK, NB)
    k1 = topk + 1
    f32 = jnp.float32
    k_mean = jnp.mean(k_blocks, axis=2, dtype=f32)
    qc = math.gcd(T, max(1, QROWS // B))
    n_chunks = T // qc
    scale = hd ** -0.5
    bidx = jnp.arange(B)[:, None, None, None]
    hidx = jnp.arange(H)[None, :, None, None]
    blk_ids = jnp.arange(NB)
    rank = jnp.arange(topk)
    offs = jnp.arange(MOBA_BLOCK)

    def one_chunk(args):
        qq, pp = args
        cur = pp // MOBA_BLOCK
        gate = jnp.einsum('bqhd,bnhd->bhqn', qq.astype(f32), k_mean)
        gate = jnp.where(blk_ids[None, None, None, :] < cur[None, None, :, None], gate, -jnp.inf)
        _, sel = lax.top_k(gate, topk)
        sel_ok = jnp.broadcast_to(rank[None, None, None, :] < cur[None, None, :, None], (B, H, qc, topk))
        own = jnp.broadcast_to(cur[None, None, :, None], (B, H, qc, 1))
        blk = jnp.concatenate([sel.astype(jnp.int32), own.astype(jnp.int32)], axis=-1)
        ok = jnp.concatenate([sel_ok, jnp.ones((B, H, qc, 1), bool)], axis=-1)
        kg = k_blocks[bidx, blk, :, hidx, :]
        vg = v_blocks[bidx, blk, :, hidx, :]
        s = jnp.einsum('bqhd,bhqnkd->bhqnk', qq, kg, preferred_element_type=f32) * scale
        dist = pp[None, None, :, None, None] - (blk[..., None] * MOBA_BLOCK + offs)
        s = s - slopes[None, :, None, None, None] * dist.astype(f32)
        s = jnp.where(ok[..., None] & (dist >= 0), s, -jnp.inf)
        p = jax.nn.softmax(s.reshape(B, H, qc, k1 * MOBA_BLOCK), axis=-1)
        return jnp.einsum('bhqm,bhqmd->bqhd', p.astype(vg.dtype),
                          vg.reshape(B, H, qc, k1 * MOBA_BLOCK, hd))

    qs = q.reshape(B, n_chunks, qc, H, hd).transpose(1, 0, 2, 3, 4)
    ps = q_pos.reshape(n_chunks, qc)
    out = lax.map(one_chunk, (qs, ps))
    return out.transpose(1, 0, 2, 3, 4).reshape(B, T, H, hd)


def mlstm_chunk(carry, inp):
    C, n, m = carry
    q, k, v, ig, lf = inp
    L = q.shape[2]
    b = jnp.cumsum(lf, axis=-1)
    dmat = b[..., :, None] - b[..., None, :] + ig[..., None, :]
    causal = jnp.tril(jnp.ones((L, L), bool))
    dmat = jnp.where(causal, dmat, -jnp.inf)
    inter = b + m[..., None]
    mt = jnp.maximum(inter, jnp.max(dmat, axis=-1))
    w = jnp.exp(dmat - mt[..., None])
    a = jnp.exp(inter - mt)
    sc = jnp.einsum('bhtd,bhsd->bhts', q, k) * w
    num = a[..., None] * jnp.einsum('bhvk,bhtk->bhtv', C, q) + jnp.einsum('bhts,bhsv->bhtv', sc, v)
    den = a * jnp.einsum('bhk,bhtk->bht', n, q) + jnp.sum(sc, axis=-1)
    h = num / jnp.maximum(jnp.abs(den), jnp.exp(-mt))[..., None]
    b_last = b[..., -1]
    g = b_last[..., None] - b + ig
    m_new = jnp.maximum(b_last + m, jnp.max(g, axis=-1))
    ws = jnp.exp(g - m_new[..., None])
    a_last = jnp.exp(b_last + m - m_new)
    C_new = a_last[..., None, None] * C + jnp.einsum('bhsv,bhsk->bhvk', v * ws[..., None], k)
    n_new = a_last[..., None] * n + jnp.einsum('bhs,bhsk->bhk', ws, k)
    return (C_new, n_new, m_new), h


def mlstm(q, k, v, ig, lf, C0, n0, m0):
    B, T, NH, d = q.shape
    Lc = math.gcd(T, ML_CHUNK)
    nc = T // Lc
    f32 = jnp.float32

    def seq_chunks(a):
        return a.astype(f32).reshape(B, nc, Lc, NH, -1).transpose(1, 0, 3, 2, 4)

    def gate_chunks(a):
        return a.astype(f32).reshape(B, nc, Lc, NH).transpose(1, 0, 3, 2)

    xs = (seq_chunks(q), seq_chunks(k) * (d ** -0.5), seq_chunks(v), gate_chunks(ig), gate_chunks(lf))
    (C, n, m), hs = lax.scan(mlstm_chunk, (C0.astype(f32), n0.astype(f32), m0.astype(f32)), xs)
    return hs.transpose(1, 0, 3, 2, 4).reshape(B, T, NH, d), C, n, m


def hier_moe(h, w_grp, b_grp, w_exp, b_exp, w_gate, w_up, w_down):
    B, T, D = h.shape
    f32 = jnp.float32
    t = h.reshape(B * T, D)
    gp = jax.nn.softmax((t @ w_grp + b_grp).astype(f32), axis=-1)
    g_top = jnp.argmax(gp, axis=-1)
    g_w = jnp.max(gp, axis=-1)
    el = (t @ w_exp + b_exp).astype(f32).reshape(-1, N_GROUPS, EXPERTS_PER_GROUP)
    el = jnp.take_along_axis(el, g_top[:, None, None], axis=1)[:, 0]
    e_w, e_idx = lax.top_k(jax.nn.softmax(el, axis=-1), EXPERT_TOPK)
    e_w = e_w / jnp.sum(e_w, axis=-1, keepdims=True)
    inner = jnp.sum(jax.nn.one_hot(e_idx, EXPERTS_PER_GROUP, dtype=f32) * e_w[..., None], axis=1)
    gate = (g_w[:, None, None] * jax.nn.one_hot(g_top, N_GROUPS, dtype=f32)[:, :, None]
            * inner[:, None, :]).reshape(-1, N_EXPERTS)
    hg = jnp.einsum('nd,edf->nef', t, w_gate)
    hu = jnp.einsum('nd,edf->nef', t, w_up)
    act = jax.nn.silu(hg) * hu * gate[..., None].astype(t.dtype)
    return jnp.einsum('nef,efd->nd', act, w_down).reshape(B, T, D)


def decoder_layer(x, c, q_pos, k_pool, v_pool, page_table, C0, n0, m0,
                  w_in, b_ig, b_fg, ml_gain, w_out, g_mix, g_ffn, w_mod, b_mod,
                  w_grp, b_grp, w_exp, b_exp, w_gate, w_up, w_down, slopes):
    B, T, _ = x.shape
    f32 = jnp.float32
    mod = (c @ w_mod + b_mod).reshape(B, N_MOD, 1, D_MODEL)
    shift1, scale1, gate1, shift2, scale2, gate2 = (mod[:, i] for i in range(N_MOD))
    h = rms_norm(x, g_mix) * (1.0 + scale1) + shift1
    cuts = [int(s) for s in np.cumsum([ATT_WIDTH] * 3 + [ML_WIDTH] * 4 + [ML_HEADS])]
    qa, ka, va, qm, km, vm, om, ig, fg = jnp.split(h @ w_in, cuts, axis=-1)
    att_shape = (B, T, ATT_HEADS, ATT_HEAD_DIM)
    ml_shape = (B, T, ML_HEADS, ML_HEAD_DIM)
    qa, ka, va = qa.reshape(att_shape), ka.reshape(att_shape), va.reshape(att_shape)
    if page_table is None:
        kb, vb = to_blocks([ka]), to_blocks([va])
    else:
        past_len = page_table.shape[1] * k_pool.shape[1]
        kb = to_blocks([k_pool[page_table].reshape(B, past_len, ATT_HEADS, ATT_HEAD_DIM), ka])
        vb = to_blocks([v_pool[page_table].reshape(B, past_len, ATT_HEADS, ATT_HEAD_DIM), va])
    att = moba_attention(qa, kb, vb, q_pos, slopes).reshape(B, T, ATT_WIDTH)
    lf = jax.nn.log_sigmoid((fg + b_fg).astype(f32))
    hm, C, n, m = mlstm(qm.reshape(ml_shape), km.reshape(ml_shape), vm.reshape(ml_shape),
                        ig + b_ig, lf, C0, n0, m0)
    hm = hm * lax.rsqrt(jnp.mean(hm * hm, axis=-1, keepdims=True) + EPS) \
        * ml_gain.astype(f32).reshape(ML_HEADS, ML_HEAD_DIM)
    hm = hm.reshape(B, T, ML_WIDTH) * jax.nn.sigmoid(om.astype(f32))
    mixed = jnp.concatenate([att, hm.astype(att.dtype)], axis=-1) @ w_out
    x = x + gate1 * mixed
    h2 = rms_norm(x, g_ffn) * (1.0 + scale2) + shift2
    x = x + gate2 * hier_moe(h2, w_grp, b_grp, w_exp, b_exp, w_gate, w_up, w_down)
    return x, ka, va, C, n, m


def setup_inputs(seed: int = 0) -> dict:
    key = jax.random.key(seed)
    ks = jax.random.split(key, 32)
    f32 = jnp.float32
    n_pages = PAST_LEN // PAGE_SIZE
    n_phys = (DEC_BATCH * n_pages * 5) // 4

    def nrm(k, shape, s):
        return s * jax.random.normal(k, shape, f32)

    dn = D_MODEL ** -0.5
    return {
        'x_prompt': nrm(ks[0], (BATCH, SEQ, D_MODEL), 1.0),
        'x_sample': nrm(ks[1], (DEC_BATCH, DEC_SEQ, D_MODEL), 1.0),
        'cache_k': nrm(ks[2], (DEPTH, n_phys, PAGE_SIZE, ATT_HEADS, ATT_HEAD_DIM), 1.0),
        'cache_v': nrm(ks[3], (DEPTH, n_phys, PAGE_SIZE, ATT_HEADS, ATT_HEAD_DIM), 1.0),
        'page_table': jax.random.permutation(ks[4], n_phys)[: DEC_BATCH * n_pages]
                      .reshape(DEC_BATCH, n_pages).astype(jnp.int32),
        'state_C': nrm(ks[5], (DEPTH, DEC_BATCH, ML_HEADS, ML_HEAD_DIM, ML_HEAD_DIM), 0.3),
        'state_n': nrm(ks[6], (DEPTH, DEC_BATCH, ML_HEADS, ML_HEAD_DIM), 0.3),
        'state_m': nrm(ks[7], (DEPTH, DEC_BATCH, ML_HEADS), 0.5),
        'c_prompt': nrm(ks[8], (BATCH, D_MODEL), 1.0),
        'c_sample': nrm(ks[9], (DEC_BATCH, D_MODEL), 1.0),
        'w_in': nrm(ks[10], (DEPTH, D_MODEL, IN_WIDTH), dn),
        'b_ig': nrm(ks[11], (DEPTH, ML_HEADS), 0.1),
        'b_fg': jnp.linspace(3.0, 6.0, ML_HEADS, dtype=f32)[None, :] + nrm(ks[12], (DEPTH, ML_HEADS), 0.1),
        'ml_gain': 1.0 + nrm(ks[13], (DEPTH, ML_WIDTH), 0.02),
        'w_out': nrm(ks[14], (DEPTH, ATT_WIDTH + ML_WIDTH, D_MODEL), (ATT_WIDTH + ML_WIDTH) ** -0.5),
        'g_mix': 1.0 + nrm(ks[15], (DEPTH, D_MODEL), 0.02),
        'g_ffn': 1.0 + nrm(ks[16], (DEPTH, D_MODEL), 0.02),
        'w_mod': nrm(ks[17], (DEPTH, D_MODEL, N_MOD * D_MODEL), 0.5 * dn),
        'b_mod': nrm(ks[18], (DEPTH, N_MOD * D_MODEL), 0.02),
        'w_grp': nrm(ks[19], (DEPTH, D_MODEL, N_GROUPS), dn),
        'b_grp': nrm(ks[20], (DEPTH, N_GROUPS), 0.01),
        'w_exp': nrm(ks[21], (DEPTH, D_MODEL, N_EXPERTS), dn),
        'b_exp': nrm(ks[22], (DEPTH, N_EXPERTS), 0.01),
        'w_gate': nrm(ks[23], (DEPTH, N_EXPERTS, D_MODEL, EXPERT_FF), dn),
        'w_up': nrm(ks[24], (DEPTH, N_EXPERTS, D_MODEL, EXPERT_FF), dn),
        'w_down': nrm(ks[25], (DEPTH, N_EXPERTS, EXPERT_FF, D_MODEL), EXPERT_FF ** -0.5),
        'g_final': 1.0 + nrm(ks[26], (D_MODEL,), 0.02),
    }


def reference(x_prompt, x_sample, cache_k, cache_v, page_table, state_C, state_n, state_m,
              c_prompt, c_sample, w_in, b_ig, b_fg, ml_gain, w_out, g_mix, g_ffn, w_mod, b_mod,
              w_grp, b_grp, w_exp, b_exp, w_gate, w_up, w_down, g_final):
    f32 = jnp.float32
    slopes = alibi_slopes()
    Bp, Tp, _ = x_prompt.shape
    Bd, Td, _ = x_sample.shape
    past_len = page_table.shape[1] * cache_k.shape[2]
    pos_p = jnp.arange(Tp, dtype=jnp.int32)
    pos_d = past_len + jnp.arange(Td, dtype=jnp.int32)
    C0p = jnp.zeros((Bp, ML_HEADS, ML_HEAD_DIM, ML_HEAD_DIM), f32)
    n0p = jnp.zeros((Bp, ML_HEADS, ML_HEAD_DIM), f32)
    m0p = jnp.zeros((Bp, ML_HEADS), f32)
    xp, xd = x_prompt, x_sample
    kps, vps, kds, vds = [], [], [], []
    Cps, nps, mps, Cds, nds, mds = [], [], [], [], [], []
    for l in range(DEPTH):
        w = (w_in[l], b_ig[l], b_fg[l], ml_gain[l], w_out[l], g_mix[l], g_ffn[l], w_mod[l], b_mod[l],
             w_grp[l], b_grp[l], w_exp[l], b_exp[l], w_gate[l], w_up[l], w_down[l])
        xp, kp, vp, Cp, n_p, mp = decoder_layer(xp, c_prompt, pos_p, None, None, None,
                                                C0p, n0p, m0p, *w, slopes)
        xd, kd, vd, Cd, n_d, md = decoder_layer(xd, c_sample, pos_d, cache_k[l], cache_v[l], page_table,
                                                state_C[l], state_n[l], state_m[l], *w, slopes)
        kps.append(kp); vps.append(vp); kds.append(kd); vds.append(vd)
        Cps.append(Cp); nps.append(n_p); mps.append(mp)
        Cds.append(Cd); nds.append(n_d); mds.append(md)
    y_prompt = rms_norm(xp, g_final)
    y_sample = rms_norm(xd, g_final)
    return (y_prompt, y_sample, jnp.stack(kps), jnp.stack(vps), jnp.stack(kds), jnp.stack(vds),
            jnp.stack(Cps), jnp.stack(nps), jnp.stack(mps), jnp.stack(Cds), jnp.stack(nds), jnp.stack(mds))
```

```python
import functools
import math

import jax
import jax.numpy as jnp
from jax import lax
from jax.experimental import pallas as pl
from jax.experimental.pallas import tpu as pltpu

F32 = jnp.float32
BF16 = jnp.bfloat16

MOBA_BLOCK = 256
MOBA_TOPK = 3
EXPERT_TOPK = 2
N_MOD = 6
EPS = 1e-6
NEG = -1e30

LANES = 128
V7X_VMEM_LIMIT = 56 * 1024 * 1024
TOKEN_TILE = 256


def _cparams(sem):
    return pltpu.CompilerParams(dimension_semantics=sem, vmem_limit_bytes=V7X_VMEM_LIMIT)


def _pick_tile(n, candidates):
    for c in candidates:
        if n % c == 0:
            return c
    raise ValueError(f"no tile in {candidates} divides {n}")


def _dot(a, b):
    return jnp.dot(a, b, preferred_element_type=F32)


def _dot_nt(a, b):
    return lax.dot_general(a, b, (((1,), (1,)), ((), ())), preferred_element_type=F32)


def _dot_tn(a, b):
    return lax.dot_general(a, b, (((0,), (0,)), ((), ())), preferred_element_type=F32)


def _split(x):
    hi = x.astype(BF16)
    lo = (x - hi.astype(F32)).astype(BF16)
    return hi, lo


def _dot3(a, b, dot=_dot):
    a_hi, a_lo = _split(a)
    b_hi, b_lo = _split(b)
    return dot(a_hi, b_hi) + dot(a_lo, b_hi) + dot(a_hi, b_lo)


def _sigmoid(x):
    return 1.0 / (1.0 + jnp.exp(-x))


def _log_sigmoid(x):
    return jnp.minimum(x, 0.0) - jnp.log(1.0 + jnp.exp(-jnp.abs(x)))


def _rms(x):
    return x * lax.rsqrt(jnp.mean(x * x, axis=-1, keepdims=True) + EPS)


def _mod_kernel(c_ref, w_ref, b_ref, o_ref):
    o_ref[...] = _dot3(c_ref[...], w_ref[...]) + b_ref[...]


def _modulation(c_all, w_mod, b_mod):
    rows, d = c_all.shape
    n = w_mod.shape[1]
    tn = _pick_tile(n, (1024, 512, 256, 128))
    return pl.pallas_call(
        _mod_kernel,
        grid=(n // tn,),
        in_specs=[
            pl.BlockSpec((rows, d), lambda j: (0, 0)),
            pl.BlockSpec((d, tn), lambda j: (0, j)),
            pl.BlockSpec((1, tn), lambda j: (0, j)),
        ],
        out_specs=pl.BlockSpec((rows, tn), lambda j: (0, j)),
        out_shape=jax.ShapeDtypeStruct((rows, n), F32),
        compiler_params=_cparams(("arbitrary",)),
        name="modulation",
    )(c_all, w_mod, b_mod.reshape(1, n))


def _p_rows(n_p):
    return lambda i: (jnp.minimum(i, n_p - 1), 0)


def _p_batch(n_p, tiles_per_batch):
    return lambda i: (jnp.minimum(i, n_p - 1) // tiles_per_batch, 0, 0)


def _s_rows(n_p):
    return lambda i: (jnp.maximum(i - n_p, 0), 0)


def _dual(n_p, body, p_refs, s_refs):
    i = pl.program_id(0)

    @pl.when(i < n_p)
    def _():
        body(*p_refs)

    @pl.when(i >= n_p)
    def _():
        body(*s_refs)


def _norm_kernel(n_p, n_heads, xp, xs, shp, shs, scp, scs, g_ref, wg_ref, bg_ref, h_ref, gates_ref):
    def body(x_ref, sh_ref, sc_ref):
        h = _rms(x_ref[...]) * g_ref[...] * (1.0 + sc_ref[...]) + sh_ref[...]
        h_ref[...] = h.astype(BF16)
        g = _dot3(h, wg_ref[...]) + bg_ref[...]
        lane = lax.broadcasted_iota(jnp.int32, g.shape, 1)
        gates_ref[...] = jnp.where(lane < n_heads, g, _log_sigmoid(g))

    _dual(n_p, body, (xp, shp, scp), (xs, shs, scs))


def _norm_stage(x_p, x_s, sh_p, sh_s, sc_p, sc_s, g_mix, w_gates, b_gates, n_heads, t_p):
    tm = TOKEN_TILE
    d = x_p.shape[1]
    n_p, n_s = x_p.shape[0] // tm, x_s.shape[0] // tm
    n_tok = x_p.shape[0] + x_s.shape[0]
    row = pl.BlockSpec((tm, d), _p_rows(n_p))
    srow = pl.BlockSpec((tm, d), _s_rows(n_p))
    pbat = pl.BlockSpec((None, 1, d), _p_batch(n_p, t_p // tm))
    const = lambda shape: pl.BlockSpec(shape, lambda i: (0,) * len(shape))
    return pl.pallas_call(
        functools.partial(_norm_kernel, n_p, n_heads),
        grid=(n_p + n_s,),
        in_specs=[row, srow, pbat, srow, pbat, srow, const((1, d)), const((d, LANES)), const((1, LANES))],
        out_specs=[pl.BlockSpec((tm, d), lambda i: (i, 0)), pl.BlockSpec((tm, LANES), lambda i: (i, 0))],
        out_shape=[jax.ShapeDtypeStruct((n_tok, d), BF16), jax.ShapeDtypeStruct((n_tok, LANES), F32)],
        compiler_params=_cparams(("arbitrary",)),
        name="norm_gates",
    )(x_p, x_s, sh_p, sh_s, sc_p, sc_s, g_mix, w_gates, b_gates)


def _inproj_kernel(h_ref, w_ref, o_ref, wbf_ref):
    @pl.when(pl.program_id(1) == 0)
    def _():
        wbf_ref[...] = w_ref[...].astype(BF16)

    o_ref[...] = _dot(h_ref[...], wbf_ref[...])


def _in_projection(h_all, w_in, width, tn):
    n_tok, d = h_all.shape
    tm = _pick_tile(n_tok, (768, 512, 256))
    return pl.pallas_call(
        _inproj_kernel,
        grid=(width // tn, n_tok // tm),
        in_specs=[pl.BlockSpec((tm, d), lambda j, i: (i, 0)), pl.BlockSpec((d, tn), lambda j, i: (0, j))],
        out_specs=pl.BlockSpec((tm, tn), lambda j, i: (i, j)),
        out_shape=jax.ShapeDtypeStruct((n_tok, width), F32),
        scratch_shapes=[pltpu.VMEM((d, tn), BF16)],
        compiler_params=_cparams(("arbitrary", "arbitrary")),
        name="in_projection",
    )(h_all, w_in)


def _attn_prompt_kernel(slopes_ref, q_ref, k_ref, v_ref, o_ref, selt_ref, sel_ref):
    t, hd = q_ref.shape
    blk = MOBA_BLOCK
    nb = t // blk
    slope = slopes_ref[pl.program_id(1)]
    scale = hd ** -0.5

    kmean = jnp.mean(k_ref[...].reshape(nb, blk, hd), axis=1)
    gate = _dot3(kmean, q_ref[...], _dot_nt)
    kb = lax.broadcasted_iota(jnp.int32, (nb, t), 0)
    qb = lax.broadcasted_iota(jnp.int32, (nb, t), 1) // blk
    valid = kb < qb
    g = jnp.where(valid, gate, -jnp.inf)
    rank = jnp.zeros((nb, t), jnp.int32)
    for n2 in range(nb):
        gn = g[n2:n2 + 1, :]
        rank += ((gn > g) | ((gn == g) & (n2 < kb))).astype(jnp.int32)
    allowed = ((valid & (rank < MOBA_TOPK)) | (kb == qb)).astype(F32)
    selt_ref[...] = jnp.concatenate([allowed, jnp.zeros((LANES - nb, t), F32)], axis=0).T
    for n in range(nb):
        sel_ref[n] = selt_ref[:, n:n + 1]

    r = lax.broadcasted_iota(jnp.int32, (blk, blk), 0)
    c = lax.broadcasted_iota(jnp.int32, (blk, blk), 1)
    alibi_rel = slope * (r - c).astype(F32)
    causal = r >= c

    for i in range(nb):
        rows = pl.ds(i * blk, blk)
        qt = (q_ref[rows, :] * scale).astype(BF16)
        s = _dot_nt(qt, k_ref[rows, :].astype(BF16)) - alibi_rel
        s = jnp.where(causal, s, NEG)
        m = jnp.max(s, axis=1, keepdims=True)
        p = jnp.exp(s - m)
        l = jnp.sum(p, axis=1, keepdims=True)
        acc = _dot(p.astype(BF16), v_ref[rows, :].astype(BF16))

        def past_block(j, carry, i=i, qt=qt, rows=rows):
            m, l, acc = carry
            krows = pl.ds(pl.multiple_of(j * blk, blk), blk)
            block_dist = ((i - j) * blk).astype(F32)
            row_bias = jnp.where(sel_ref[j, rows, :] > 0.0, -slope * block_dist, NEG)
            s = _dot_nt(qt, k_ref[krows, :].astype(BF16)) - alibi_rel + row_bias
            m_new = jnp.maximum(m, jnp.max(s, axis=1, keepdims=True))
            alpha = jnp.exp(m - m_new)
            p = jnp.exp(s - m_new)
            l = alpha * l + jnp.sum(p, axis=1, keepdims=True)
            acc = alpha * acc + _dot(p.astype(BF16), v_ref[krows, :].astype(BF16))
            return m_new, l, acc

        if i > 0:
            m, l, acc = lax.fori_loop(0, i, past_block, (m, l, acc))
        o_ref[rows, :] = (acc / l).astype(o_ref.dtype)


def _attn_prompt(proj, slopes, batch, t, n_heads, hd):
    def col(off):
        return pl.BlockSpec((t, hd), lambda b, h, s: (b, off + h))

    grid_spec = pltpu.PrefetchScalarGridSpec(
        num_scalar_prefetch=1,
        grid=(batch, n_heads),
        in_specs=[col(0), col(n_heads), col(2 * n_heads)],
        out_specs=pl.BlockSpec((t, hd), lambda b, h, s: (b, h)),
        scratch_shapes=[pltpu.VMEM((t, LANES), F32), pltpu.VMEM((t // MOBA_BLOCK, t, 1), F32)],
    )
    return pl.pallas_call(
        _attn_prompt_kernel,
        grid_spec=grid_spec,
        out_shape=jax.ShapeDtypeStruct((batch * t, n_heads * hd), BF16),
        compiler_params=_cparams(("arbitrary", "arbitrary")),
        name="attn_prompt",
    )(slopes, proj, proj, proj)


def _attn_decode_kernel(pt_ref, q_ref, k0_ref, k1_ref, v0_ref, v1_ref, kn_ref, vn_ref, slope_ref, qi_ref,
                        o_ref, qbd_ref, qbf_ref, ksum_ref, mall_ref, lall_ref, acc_ref,
                        *, n_heads, hd, past_len):
    del pt_ref
    n = pl.program_id(1)
    nbp = pl.num_programs(1)
    tq, width = q_ref.shape
    rows = n_heads * tq
    blk = MOBA_BLOCK
    slope = slope_ref[...]
    qi = qi_ref[...]

    def head_diag(o):
        return jnp.concatenate([o[h * tq:(h + 1) * tq, h * hd:(h + 1) * hd] for h in range(n_heads)], axis=0)

    @pl.when(n == 0)
    def _():
        qrep = jnp.concatenate([q_ref[...]] * n_heads, axis=0) * (hd ** -0.5)
        rh = lax.broadcasted_iota(jnp.int32, (rows, width), 0) // tq
        ch = lax.broadcasted_iota(jnp.int32, (rows, width), 1) // hd
        qbd = jnp.where(rh == ch, qrep, 0.0)
        qbd_ref[...] = qbd
        qbf_ref[...] = qbd.astype(BF16)
        ksum_ref[...] = jnp.zeros_like(ksum_ref)
        mall_ref[...] = jnp.full_like(mall_ref, NEG)
        lall_ref[...] = jnp.zeros_like(lall_ref)

    k = jnp.concatenate([k0_ref[...], k1_ref[...]], axis=0)
    v = jnp.concatenate([v0_ref[...], v1_ref[...]], axis=0)
    ksum_ref[pl.ds(n, 1), :] = jnp.sum(k, axis=0, keepdims=True)
    s = _dot_nt(qbf_ref[...], k.astype(BF16))
    col = lax.broadcasted_iota(jnp.int32, (rows, blk), 1)
    dist = (past_len - n * blk - col).astype(F32) + qi
    s = s - slope * dist
    m = jnp.max(s, axis=1, keepdims=True)
    p = jnp.exp(s - m)
    l = jnp.sum(p, axis=1, keepdims=True)
    acc_ref[n] = head_diag(_dot(p.astype(BF16), v.astype(BF16)))
    lane = lax.broadcasted_iota(jnp.int32, (rows, LANES), 1)
    mall_ref[...] = jnp.where(lane == n, m, mall_ref[...])
    lall_ref[...] = jnp.where(lane == n, l, lall_ref[...])

    @pl.when(n == nbp - 1)
    def _():
        n_past = acc_ref.shape[0]
        gate = _dot3(qbd_ref[...], ksum_ref[...] * (1.0 / blk), _dot_nt)
        valid = lane < n_past
        g = jnp.where(valid, gate, -jnp.inf)
        rank = jnp.zeros((rows, LANES), jnp.int32)
        for n2 in range(n_past):
            gn = g[:, n2:n2 + 1]
            rank += ((gn > g) | ((gn == g) & (n2 < lane))).astype(jnp.int32)
        sel = valid & (rank < MOBA_TOPK)

        pad = jnp.zeros((16 - tq, width), F32)
        kn = jnp.concatenate([kn_ref[...], pad], axis=0).astype(BF16)
        vn = jnp.concatenate([vn_ref[...], pad], axis=0).astype(BF16)
        d_own = qi - lax.broadcasted_iota(jnp.int32, (rows, 16), 1).astype(F32)
        s_own = jnp.where(d_own >= 0.0, _dot_nt(qbf_ref[...], kn) - slope * d_own, NEG)
        m_own = jnp.max(s_own, axis=1, keepdims=True)
        p_own = jnp.exp(s_own - m_own)
        l_own = jnp.sum(p_own, axis=1, keepdims=True)
        acc_own = head_diag(_dot(p_own.astype(BF16), vn))

        m_all = mall_ref[...]
        m_fin = jnp.maximum(jnp.max(jnp.where(sel, m_all, NEG), axis=1, keepdims=True), m_own)
        wgt = jnp.where(sel, jnp.exp(m_all - m_fin), 0.0)
        w_own = jnp.exp(m_own - m_fin)
        l_fin = jnp.sum(wgt * lall_ref[...], axis=1, keepdims=True) + w_own * l_own
        out = w_own * acc_own
        for n2 in range(n_past):
            out += wgt[:, n2:n2 + 1] * acc_ref[n2]
        out = out / l_fin
        o_ref[...] = jnp.concatenate([out[h * tq:(h + 1) * tq, :] for h in range(n_heads)], axis=1)


def _attn_decode(proj, cache_k, cache_v, page_table, slope_rows, qi_rows, batch, tq, n_heads, hd, row0):
    n_phys, page, _, _ = cache_k.shape
    width = n_heads * hd
    assert 2 * page == MOBA_BLOCK, "a MoBA block must span exactly two cache pages"
    assert tq <= 16 and row0 % tq == 0
    n_pages = page_table.shape[1]
    n_past = n_pages // 2
    assert n_past <= LANES
    rows = n_heads * tq
    ck = cache_k.reshape(n_phys, page, width)
    cv = cache_v.reshape(n_phys, page, width)

    def page_spec(which):
        return pl.BlockSpec((None, page, width), lambda b, n, pt: (pt[b * n_pages + 2 * n + which], 0, 0))

    def new_spec(colblock):
        return pl.BlockSpec((tq, width), lambda b, n, pt: (row0 // tq + b, colblock))

    const = pl.BlockSpec((rows, 1), lambda b, n, pt: (0, 0))
    grid_spec = pltpu.PrefetchScalarGridSpec(
        num_scalar_prefetch=1,
        grid=(batch, n_past),
        in_specs=[new_spec(0), page_spec(0), page_spec(1), page_spec(0), page_spec(1),
                  new_spec(1), new_spec(2), const, const],
        out_specs=pl.BlockSpec((tq, width), lambda b, n, pt: (b, 0)),
        scratch_shapes=[
            pltpu.VMEM((rows, width), F32), pltpu.VMEM((rows, width), BF16), pltpu.VMEM((LANES, width), F32),
            pltpu.VMEM((rows, LANES), F32), pltpu.VMEM((rows, LANES), F32), pltpu.VMEM((n_past, rows, hd), F32),
        ],
    )
    return pl.pallas_call(
        functools.partial(_attn_decode_kernel, n_heads=n_heads, hd=hd, past_len=n_pages * page),
        grid_spec=grid_spec,
        out_shape=jax.ShapeDtypeStruct((batch * tq, width), F32),
        compiler_params=_cparams(("arbitrary", "arbitrary")),
        name="attn_decode",
    )(page_table.reshape(-1), proj, ck, ck, cv, cv, proj, proj, slope_rows, qi_rows)


def _mlstm_kernel(q_ref, k_ref, v_ref, og_ref, gcol_ref, grow_ref, c0_ref, n0_ref, m0_ref, gain_ref,
                  hm_ref, c_ref, n_ref, m_ref, *, chunk):
    t, d = q_ref.shape
    lc = chunk
    c_ref[...] = c0_ref[...]
    n_ref[...] = n0_ref[...]
    m_ref[...] = m0_ref[...]
    r = lax.broadcasted_iota(jnp.int32, (lc, lc), 0)
    c = lax.broadcasted_iota(jnp.int32, (lc, lc), 1)
    lower = r >= c

    def one_chunk(rows):
        q = q_ref[rows, :]
        k = k_ref[rows, :] * (d ** -0.5)
        v = v_ref[rows, :]
        gc = gcol_ref[rows, :]
        ig_c, lf_c = gc[:, 0:1], gc[:, 1:2]
        gr = grow_ref[:, rows]
        ig_r, lf_r = gr[0:1, :], gr[1:2, :]
        cmat, nvec, m0 = c_ref[...], n_ref[...], m_ref[...]

        b_c = jnp.sum(jnp.where(lower, lf_r, 0.0), axis=1, keepdims=True)
        b_r = jnp.sum(jnp.where(r <= c, lf_c, 0.0), axis=0, keepdims=True)
        dmat = jnp.where(lower, b_c - b_r + ig_r, NEG)
        inter = b_c + m0
        mt = jnp.maximum(inter, jnp.max(dmat, axis=1, keepdims=True))
        w = jnp.exp(dmat - mt)
        a = jnp.exp(inter - mt)
        qb, kb, vb = q.astype(BF16), k.astype(BF16), v.astype(BF16)
        sc = _dot_nt(qb, kb) * w
        num = a * _dot_nt(qb, cmat.astype(BF16)) + _dot(sc.astype(BF16), vb)
        den = a * jnp.sum(q * nvec, axis=1, keepdims=True) + jnp.sum(sc, axis=1, keepdims=True)
        h = num / jnp.maximum(jnp.abs(den), jnp.exp(-mt))
        hn = _rms(h) * gain_ref[...]
        hm_ref[rows, :] = (hn * _sigmoid(og_ref[rows, :])).astype(hm_ref.dtype)

        b_last = b_c[lc - 1:lc, :]
        g = b_last - b_c + ig_c
        m_new = jnp.maximum(b_last + m0, jnp.max(g, axis=0, keepdims=True))
        ws = jnp.exp(g - m_new)
        a_last = jnp.exp(b_last + m0 - m_new)
        c_ref[...] = a_last * cmat + _dot_tn((v * ws).astype(BF16), kb)
        n_ref[...] = a_last * nvec + jnp.sum(ws * k, axis=0, keepdims=True)
        m_ref[...] = m_new

    if t == lc:
        one_chunk(pl.ds(0, lc))
    else:
        def body(ci, carry):
            one_chunk(pl.ds(pl.multiple_of(ci * lc, lc), lc))
            return carry

        lax.fori_loop(0, t // lc, body, 0)


def _mlstm(proj, gcol, grow, c0, n0, m0, gain, batch, t, n_heads, d, row0, col0, out_dtype):
    lc = math.gcd(t, MOBA_BLOCK)
    rb = row0 // t
    cb = col0 // d
    assert row0 % t == 0 and col0 % d == 0

    def col(off):
        return pl.BlockSpec((t, d), lambda b, h: (rb + b, cb + off * n_heads + h))

    per_head = lambda shape: pl.BlockSpec((None, None) + shape, lambda b, h: (b, h, 0, 0))
    return pl.pallas_call(
        functools.partial(_mlstm_kernel, chunk=lc),
        grid=(batch, n_heads),
        in_specs=[col(0), col(1), col(2), col(3), per_head((t, 2)), per_head((2, t)),
                  per_head((d, d)), per_head((1, d)), per_head((1, 1)),
                  pl.BlockSpec((None, 1, d), lambda b, h: (h, 0, 0))],
        out_specs=[pl.BlockSpec((t, d), lambda b, h: (b, h)), per_head((d, d)), per_head((1, d)), per_head((1, 1))],
        out_shape=[jax.ShapeDtypeStruct((batch * t, n_heads * d), out_dtype),
                   jax.ShapeDtypeStruct((batch, n_heads, d, d), F32),
                   jax.ShapeDtypeStruct((batch, n_heads, 1, d), F32),
                   jax.ShapeDtypeStruct((batch, n_heads, 1, 1), F32)],
        compiler_params=_cparams(("arbitrary", "arbitrary")),
        name=f"mlstm_t{t}",
    )(proj, proj, proj, proj, gcol, grow, c0, n0, m0, gain)


def _route(logits, n_groups, n_experts):
    epg = n_experts // n_groups
    lane = lax.broadcasted_iota(jnp.int32, logits.shape, 1).astype(F32)
    big = float(LANES)

    def rmax(x):
        return jnp.max(x, axis=1, keepdims=True)

    def first_lane(mask):
        return jnp.min(jnp.where(mask, lane, big), axis=1, keepdims=True)

    is_g = (lane >= n_experts) & (lane < n_experts + n_groups)
    gmax = rmax(jnp.where(is_g, logits, -jnp.inf))
    g_w = 1.0 / jnp.sum(jnp.where(is_g, jnp.exp(logits - gmax), 0.0), axis=1, keepdims=True)
    lo = (first_lane(is_g & (logits == gmax)) - n_experts) * epg
    in_grp = (lane >= lo) & (lane < lo + epg)
    emax = rmax(jnp.where(in_grp, logits, -jnp.inf))
    pe = jnp.where(in_grp, jnp.exp(logits - emax), 0.0)
    prob = pe / jnp.sum(pe, axis=1, keepdims=True)
    p1 = rmax(jnp.where(in_grp, prob, -1.0))
    i1 = first_lane(in_grp & (prob == p1))
    rest = in_grp & (lane != i1)
    p2 = rmax(jnp.where(rest, prob, -1.0))
    i2 = first_lane(rest & (prob == p2))
    den = p1 + p2
    return g_w * jnp.where(lane == i1, p1 / den, jnp.where(lane == i2, p2 / den, 0.0))


def _outproj_kernel(n_p, att_w, n_groups, n_experts,
                    attp, atts, hmp, hms, xp, xs, g1p, g1s, sh2p, sh2s, sc2p, sc2s,
                    wo_ref, gffn_ref, wr_ref, br_ref, x1_ref, h2_ref, gate_ref):
    def body(att_ref, hm_ref, x_ref, g1_ref, sh2_ref, sc2_ref):
        mixed = (_dot(att_ref[...].astype(BF16), wo_ref[0:att_w, :])
                 + _dot(hm_ref[...].astype(BF16), wo_ref[att_w:, :]))
        x1 = x_ref[...] + g1_ref[...] * mixed
        x1_ref[...] = x1
        h2 = _rms(x1) * gffn_ref[...] * (1.0 + sc2_ref[...]) + sh2_ref[...]
        h2_ref[...] = h2.astype(BF16)
        gate_ref[...] = _route(_dot3(h2, wr_ref[...]) + br_ref[...], n_groups, n_experts)

    _dual(n_p, body, (attp, hmp, xp, g1p, sh2p, sc2p), (atts, hms, xs, g1s, sh2s, sc2s))


def _outproj_stage(att_p, att_s, hm_p, hm_s, x_p, x_s, g1_p, g1_s, sh2_p, sh2_s, sc2_p, sc2_s,
                   w_out_bf, g_ffn, w_route, b_route, n_groups, n_experts, t_p):
    tm = TOKEN_TILE
    d = x_p.shape[1]
    att_w = att_p.shape[1]
    ml_w = hm_p.shape[1]
    n_p, n_s = x_p.shape[0] // tm, x_s.shape[0] // tm
    n_tok = x_p.shape[0] + x_s.shape[0]
    prow = lambda w: pl.BlockSpec((tm, w), _p_rows(n_p))
    srow = lambda w: pl.BlockSpec((tm, w), _s_rows(n_p))
    pbat = pl.BlockSpec((None, 1, d), _p_batch(n_p, t_p // tm))
    const = lambda shape: pl.BlockSpec(shape, lambda i: (0,) * len(shape))
    out_row = lambda w: pl.BlockSpec((tm, w), lambda i: (i, 0))
    return pl.pallas_call(
        functools.partial(_outproj_kernel, n_p, att_w, n_groups, n_experts),
        grid=(n_p + n_s,),
        in_specs=[prow(att_w), srow(att_w), prow(ml_w), srow(ml_w), prow(d), srow(d),
                  pbat, srow(d), pbat, srow(d), pbat, srow(d),
                  const((att_w + ml_w, d)), const((1, d)), const((d, LANES)), const((1, LANES))],
        out_specs=[out_row(d), out_row(d), out_row(LANES)],
        out_shape=[jax.ShapeDtypeStruct((n_tok, d), F32), jax.ShapeDtypeStruct((n_tok, d), BF16),
                   jax.ShapeDtypeStruct((n_tok, LANES), F32)],
        compiler_params=_cparams(("arbitrary",)),
        name="out_projection_router",
    )(att_p, att_s, hm_p, hm_s, x_p, x_s, g1_p, g1_s, sh2_p, sh2_s, sc2_p, sc2_s,
      w_out_bf, g_ffn, w_route, b_route)


def _moe_kernel(h2_ref, gate_ref, wg_ref, wu_ref, wd_ref, o_ref):
    e = pl.program_id(1)

    @pl.when(e == 0)
    def _():
        o_ref[...] = jnp.zeros_like(o_ref)

    h2 = h2_ref[...]
    hg = _dot(h2, wg_ref[...])
    hu = _dot(h2, wu_ref[...])
    gate = gate_ref[...]
    lane = lax.broadcasted_iota(jnp.int32, gate.shape, 1)
    ge = jnp.sum(jnp.where(lane == e, gate, 0.0), axis=1, keepdims=True)
    act = hg * _sigmoid(hg) * hu * ge
    o_ref[...] += _dot(act.astype(BF16), wd_ref[...])


def _moe_stage(h2, gate, w_gate_bf, w_up_bf, w_down_bf):
    n_tok, d = h2.shape
    n_experts, _, ff = w_gate_bf.shape
    tm = _pick_tile(n_tok, (768, 512, 256))
    return pl.pallas_call(
        _moe_kernel,
        grid=(n_tok // tm, n_experts),
        in_specs=[pl.BlockSpec((tm, d), lambda i, e: (i, 0)), pl.BlockSpec((tm, LANES), lambda i, e: (i, 0)),
                  pl.BlockSpec((None, d, ff), lambda i, e: (e, 0, 0)),
                  pl.BlockSpec((None, d, ff), lambda i, e: (e, 0, 0)),
                  pl.BlockSpec((None, ff, d), lambda i, e: (e, 0, 0))],
        out_specs=pl.BlockSpec((tm, d), lambda i, e: (i, 0)),
        out_shape=jax.ShapeDtypeStruct((n_tok, d), F32),
        compiler_params=_cparams(("arbitrary", "arbitrary")),
        name="moe_experts",
    )(h2, gate, w_gate_bf, w_up_bf, w_down_bf)


def _final_kernel(n_p, x1_ref, moe_ref, g2p, g2s, gfin_ref, yp_ref, ys_ref):
    def body(g2_ref, y_ref):
        x2 = x1_ref[...] + g2_ref[...] * moe_ref[...]
        y_ref[...] = _rms(x2) * gfin_ref[...]

    _dual(n_p, body, (g2p, yp_ref), (g2s, ys_ref))


def _final_stage(x1, moe, g2_p, g2_s, g_final, n_tok_p, t_p):
    tm = TOKEN_TILE
    n_tok, d = x1.shape
    n_p = n_tok_p // tm
    n_s = (n_tok - n_tok_p) // tm
    row = pl.BlockSpec((tm, d), lambda i: (i, 0))
    return pl.pallas_call(
        functools.partial(_final_kernel, n_p),
        grid=(n_p + n_s,),
        in_specs=[row, row, pl.BlockSpec((None, 1, d), _p_batch(n_p, t_p // tm)),
                  pl.BlockSpec((tm, d), _s_rows(n_p)), pl.BlockSpec((1, d), lambda i: (0, 0))],
        out_specs=[pl.BlockSpec((tm, d), _p_rows(n_p)), pl.BlockSpec((tm, d), _s_rows(n_p))],
        out_shape=[jax.ShapeDtypeStruct((n_tok_p, d), F32), jax.ShapeDtypeStruct((n_tok - n_tok_p, d), F32)],
        compiler_params=_cparams(("arbitrary",)),
        name="final_norm",
    )(x1, moe, g2_p, g2_s, g_final)


def kernel(x_prompt, x_sample, cache_k, cache_v, page_table, state_C, state_n, state_m, c_prompt, c_sample,
           w_in, b_ig, b_fg, ml_gain, w_out, g_mix, g_ffn, w_mod, b_mod, w_grp, b_grp, w_exp, b_exp,
           w_gate, w_up, w_down, g_final):
    bp, tp, d = x_prompt.shape
    bd, td, _ = x_sample.shape
    depth = w_in.shape[0]
    assert depth == 1, "one trunk layer"
    n_heads, hd = cache_k.shape[-2:]
    att_w = n_heads * hd
    nh_ml, d_ml = state_n.shape[-2:]
    ml_w = nh_ml * d_ml
    assert att_w == ml_w and att_w + ml_w == d
    n_groups, n_experts = w_grp.shape[-1], w_exp.shape[-1]
    assert n_experts + n_groups <= LANES and 2 * nh_ml <= LANES
    main_w = 3 * att_w + 4 * ml_w
    ntp, nts = bp * tp, bd * td
    past_len = page_table.shape[1] * cache_k.shape[2]

    c_rows = bp + bd
    c_pad = -c_rows % 16
    c_all = jnp.concatenate([c_prompt, c_sample, jnp.zeros((c_pad, d), F32)], axis=0)
    mod = _modulation(c_all, w_mod[0], b_mod[0])
    mod_p = mod[:bp].reshape(bp, N_MOD, 1, d)
    mod_s = jnp.repeat(mod[bp:c_rows].reshape(bd, N_MOD, d), td, axis=0)
    shift1_p, scale1_p, gate1_p, shift2_p, scale2_p, gate2_p = (mod_p[:, i] for i in range(N_MOD))
    shift1_s, scale1_s, gate1_s, shift2_s, scale2_s, gate2_s = (mod_s[:, i] for i in range(N_MOD))

    x_p = x_prompt.reshape(ntp, d)
    x_s = x_sample.reshape(nts, d)

    w_gates = jnp.pad(w_in[0][:, main_w:], ((0, 0), (0, LANES - 2 * nh_ml)))
    b_gates = jnp.pad(jnp.concatenate([b_ig[0], b_fg[0]]), (0, LANES - 2 * nh_ml)).reshape(1, LANES)
    h_all, gates = _norm_stage(x_p, x_s, shift1_p, shift1_s, scale1_p, scale1_s, g_mix[0].reshape(1, d),
                               w_gates, b_gates, nh_ml, tp)

    proj = _in_projection(h_all, w_in[0], main_w, att_w)

    slopes = 2.0 ** (-8.0 * jnp.arange(1, n_heads + 1, dtype=F32) / n_heads)
    att_p = _attn_prompt(proj, slopes, bp, tp, n_heads, hd)
    slope_rows = jnp.repeat(slopes, td).reshape(n_heads * td, 1)
    qi_rows = jnp.tile(jnp.arange(td, dtype=F32), n_heads).reshape(n_heads * td, 1)
    att_s = _attn_decode(proj, cache_k[0], cache_v[0], page_table, slope_rows, qi_rows, bd, td, n_heads, hd, ntp)

    def gate_layouts(g, b, t):
        g = g[:, :2 * nh_ml].reshape(b, t, 2, nh_ml)
        return g.transpose(0, 3, 1, 2), g.transpose(0, 3, 2, 1)

    gcol_p, grow_p = gate_layouts(gates[:ntp], bp, tp)
    gcol_s, grow_s = gate_layouts(gates[ntp:], bd, td)
    gain = ml_gain[0].reshape(nh_ml, 1, d_ml)
    zeros = lambda *s: jnp.zeros(s, F32)
    hm_p, c_p, n_p, m_p = _mlstm(proj, gcol_p, grow_p, zeros(bp, nh_ml, d_ml, d_ml), zeros(bp, nh_ml, 1, d_ml),
                                 zeros(bp, nh_ml, 1, 1), gain, bp, tp, nh_ml, d_ml, 0, 3 * att_w, BF16)
    hm_s, c_s, n_s, m_s = _mlstm(proj, gcol_s, grow_s, state_C[0], state_n[0].reshape(bd, nh_ml, 1, d_ml),
                                 state_m[0].reshape(bd, nh_ml, 1, 1), gain, bd, td, nh_ml, d_ml, ntp, 3 * att_w, F32)

    w_route = jnp.pad(jnp.concatenate([w_exp[0], w_grp[0]], axis=1), ((0, 0), (0, LANES - n_experts - n_groups)))
    b_route = jnp.pad(jnp.concatenate([b_exp[0], b_grp[0]]), (0, LANES - n_experts - n_groups)).reshape(1, LANES)
    x1, h2, gate = _outproj_stage(att_p, att_s, hm_p, hm_s, x_p, x_s, gate1_p, gate1_s, shift2_p, shift2_s,
                                  scale2_p, scale2_s, w_out[0].astype(BF16), g_ffn[0].reshape(1, d),
                                  w_route, b_route, n_groups, n_experts, tp)

    moe = _moe_stage(h2, gate, w_gate[0].astype(BF16), w_up[0].astype(BF16), w_down[0].astype(BF16))
    y_p, y_s = _final_stage(x1, moe, gate2_p, gate2_s, g_final.reshape(1, d), ntp, tp)

    kv_shape_p = (1, bp, tp, n_heads, hd)
    kv_shape_s = (1, bd, td, n_heads, hd)
    return (y_p.reshape(bp, tp, d), y_s.reshape(bd, td, d),
            proj[:ntp, att_w:2 * att_w].reshape(kv_shape_p), proj[:ntp, 2 * att_w:3 * att_w].reshape(kv_shape_p),
            proj[ntp:, att_w:2 * att_w].reshape(kv_shape_s), proj[ntp:, 2 * att_w:3 * att_w].reshape(kv_shape_s),
            c_p[None], n_p.reshape(1, bp, nh_ml, d_ml), m_p.reshape(1, bp, nh_ml),
            c_s[None], n_s.reshape(1, bd, nh_ml, d_ml), m_s.reshape(1, bd, nh_ml))
```

```python
import functools
import math

import jax
import jax.numpy as jnp
from jax import lax
from jax.experimental import pallas as pl
from jax.experimental.pallas import tpu as pltpu

F32 = jnp.float32
BF16 = jnp.bfloat16

MOBA_BLOCK = 256
MOBA_TOPK = 3
EXPERT_TOPK = 2
N_MOD = 6
EPS = 1e-6
NEG = -1e30

LANES = 128
V7X_VMEM_LIMIT = 56 * 1024 * 1024
TOKEN_TILE = 256


def _cparams(sem):
    return pltpu.CompilerParams(dimension_semantics=sem, vmem_limit_bytes=V7X_VMEM_LIMIT)


def _pick_tile(n, candidates):
    for c in candidates:
        if n % c == 0:
            return c
    raise ValueError(f"no tile in {candidates} divides {n}")


def _dot(a, b):
    return jnp.dot(a, b, preferred_element_type=F32)


def _dot_nt(a, b):
    return lax.dot_general(a, b, (((1,), (1,)), ((), ())), preferred_element_type=F32)


def _dot_tn(a, b):
    return lax.dot_general(a, b, (((0,), (0,)), ((), ())), preferred_element_type=F32)


def _split(x):
    hi = x.astype(BF16)
    lo = (x - hi.astype(F32)).astype(BF16)
    return hi, lo


def _dot3(a, b, dot=_dot):
    a_hi, a_lo = _split(a)
    b_hi, b_lo = _split(b)
    return dot(a_hi, b_hi) + dot(a_lo, b_hi) + dot(a_hi, b_lo)


def _sigmoid(x):
    return 1.0 / (1.0 + jnp.exp(-x))


def _log_sigmoid(x):
    return jnp.minimum(x, 0.0) - jnp.log(1.0 + jnp.exp(-jnp.abs(x)))


def _rms(x):
    return x * lax.rsqrt(jnp.mean(x * x, axis=-1, keepdims=True) + EPS)


def _mod_kernel(c_ref, w_ref, b_ref, o_ref):
    o_ref[...] = _dot3(c_ref[...], w_ref[...]) + b_ref[...]


def _modulation(c_all, w_mod, b_mod):
    rows, d = c_all.shape
    n = w_mod.shape[1]
    tn = _pick_tile(n, (1024, 512, 256, 128))
    return pl.pallas_call(
        _mod_kernel,
        grid=(n // tn,),
        in_specs=[
            pl.BlockSpec((rows, d), lambda j: (0, 0)),
            pl.BlockSpec((d, tn), lambda j: (0, j)),
            pl.BlockSpec((1, tn), lambda j: (0, j)),
        ],
        out_specs=pl.BlockSpec((rows, tn), lambda j: (0, j)),
        out_shape=jax.ShapeDtypeStruct((rows, n), F32),
        compiler_params=_cparams(("arbitrary",)),
        name="modulation",
    )(c_all, w_mod, b_mod.reshape(1, n))


def _p_rows(n_p):
    return lambda i: (jnp.minimum(i, n_p - 1), 0)


def _p_batch(n_p, tiles_per_batch):
    return lambda i: (jnp.minimum(i, n_p - 1) // tiles_per_batch, 0, 0)


def _s_rows(n_p):
    return lambda i: (jnp.maximum(i - n_p, 0), 0)


def _dual(n_p, body, p_refs, s_refs):
    i = pl.program_id(0)

    @pl.when(i < n_p)
    def _():
        body(*p_refs)

    @pl.when(i >= n_p)
    def _():
        body(*s_refs)


def _norm_kernel(n_p, n_heads, xp, xs, shp, shs, scp, scs, g_ref, wg_ref, bg_ref, h_ref, gates_ref):
    def body(x_ref, sh_ref, sc_ref):
        h = _rms(x_ref[...]) * g_ref[...] * (1.0 + sc_ref[...]) + sh_ref[...]
        h_ref[...] = h.astype(BF16)
        g = _dot3(h, wg_ref[...]) + bg_ref[...]
        lane = lax.broadcasted_iota(jnp.int32, g.shape, 1)
        gates_ref[...] = jnp.where(lane < n_heads, g, _log_sigmoid(g))

    _dual(n_p, body, (xp, shp, scp), (xs, shs, scs))


def _norm_stage(x_p, x_s, sh_p, sh_s, sc_p, sc_s, g_mix, w_gates, b_gates, n_heads, t_p):
    tm = TOKEN_TILE
    d = x_p.shape[1]
    n_p, n_s = x_p.shape[0] // tm, x_s.shape[0] // tm
    n_tok = x_p.shape[0] + x_s.shape[0]
    row = pl.BlockSpec((tm, d), _p_rows(n_p))
    srow = pl.BlockSpec((tm, d), _s_rows(n_p))
    pbat = pl.BlockSpec((None, 1, d), _p_batch(n_p, t_p // tm))
    const = lambda shape: pl.BlockSpec(shape, lambda i: (0,) * len(shape))
    return pl.pallas_call(
        functools.partial(_norm_kernel, n_p, n_heads),
        grid=(n_p + n_s,),
        in_specs=[row, srow, pbat, srow, pbat, srow, const((1, d)), const((d, LANES)), const((1, LANES))],
        out_specs=[pl.BlockSpec((tm, d), lambda i: (i, 0)), pl.BlockSpec((tm, LANES), lambda i: (i, 0))],
        out_shape=[jax.ShapeDtypeStruct((n_tok, d), BF16), jax.ShapeDtypeStruct((n_tok, LANES), F32)],
        compiler_params=_cparams(("arbitrary",)),
        name="norm_gates",
    )(x_p, x_s, sh_p, sh_s, sc_p, sc_s, g_mix, w_gates, b_gates)


def _inproj_kernel(h_ref, w_ref, o_ref, wbf_ref):
    @pl.when(pl.program_id(1) == 0)
    def _():
        wbf_ref[...] = w_ref[...].astype(BF16)

    o_ref[...] = _dot(h_ref[...], wbf_ref[...])


def _in_projection(h_all, w_in, width, tn):
    n_tok, d = h_all.shape
    tm = _pick_tile(n_tok, (768, 512, 256))
    return pl.pallas_call(
        _inproj_kernel,
        grid=(width // tn, n_tok // tm),
        in_specs=[pl.BlockSpec((tm, d), lambda j, i: (i, 0)), pl.BlockSpec((d, tn), lambda j, i: (0, j))],
        out_specs=pl.BlockSpec((tm, tn), lambda j, i: (i, j)),
        out_shape=jax.ShapeDtypeStruct((n_tok, width), F32),
        scratch_shapes=[pltpu.VMEM((d, tn), BF16)],
        compiler_params=_cparams(("arbitrary", "arbitrary")),
        name="in_projection",
    )(h_all, w_in)


def _attn_prompt_kernel(slopes_ref, q_ref, k_ref, v_ref, o_ref, kbf_ref, vaug_ref, sel_ref, alibi_ref):
    t, hd = q_ref.shape
    blk = MOBA_BLOCK
    nb = t // blk
    slope = slopes_ref[pl.program_id(1)]
    scale = hd ** -0.5

    k = k_ref[...]
    kbf_ref[...] = k.astype(BF16)
    vaug_ref[:, :hd] = v_ref[...].astype(BF16)
    vaug_ref[:, hd:] = jnp.ones((t, hd), BF16)

    kmean = jnp.mean(k.reshape(nb, blk, hd), axis=1)
    gate = _dot3(kmean, q_ref[...], _dot_nt)
    kb = lax.broadcasted_iota(jnp.int32, (nb, t), 0)
    qb = lax.broadcasted_iota(jnp.int32, (nb, t), 1) // blk
    valid = kb < qb
    g = jnp.where(valid, gate, -jnp.inf)
    rank = jnp.zeros((nb, t), jnp.int32)
    for n2 in range(nb):
        gn = g[n2:n2 + 1, :]
        rank += ((gn > g) | ((gn == g) & (n2 < kb))).astype(jnp.int32)
    allowed = (valid & (rank < MOBA_TOPK)).astype(F32)
    sel_ref[...] = jnp.concatenate([allowed, jnp.zeros((LANES - nb, t), F32)], axis=0).T.astype(BF16)
    expand = (lax.broadcasted_iota(jnp.int32, (LANES, t), 0)
              == lax.broadcasted_iota(jnp.int32, (LANES, t), 1) // blk).astype(BF16)
    alibi_ref[...] = slope * (lax.broadcasted_iota(jnp.int32, (blk, t), 0)
                              - lax.broadcasted_iota(jnp.int32, (blk, t), 1)).astype(F32)
    causal = (lax.broadcasted_iota(jnp.int32, (blk, blk), 0) >= lax.broadcasted_iota(jnp.int32, (blk, blk), 1))

    for i in range(nb):
        w = (i + 1) * blk
        rows = pl.ds(i * blk, blk)
        qt = (q_ref[rows, :] * scale).astype(BF16)
        s = _dot_nt(qt, kbf_ref[0:w, :]) - alibi_ref[:, 0:w]
        ok = causal
        if i > 0:
            past_ok = _dot(sel_ref[rows, :], expand[:, 0:i * blk]) > 0.5
            ok = jnp.concatenate([past_ok, causal], axis=1)
        s = jnp.where(ok, s, NEG)
        p = jnp.exp(s - jnp.max(s, axis=1, keepdims=True)).astype(BF16)
        out = _dot(p, vaug_ref[0:w, :])
        o_ref[rows, :] = (out[:, :hd] / out[:, hd:]).astype(o_ref.dtype)


def _attn_prompt(proj, slopes, batch, t, n_heads, hd):
    def col(off):
        return pl.BlockSpec((t, hd), lambda b, h, s: (b, off + h))

    grid_spec = pltpu.PrefetchScalarGridSpec(
        num_scalar_prefetch=1,
        grid=(batch, n_heads),
        in_specs=[col(0), col(n_heads), col(2 * n_heads)],
        out_specs=pl.BlockSpec((t, hd), lambda b, h, s: (b, h)),
        scratch_shapes=[pltpu.VMEM((t, hd), BF16), pltpu.VMEM((t, 2 * hd), BF16), pltpu.VMEM((t, LANES), BF16),
                        pltpu.VMEM((MOBA_BLOCK, t), F32)],
    )
    return pl.pallas_call(
        _attn_prompt_kernel,
        grid_spec=grid_spec,
        out_shape=jax.ShapeDtypeStruct((batch * t, n_heads * hd), BF16),
        compiler_params=_cparams(("arbitrary", "arbitrary")),
        name="attn_prompt",
    )(slopes, proj, proj, proj)


DECODE_BLOCKS_PER_STEP = 2
PACKED_ROWS = 16


def _attn_decode_kernel(pt_ref, q_ref, *refs, n_heads, hd, past_len, bps, slopes):
    del pt_ref
    npg = 2 * bps
    k_refs, v_refs = refs[:npg], refs[npg:2 * npg]
    kn_ref, vn_ref, o_ref, qflat_ref, mb_ref, ksum_ref, m_ref, l_ref, acc_ref = refs[2 * npg:]
    step = pl.program_id(1)
    tq = q_ref.shape[0]
    blk = MOBA_BLOCK
    scale = hd ** -0.5
    n_past = acc_ref.shape[0]
    rowpad = jnp.zeros((PACKED_ROWS - tq, hd), F32)
    rows = n_heads * tq
    bw = blk * n_heads
    row_head = lax.broadcasted_iota(jnp.int32, (rows, 1), 0) // tq
    slope_row = jnp.zeros((rows, 1), F32)
    for h in range(n_heads):
        slope_row = jnp.where(row_head == h, slopes[h], slope_row)

    @pl.when(step == 0)
    def _():
        qflat_ref[...] = jnp.concatenate([q_ref[:, h * hd:(h + 1) * hd] for h in range(n_heads)], axis=0) * scale
        ksum_ref[...] = jnp.zeros_like(ksum_ref)
        r = lax.broadcasted_iota(jnp.int32, (rows, bw), 0)
        c = lax.broadcasted_iota(jnp.int32, (rows, bw), 1)
        in_block_dist = (r % tq - c // n_heads).astype(F32)
        mb_ref[...] = jnp.where(c % n_heads == r // tq, -slope_row * in_block_dist, NEG)

    qb = qflat_ref[...].astype(BF16)
    for j in range(bps):
        n = step * bps + j
        k2 = jnp.concatenate([k_refs[2 * j][...], k_refs[2 * j + 1][...]], axis=0)
        v2 = jnp.concatenate([v_refs[2 * j][...], v_refs[2 * j + 1][...]], axis=0)
        ksum_ref[pl.ds(pl.multiple_of(n * n_heads, n_heads), n_heads), :] = jnp.sum(
            k2.reshape(blk, n_heads, hd), axis=0)
        s = _dot_nt(qb, k2.astype(BF16)) + mb_ref[...]
        m = jnp.max(s, axis=1, keepdims=True)
        p = jnp.exp(s - m)
        l = jnp.sum(p, axis=1, keepdims=True)
        acc_ref[n] = _dot(p.astype(BF16), v2.astype(BF16))
        block_dist = (past_len - n * blk).astype(F32)
        m_ref[n] = jnp.broadcast_to(m - slope_row * block_dist, (rows, hd))
        l_ref[n] = jnp.broadcast_to(l, (rows, hd))

    @pl.when(step == pl.num_programs(1) - 1)
    def _():
        lane = lax.broadcasted_iota(jnp.int32, (tq, LANES), 1)
        valid = lane < n_past
        orow = lax.broadcasted_iota(jnp.int32, (PACKED_ROWS, PACKED_ROWS), 0)
        ocol = lax.broadcasted_iota(jnp.int32, (PACKED_ROWS, PACKED_ROWS), 1)
        d_own = (orow - ocol).astype(F32)
        for h in range(n_heads):
            hs = slice(h * hd, (h + 1) * hd)
            rs = slice(h * tq, (h + 1) * tq)
            qf = jnp.concatenate([qflat_ref[rs, :], rowpad], axis=0)
            kmean = ksum_ref[pl.ds(h, LANES, stride=n_heads), :] * (1.0 / blk)
            gate = _dot3(qf[:tq], kmean, _dot_nt)
            g = jnp.where(valid, gate, -jnp.inf)
            rank = jnp.zeros((tq, LANES), jnp.int32)
            for n2 in range(n_past):
                gn = g[:, n2:n2 + 1]
                rank += ((gn > g) | ((gn == g) & (n2 < lane))).astype(jnp.int32)
            sel = (valid & (rank < MOBA_TOPK)).astype(F32)

            kn = jnp.concatenate([kn_ref[:, hs], rowpad], axis=0).astype(BF16)
            vn = jnp.concatenate([vn_ref[:, hs], rowpad], axis=0).astype(BF16)
            s_own = jnp.where(d_own >= 0.0, _dot_nt(qf.astype(BF16), kn) - slopes[h] * d_own, NEG)
            m_own = jnp.max(s_own, axis=1, keepdims=True)
            p_own = jnp.exp(s_own - m_own)
            l_own = jnp.sum(p_own, axis=1, keepdims=True)[:tq]
            acc_own = _dot(p_own.astype(BF16), vn)[:tq]
            m_own = m_own[:tq]

            picked = [sel[:, n2:n2 + 1] > 0.0 for n2 in range(n_past)]
            m_fin = jnp.broadcast_to(m_own, (tq, hd))
            for n2 in range(n_past):
                m_fin = jnp.maximum(m_fin, jnp.where(picked[n2], m_ref[n2, rs, :], NEG))
            w_own = jnp.exp(m_own - m_fin)
            l_fin = w_own * l_own
            out = w_own * acc_own
            for n2 in range(n_past):
                w = jnp.where(picked[n2], jnp.exp(m_ref[n2, rs, :] - m_fin), 0.0)
                l_fin += w * l_ref[n2, rs, :]
                out += w * acc_ref[n2, rs, :]
            o_ref[:, hs] = out / l_fin


def _attn_decode(proj, cache_k, cache_v, page_table, batch, tq, n_heads, hd, row0):
    _, n_phys, page, _, _ = cache_k.shape
    width = n_heads * hd
    bps = DECODE_BLOCKS_PER_STEP
    assert 2 * page == MOBA_BLOCK, "a MoBA block must span exactly two cache pages"
    assert tq <= PACKED_ROWS and row0 % tq == 0
    n_pages = page_table.shape[1]
    n_past = n_pages // 2
    assert n_past <= LANES and n_past % bps == 0
    slopes = tuple(2.0 ** (-8.0 * (h + 1) / n_heads) for h in range(n_heads))

    ck = cache_k.reshape(n_phys, page * n_heads, hd)
    cv = cache_v.reshape(n_phys, page * n_heads, hd)

    def page_spec(p):
        return pl.BlockSpec((None, page * n_heads, hd),
                            lambda b, s, pt: (pt[b * n_pages + 2 * bps * s + p], 0, 0))

    def new_spec(colblock):
        return pl.BlockSpec((tq, width), lambda b, s, pt: (row0 // tq + b, colblock))

    pages = [page_spec(p) for p in range(2 * bps)]
    rows = n_heads * tq
    per_block = (n_past, rows, hd)
    grid_spec = pltpu.PrefetchScalarGridSpec(
        num_scalar_prefetch=1,
        grid=(batch, n_past // bps),
        in_specs=[new_spec(0)] + pages + pages + [new_spec(1), new_spec(2)],
        out_specs=pl.BlockSpec((tq, width), lambda b, s, pt: (b, 0)),
        scratch_shapes=[pltpu.VMEM((rows, hd), F32), pltpu.VMEM((rows, MOBA_BLOCK * n_heads), F32),
                        pltpu.VMEM((LANES * n_heads, hd), F32), pltpu.VMEM(per_block, F32),
                        pltpu.VMEM(per_block, F32), pltpu.VMEM(per_block, F32)],
    )
    return pl.pallas_call(
        functools.partial(_attn_decode_kernel, n_heads=n_heads, hd=hd, past_len=n_pages * page, bps=bps,
                          slopes=slopes),
        grid_spec=grid_spec,
        out_shape=jax.ShapeDtypeStruct((batch * tq, width), F32),
        compiler_params=_cparams(("arbitrary", "arbitrary")),
        name="attn_decode",
    )(page_table.reshape(-1), proj, *([ck] * (2 * bps)), *([cv] * (2 * bps)), proj, proj)


def _mlstm_kernel(q_ref, k_ref, v_ref, og_ref, gcol_ref, grow_ref, c0_ref, n0_ref, m0_ref, gain_ref,
                  hm_ref, c_ref, n_ref, m_ref, *, chunk):
    t, d = q_ref.shape
    lc = chunk
    c_ref[...] = c0_ref[...]
    n_ref[...] = n0_ref[...]
    m_ref[...] = m0_ref[...]
    r = lax.broadcasted_iota(jnp.int32, (lc, lc), 0)
    c = lax.broadcasted_iota(jnp.int32, (lc, lc), 1)
    lower = r >= c

    def one_chunk(rows):
        q = q_ref[rows, :]
        k = k_ref[rows, :] * (d ** -0.5)
        v = v_ref[rows, :]
        gc = gcol_ref[rows, :]
        ig_c, lf_c = gc[:, 0:1], gc[:, 1:2]
        gr = grow_ref[:, rows]
        ig_r, lf_r = gr[0:1, :], gr[1:2, :]
        cmat, nvec, m0 = c_ref[...], n_ref[...], m_ref[...]

        b_c = jnp.sum(jnp.where(lower, lf_r, 0.0), axis=1, keepdims=True)
        b_r = jnp.sum(jnp.where(r <= c, lf_c, 0.0), axis=0, keepdims=True)
        dmat = jnp.where(lower, b_c - b_r + ig_r, NEG)
        inter = b_c + m0
        mt = jnp.maximum(inter, jnp.max(dmat, axis=1, keepdims=True))
        w = jnp.exp(dmat - mt)
        a = jnp.exp(inter - mt)
        qb, kb, vb = q.astype(BF16), k.astype(BF16), v.astype(BF16)
        sc = _dot_nt(qb, kb) * w
        num = a * _dot_nt(qb, cmat.astype(BF16)) + _dot(sc.astype(BF16), vb)
        den = a * jnp.sum(q * nvec, axis=1, keepdims=True) + jnp.sum(sc, axis=1, keepdims=True)
        h = num / jnp.maximum(jnp.abs(den), jnp.exp(-mt))
        hn = _rms(h) * gain_ref[...]
        hm_ref[rows, :] = (hn * _sigmoid(og_ref[rows, :])).astype(hm_ref.dtype)

        b_last = b_c[lc - 1:lc, :]
        g = b_last - b_c + ig_c
        m_new = jnp.maximum(b_last + m0, jnp.max(g, axis=0, keepdims=True))
        ws = jnp.exp(g - m_new)
        a_last = jnp.exp(b_last + m0 - m_new)
        c_ref[...] = a_last * cmat + _dot_tn((v * ws).astype(BF16), kb)
        n_ref[...] = a_last * nvec + jnp.sum(ws * k, axis=0, keepdims=True)
        m_ref[...] = m_new

    if t == lc:
        one_chunk(pl.ds(0, lc))
    else:
        def body(ci, carry):
            one_chunk(pl.ds(pl.multiple_of(ci * lc, lc), lc))
            return carry

        lax.fori_loop(0, t // lc, body, 0)


def _mlstm(proj, gcol, grow, c0, n0, m0, gain, batch, t, n_heads, d, row0, col0, out_dtype):
    lc = math.gcd(t, MOBA_BLOCK)
    rb = row0 // t
    cb = col0 // d
    assert row0 % t == 0 and col0 % d == 0

    def col(off):
        return pl.BlockSpec((t, d), lambda b, h: (rb + b, cb + off * n_heads + h))

    per_head = lambda shape: pl.BlockSpec((None, None) + shape, lambda b, h: (b, h, 0, 0))
    return pl.pallas_call(
        functools.partial(_mlstm_kernel, chunk=lc),
        grid=(batch, n_heads),
        in_specs=[col(0), col(1), col(2), col(3), per_head((t, 2)), per_head((2, t)),
                  per_head((d, d)), per_head((1, d)), per_head((1, 1)),
                  pl.BlockSpec((None, 1, d), lambda b, h: (h, 0, 0))],
        out_specs=[pl.BlockSpec((t, d), lambda b, h: (b, h)), per_head((d, d)), per_head((1, d)), per_head((1, 1))],
        out_shape=[jax.ShapeDtypeStruct((batch * t, n_heads * d), out_dtype),
                   jax.ShapeDtypeStruct((batch, n_heads, d, d), F32),
                   jax.ShapeDtypeStruct((batch, n_heads, 1, d), F32),
                   jax.ShapeDtypeStruct((batch, n_heads, 1, 1), F32)],
        compiler_params=_cparams(("arbitrary", "arbitrary")),
        name=f"mlstm_t{t}",
    )(proj, proj, proj, proj, gcol, grow, c0, n0, m0, gain)


def _route(logits, n_groups, n_experts):
    epg = n_experts // n_groups
    lane = lax.broadcasted_iota(jnp.int32, logits.shape, 1).astype(F32)
    big = float(LANES)

    def rmax(x):
        return jnp.max(x, axis=1, keepdims=True)

    def first_lane(mask):
        return jnp.min(jnp.where(mask, lane, big), axis=1, keepdims=True)

    is_g = (lane >= n_experts) & (lane < n_experts + n_groups)
    gmax = rmax(jnp.where(is_g, logits, -jnp.inf))
    g_w = 1.0 / jnp.sum(jnp.where(is_g, jnp.exp(logits - gmax), 0.0), axis=1, keepdims=True)
    group = first_lane(is_g & (logits == gmax)) - n_experts
    lo = group * epg
    in_grp = (lane >= lo) & (lane < lo + epg)
    emax = rmax(jnp.where(in_grp, logits, -jnp.inf))
    pe = jnp.where(in_grp, jnp.exp(logits - emax), 0.0)
    prob = pe / jnp.sum(pe, axis=1, keepdims=True)
    p1 = rmax(jnp.where(in_grp, prob, -1.0))
    i1 = first_lane(in_grp & (prob == p1))
    rest = in_grp & (lane != i1)
    p2 = rmax(jnp.where(rest, prob, -1.0))
    i2 = first_lane(rest & (prob == p2))
    den = p1 + p2
    gate = g_w * jnp.where(lane == i1, p1 / den, jnp.where(lane == i2, p2 / den, 0.0))
    return gate, group


def _outproj_kernel(n_p, att_w, n_groups, n_experts,
                    attp, atts, hmp, hms, xp, xs, g1p, g1s, sh2p, sh2s, sc2p, sc2s,
                    wo_ref, gffn_ref, wr_ref, br_ref, x1_ref, h2g_ref, rinfo_ref, counts_ref, carry_ref):
    d = x1_ref.shape[1]

    @pl.when(pl.program_id(0) == 0)
    def _():
        carry_ref[...] = jnp.zeros_like(carry_ref)

    def body(att_ref, hm_ref, x_ref, g1_ref, sh2_ref, sc2_ref):
        mixed = (_dot(att_ref[...].astype(BF16), wo_ref[0:att_w, :])
                 + _dot(hm_ref[...].astype(BF16), wo_ref[att_w:, :]))
        x1 = x_ref[...] + g1_ref[...] * mixed
        x1_ref[...] = x1
        h2 = _rms(x1) * gffn_ref[...] * (1.0 + sc2_ref[...]) + sh2_ref[...]
        gate, group = _route(_dot3(h2, wr_ref[...]) + br_ref[...], n_groups, n_experts)
        h2g_ref[:, :d] = h2
        h2g_ref[:, d:] = gate
        tm = h2.shape[0]
        lane = lax.broadcasted_iota(jnp.int32, (tm, LANES), 1).astype(F32)
        onehot = (lane == group).astype(F32)
        r = lax.broadcasted_iota(jnp.int32, (tm, tm), 0)
        c = lax.broadcasted_iota(jnp.int32, (tm, tm), 1)
        before = _dot((r > c).astype(BF16), onehot.astype(BF16)) + carry_ref[...]
        rank = jnp.sum(onehot * before, axis=1, keepdims=True)
        rinfo_ref[...] = jnp.where(lane == 0.0, group, jnp.where(lane == 1.0, rank, 0.0))
        carry_ref[...] += jnp.sum(onehot, axis=0, keepdims=True)
        counts_ref[...] = carry_ref[...]

    _dual(n_p, body, (attp, hmp, xp, g1p, sh2p, sc2p), (atts, hms, xs, g1s, sh2s, sc2s))


def _outproj_stage(att_p, att_s, hm_p, hm_s, x_p, x_s, g1_p, g1_s, sh2_p, sh2_s, sc2_p, sc2_s,
                   w_out_bf, g_ffn, w_route, b_route, n_groups, n_experts, t_p):
    tm = TOKEN_TILE
    d = x_p.shape[1]
    att_w = att_p.shape[1]
    ml_w = hm_p.shape[1]
    n_p, n_s = x_p.shape[0] // tm, x_s.shape[0] // tm
    n_tok = x_p.shape[0] + x_s.shape[0]
    prow = lambda w: pl.BlockSpec((tm, w), _p_rows(n_p))
    srow = lambda w: pl.BlockSpec((tm, w), _s_rows(n_p))
    pbat = pl.BlockSpec((None, 1, d), _p_batch(n_p, t_p // tm))
    const = lambda shape: pl.BlockSpec(shape, lambda i: (0,) * len(shape))
    out_row = lambda w: pl.BlockSpec((tm, w), lambda i: (i, 0))
    return pl.pallas_call(
        functools.partial(_outproj_kernel, n_p, att_w, n_groups, n_experts),
        grid=(n_p + n_s,),
        in_specs=[prow(att_w), srow(att_w), prow(ml_w), srow(ml_w), prow(d), srow(d),
                  pbat, srow(d), pbat, srow(d), pbat, srow(d),
                  const((att_w + ml_w, d)), const((1, d)), const((d, LANES)), const((1, LANES))],
        out_specs=[out_row(d), out_row(d + LANES), out_row(LANES), const((1, LANES))],
        out_shape=[jax.ShapeDtypeStruct((n_tok, d), F32), jax.ShapeDtypeStruct((n_tok, d + LANES), F32),
                   jax.ShapeDtypeStruct((n_tok, LANES), F32), jax.ShapeDtypeStruct((1, LANES), F32)],
        scratch_shapes=[pltpu.VMEM((1, LANES), F32)],
        compiler_params=_cparams(("arbitrary",)),
        name="out_projection_router",
    )(att_p, att_s, hm_p, hm_s, x_p, x_s, g1_p, g1_s, sh2_p, sh2_s, sc2_p, sc2_s,
      w_out_bf, g_ffn, w_route, b_route)


MOE_TILE = 512
ROW_DMA_CHUNK = 64


def _row_move_kernel(idx_ref, src_hbm, *rest, scatter):
    dst_hbm, sem = rest[-2], rest[-1]
    n = idx_ref.shape[0]
    ch = ROW_DMA_CHUNK

    def row_copy(i, slot):
        if scatter:
            return pltpu.make_async_copy(src_hbm.at[pl.ds(i, 1)], dst_hbm.at[pl.ds(idx_ref[i], 1)], sem.at[slot])
        return pltpu.make_async_copy(src_hbm.at[pl.ds(idx_ref[i], 1)], dst_hbm.at[pl.ds(i, 1)], sem.at[slot])

    def drain(slot):
        for _ in range(ch):
            row_copy(0, slot).wait()

    def chunk(c, carry):
        slot = c % 2

        @pl.when(c >= 2)
        def _():
            drain(slot)

        for u in range(ch):
            row_copy(c * ch + u, slot).start()
        return carry

    n_chunks = n // ch
    lax.fori_loop(0, n_chunks, chunk, 0)
    for c in range(max(n_chunks - 2, 0), n_chunks):
        drain(c % 2)


def _scatter_rows(idx, src, n_dst):
    n, w = src.shape
    assert n % ROW_DMA_CHUNK == 0
    any_spec = pl.BlockSpec(memory_space=pl.ANY)
    grid_spec = pltpu.PrefetchScalarGridSpec(
        num_scalar_prefetch=1, grid=(1,), in_specs=[any_spec, any_spec], out_specs=any_spec,
        scratch_shapes=[pltpu.SemaphoreType.DMA((2,))])
    return pl.pallas_call(
        functools.partial(_row_move_kernel, scatter=True),
        grid_spec=grid_spec,
        out_shape=jax.ShapeDtypeStruct((n_dst, w), src.dtype),
        input_output_aliases={2: 0},
        compiler_params=_cparams(("arbitrary",)),
        name="moe_dispatch",
    )(idx, src, jnp.zeros((n_dst, w), src.dtype))


def _gather_rows(idx, src):
    n = idx.shape[0]
    assert n % ROW_DMA_CHUNK == 0
    any_spec = pl.BlockSpec(memory_space=pl.ANY)
    grid_spec = pltpu.PrefetchScalarGridSpec(
        num_scalar_prefetch=1, grid=(1,), in_specs=[any_spec], out_specs=any_spec,
        scratch_shapes=[pltpu.SemaphoreType.DMA((2,))])
    return pl.pallas_call(
        functools.partial(_row_move_kernel, scatter=False),
        grid_spec=grid_spec,
        out_shape=jax.ShapeDtypeStruct((n, src.shape[1]), src.dtype),
        compiler_params=_cparams(("arbitrary",)),
        name="moe_combine",
    )(idx, src)


def _moe_kernel(tg_ref, nu_ref, hs_ref, wg_ref, wu_ref, wd_ref, o_ref, h2_ref, *, epg):
    t = pl.program_id(0)
    j = pl.program_id(1)
    d = o_ref.shape[1]
    used = t < nu_ref[0]

    @pl.when(j == 0)
    def _():
        o_ref[...] = jnp.zeros_like(o_ref)
        h2_ref[...] = hs_ref[:, :d].astype(BF16)

    @pl.when(used)
    def _():
        h2 = h2_ref[...]
        hg = _dot(h2, wg_ref[...])
        hu = _dot(h2, wu_ref[...])
        gate = hs_ref[:, d:]
        lane = lax.broadcasted_iota(jnp.int32, gate.shape, 1)
        ge = jnp.sum(jnp.where(lane == tg_ref[t] * epg + j, gate, 0.0), axis=1, keepdims=True)
        act = hg * _sigmoid(hg) * hu * ge
        o_ref[...] += _dot(act.astype(BF16), wd_ref[...])


def _moe_stage(h2g, rinfo, counts, w_gate_bf, w_up_bf, w_down_bf, n_groups):
    n_tok, dw = h2g.shape
    d = dw - LANES
    n_experts, _, ff = w_gate_bf.shape
    epg = n_experts // n_groups
    tm = MOE_TILE
    n_tiles = (n_tok + n_groups * (tm - 1)) // tm

    group = rinfo[:, 0].astype(jnp.int32)
    rank = rinfo[:, 1].astype(jnp.int32)
    cnt = counts[0, :n_groups].astype(jnp.int32)
    tiles_g = (cnt + tm - 1) // tm
    tile_end = jnp.cumsum(tiles_g)
    row_start = (tile_end - tiles_g) * tm
    dest = row_start[group] + rank
    tile_group = jnp.minimum(jnp.searchsorted(tile_end, jnp.arange(n_tiles, dtype=jnp.int32), side="right"),
                             n_groups - 1).astype(jnp.int32)
    n_used = tile_end[-1:].astype(jnp.int32)

    hs = _scatter_rows(dest, h2g, n_tiles * tm)

    def w_index(t, j, tg, nu):
        return (jnp.where(t < nu[0], tg[t] * epg + j, n_experts - 1), 0, 0)

    grid_spec = pltpu.PrefetchScalarGridSpec(
        num_scalar_prefetch=2,
        grid=(n_tiles, epg),
        in_specs=[pl.BlockSpec((tm, dw), lambda t, j, tg, nu: (t, 0)),
                  pl.BlockSpec((None, d, ff), w_index), pl.BlockSpec((None, d, ff), w_index),
                  pl.BlockSpec((None, ff, d), w_index)],
        out_specs=pl.BlockSpec((tm, d), lambda t, j, tg, nu: (t, 0)),
        scratch_shapes=[pltpu.VMEM((tm, d), BF16)],
    )
    ys = pl.pallas_call(
        functools.partial(_moe_kernel, epg=epg),
        grid_spec=grid_spec,
        out_shape=jax.ShapeDtypeStruct((n_tiles * tm, d), F32),
        compiler_params=_cparams(("arbitrary", "arbitrary")),
        name="moe_experts",
    )(tile_group, n_used, hs, w_gate_bf, w_up_bf, w_down_bf)
    return _gather_rows(dest, ys)


def _final_kernel(n_p, x1_ref, moe_ref, g2p, g2s, gfin_ref, yp_ref, ys_ref):
    def body(g2_ref, y_ref):
        x2 = x1_ref[...] + g2_ref[...] * moe_ref[...]
        y_ref[...] = _rms(x2) * gfin_ref[...]

    _dual(n_p, body, (g2p, yp_ref), (g2s, ys_ref))


def _final_stage(x1, moe, g2_p, g2_s, g_final, n_tok_p, t_p):
    tm = TOKEN_TILE
    n_tok, d = x1.shape
    n_p = n_tok_p // tm
    n_s = (n_tok - n_tok_p) // tm
    row = pl.BlockSpec((tm, d), lambda i: (i, 0))
    return pl.pallas_call(
        functools.partial(_final_kernel, n_p),
        grid=(n_p + n_s,),
        in_specs=[row, row, pl.BlockSpec((None, 1, d), _p_batch(n_p, t_p // tm)),
                  pl.BlockSpec((tm, d), _s_rows(n_p)), pl.BlockSpec((1, d), lambda i: (0, 0))],
        out_specs=[pl.BlockSpec((tm, d), _p_rows(n_p)), pl.BlockSpec((tm, d), _s_rows(n_p))],
        out_shape=[jax.ShapeDtypeStruct((n_tok_p, d), F32), jax.ShapeDtypeStruct((n_tok - n_tok_p, d), F32)],
        compiler_params=_cparams(("arbitrary",)),
        name="final_norm",
    )(x1, moe, g2_p, g2_s, g_final)


def kernel(x_prompt, x_sample, cache_k, cache_v, page_table, state_C, state_n, state_m, c_prompt, c_sample,
           w_in, b_ig, b_fg, ml_gain, w_out, g_mix, g_ffn, w_mod, b_mod, w_grp, b_grp, w_exp, b_exp,
           w_gate, w_up, w_down, g_final):
    bp, tp, d = x_prompt.shape
    bd, td, _ = x_sample.shape
    depth = w_in.shape[0]
    assert depth == 1, "one trunk layer"
    n_heads, hd = cache_k.shape[-2:]
    att_w = n_heads * hd
    nh_ml, d_ml = state_n.shape[-2:]
    ml_w = nh_ml * d_ml
    assert att_w == ml_w and att_w + ml_w == d
    n_groups, n_experts = w_grp.shape[-1], w_exp.shape[-1]
    assert n_experts + n_groups <= LANES and 2 * nh_ml <= LANES
    main_w = 3 * att_w + 4 * ml_w
    ntp, nts = bp * tp, bd * td

    c_rows = bp + bd
    c_pad = -c_rows % 16
    c_all = jnp.concatenate([c_prompt, c_sample, jnp.zeros((c_pad, d), F32)], axis=0)
    mod = _modulation(c_all, w_mod[0], b_mod[0])
    mod_p = mod[:bp].reshape(bp, N_MOD, 1, d)
    mod_s = jnp.repeat(mod[bp:c_rows].reshape(bd, N_MOD, d), td, axis=0)
    shift1_p, scale1_p, gate1_p, shift2_p, scale2_p, gate2_p = (mod_p[:, i] for i in range(N_MOD))
    shift1_s, scale1_s, gate1_s, shift2_s, scale2_s, gate2_s = (mod_s[:, i] for i in range(N_MOD))

    x_p = x_prompt.reshape(ntp, d)
    x_s = x_sample.reshape(nts, d)

    w_gates = jnp.pad(w_in[0][:, main_w:], ((0, 0), (0, LANES - 2 * nh_ml)))
    b_gates = jnp.pad(jnp.concatenate([b_ig[0], b_fg[0]]), (0, LANES - 2 * nh_ml)).reshape(1, LANES)
    h_all, gates = _norm_stage(x_p, x_s, shift1_p, shift1_s, scale1_p, scale1_s, g_mix[0].reshape(1, d),
                               w_gates, b_gates, nh_ml, tp)

    proj = _in_projection(h_all, w_in[0], main_w, att_w)

    slopes = 2.0 ** (-8.0 * jnp.arange(1, n_heads + 1, dtype=F32) / n_heads)
    att_p = _attn_prompt(proj, slopes, bp, tp, n_heads, hd)
    att_s = _attn_decode(proj, cache_k, cache_v, page_table, bd, td, n_heads, hd, ntp)

    def gate_layouts(g, b, t):
        g = g[:, :2 * nh_ml].reshape(b, t, 2, nh_ml)
        return g.transpose(0, 3, 1, 2), g.transpose(0, 3, 2, 1)

    gcol_p, grow_p = gate_layouts(gates[:ntp], bp, tp)
    gcol_s, grow_s = gate_layouts(gates[ntp:], bd, td)
    gain = ml_gain[0].reshape(nh_ml, 1, d_ml)
    zeros = lambda *s: jnp.zeros(s, F32)
    hm_p, c_p, n_p, m_p = _mlstm(proj, gcol_p, grow_p, zeros(bp, nh_ml, d_ml, d_ml), zeros(bp, nh_ml, 1, d_ml),
                                 zeros(bp, nh_ml, 1, 1), gain, bp, tp, nh_ml, d_ml, 0, 3 * att_w, BF16)
    hm_s, c_s, n_s, m_s = _mlstm(proj, gcol_s, grow_s, state_C[0], state_n[0].reshape(bd, nh_ml, 1, d_ml),
                                 state_m[0].reshape(bd, nh_ml, 1, 1), gain, bd, td, nh_ml, d_ml, ntp, 3 * att_w, F32)

    w_route = jnp.pad(jnp.concatenate([w_exp[0], w_grp[0]], axis=1), ((0, 0), (0, LANES - n_experts - n_groups)))
    b_route = jnp.pad(jnp.concatenate([b_exp[0], b_grp[0]]), (0, LANES - n_experts - n_groups)).reshape(1, LANES)
    x1, h2g, rinfo, counts = _outproj_stage(att_p, att_s, hm_p, hm_s, x_p, x_s, gate1_p, gate1_s, shift2_p,
                                            shift2_s, scale2_p, scale2_s, w_out[0].astype(BF16),
                                            g_ffn[0].reshape(1, d), w_route, b_route, n_groups, n_experts, tp)

    moe = _moe_stage(h2g, rinfo, counts, w_gate[0].astype(BF16), w_up[0].astype(BF16), w_down[0].astype(BF16),
                     n_groups)
    y_p, y_s = _final_stage(x1, moe, gate2_p, gate2_s, g_final.reshape(1, d), ntp, tp)

    kv_shape_p = (1, bp, tp, n_heads, hd)
    kv_shape_s = (1, bd, td, n_heads, hd)
    return (y_p.reshape(bp, tp, d), y_s.reshape(bd, td, d),
            proj[:ntp, att_w:2 * att_w].reshape(kv_shape_p), proj[:ntp, 2 * att_w:3 * att_w].reshape(kv_shape_p),
            proj[ntp:, att_w:2 * att_w].reshape(kv_shape_s), proj[ntp:, 2 * att_w:3 * att_w].reshape(kv_shape_s),
            c_p[None], n_p.reshape(1, bp, nh_ml, d_ml), m_p.reshape(1, bp, nh_ml),
            c_s[None], n_s.reshape(1, bd, nh_ml, d_ml), m_s.reshape(1, bd, nh_ml))
```

```python
import functools
import math

import jax
import jax.numpy as jnp
from jax import lax
from jax.experimental import pallas as pl
from jax.experimental.pallas import tpu as pltpu

F32 = jnp.float32
BF16 = jnp.bfloat16

MOBA_BLOCK = 256
MOBA_TOPK = 3
EXPERT_TOPK = 2
N_MOD = 6
EPS = 1e-6
NEG = -1e30

LANES = 128
V7X_VMEM_LIMIT = 56 * 1024 * 1024
TOKEN_TILE = 256


def _cparams(sem):
    return pltpu.CompilerParams(dimension_semantics=sem, vmem_limit_bytes=V7X_VMEM_LIMIT)


def _pick_tile(n, candidates):
    for c in candidates:
        if n % c == 0:
            return c
    raise ValueError(f"no tile in {candidates} divides {n}")


def _dot(a, b):
    return jnp.dot(a, b, preferred_element_type=F32)


def _dot_nt(a, b):
    return lax.dot_general(a, b, (((1,), (1,)), ((), ())), preferred_element_type=F32)


def _dot_tn(a, b):
    return lax.dot_general(a, b, (((0,), (0,)), ((), ())), preferred_element_type=F32)


def _split(x):
    hi = x.astype(BF16)
    lo = (x - hi.astype(F32)).astype(BF16)
    return hi, lo


def _dot3(a, b, dot=_dot):
    a_hi, a_lo = _split(a)
    b_hi, b_lo = _split(b)
    return dot(a_hi, b_hi) + dot(a_lo, b_hi) + dot(a_hi, b_lo)


def _sigmoid(x):
    return 1.0 / (1.0 + jnp.exp(-x))


def _log_sigmoid(x):
    return jnp.minimum(x, 0.0) - jnp.log(1.0 + jnp.exp(-jnp.abs(x)))


def _rms(x):
    return x * lax.rsqrt(jnp.mean(x * x, axis=-1, keepdims=True) + EPS)


def _mod_kernel(c_ref, w_ref, b_ref, o_ref):
    o_ref[...] = _dot3(c_ref[...], w_ref[...]) + b_ref[...]


def _modulation(c_all, w_mod, b_mod):
    rows, d = c_all.shape
    n = w_mod.shape[1]
    tn = _pick_tile(n, (1024, 512, 256, 128))
    return pl.pallas_call(
        _mod_kernel,
        grid=(n // tn,),
        in_specs=[
            pl.BlockSpec((rows, d), lambda j: (0, 0)),
            pl.BlockSpec((d, tn), lambda j: (0, j)),
            pl.BlockSpec((1, tn), lambda j: (0, j)),
        ],
        out_specs=pl.BlockSpec((rows, tn), lambda j: (0, j)),
        out_shape=jax.ShapeDtypeStruct((rows, n), F32),
        compiler_params=_cparams(("arbitrary",)),
        name="modulation",
    )(c_all, w_mod, b_mod.reshape(1, n))


def _p_rows(n_p):
    return lambda i: (jnp.minimum(i, n_p - 1), 0)


def _p_batch(n_p, tiles_per_batch):
    return lambda i: (jnp.minimum(i, n_p - 1) // tiles_per_batch, 0, 0)


def _s_rows(n_p):
    return lambda i: (jnp.maximum(i - n_p, 0), 0)


def _dual(n_p, body, p_refs, s_refs):
    i = pl.program_id(0)

    @pl.when(i < n_p)
    def _():
        body(*p_refs)

    @pl.when(i >= n_p)
    def _():
        body(*s_refs)


def _norm_kernel(n_p, n_heads, xp, xs, shp, shs, scp, scs, g_ref, wg_ref, bg_ref, h_ref, gates_ref):
    def body(x_ref, sh_ref, sc_ref):
        h = _rms(x_ref[...]) * g_ref[...] * (1.0 + sc_ref[...]) + sh_ref[...]
        h_ref[...] = h.astype(BF16)
        g = _dot3(h, wg_ref[...]) + bg_ref[...]
        lane = lax.broadcasted_iota(jnp.int32, g.shape, 1)
        gates_ref[...] = jnp.where(lane < n_heads, g, _log_sigmoid(g))

    _dual(n_p, body, (xp, shp, scp), (xs, shs, scs))


def _norm_stage(x_p, x_s, sh_p, sh_s, sc_p, sc_s, g_mix, w_gates, b_gates, n_heads, t_p):
    tm = TOKEN_TILE
    d = x_p.shape[1]
    n_p, n_s = x_p.shape[0] // tm, x_s.shape[0] // tm
    n_tok = x_p.shape[0] + x_s.shape[0]
    row = pl.BlockSpec((tm, d), _p_rows(n_p))
    srow = pl.BlockSpec((tm, d), _s_rows(n_p))
    pbat = pl.BlockSpec((None, 1, d), _p_batch(n_p, t_p // tm))
    const = lambda shape: pl.BlockSpec(shape, lambda i: (0,) * len(shape))
    return pl.pallas_call(
        functools.partial(_norm_kernel, n_p, n_heads),
        grid=(n_p + n_s,),
        in_specs=[row, srow, pbat, srow, pbat, srow, const((1, d)), const((d, LANES)), const((1, LANES))],
        out_specs=[pl.BlockSpec((tm, d), lambda i: (i, 0)), pl.BlockSpec((tm, LANES), lambda i: (i, 0))],
        out_shape=[jax.ShapeDtypeStruct((n_tok, d), BF16), jax.ShapeDtypeStruct((n_tok, LANES), F32)],
        compiler_params=_cparams(("arbitrary",)),
        name="norm_gates",
    )(x_p, x_s, sh_p, sh_s, sc_p, sc_s, g_mix, w_gates, b_gates)


K_TILE, V_TILE = 1, 2


def _inproj_kernel(h_ref, w_ref, rest_ref, k_ref, v_ref, wbf_ref):
    j = pl.program_id(0)

    @pl.when(pl.program_id(1) == 0)
    def _():
        wbf_ref[...] = w_ref[...].astype(BF16)

    acc = _dot(h_ref[...], wbf_ref[...])

    @pl.when(j == K_TILE)
    def _():
        k_ref[...] = acc

    @pl.when(j == V_TILE)
    def _():
        v_ref[...] = acc

    @pl.when((j != K_TILE) & (j != V_TILE))
    def _():
        rest_ref[...] = acc


def _in_projection(h_all, w_in, row0, n_rows, n_col_tiles, tn, name):
    d = h_all.shape[1]
    tm = _pick_tile(n_rows, (512, 256))
    n_i = n_rows // tm
    rb = row0 // tm
    assert row0 % tm == 0 and V_TILE == K_TILE + 1 and K_TILE == 1
    last = n_i - 1

    def rest_index(j, i):
        skipping = (j == K_TILE) | (j == V_TILE)
        return (jnp.where(skipping, last, i), jnp.where(j <= V_TILE, 0, j - V_TILE))

    def own_index(tile):
        return lambda j, i: (jnp.where(j < tile, 0, jnp.where(j == tile, i, last)), 0)

    return pl.pallas_call(
        _inproj_kernel,
        grid=(n_col_tiles, n_i),
        in_specs=[pl.BlockSpec((tm, d), lambda j, i: (rb + i, 0)), pl.BlockSpec((d, tn), lambda j, i: (0, j))],
        out_specs=[pl.BlockSpec((tm, tn), rest_index), pl.BlockSpec((tm, tn), own_index(K_TILE)),
                   pl.BlockSpec((tm, tn), own_index(V_TILE))],
        out_shape=[jax.ShapeDtypeStruct((n_rows, (n_col_tiles - 2) * tn), F32),
                   jax.ShapeDtypeStruct((n_rows, tn), F32), jax.ShapeDtypeStruct((n_rows, tn), F32)],
        scratch_shapes=[pltpu.VMEM((d, tn), BF16)],
        compiler_params=_cparams(("arbitrary", "arbitrary")),
        name=name,
    )(h_all, w_in)


def _attn_prompt_kernel(slopes_ref, q_ref, k_ref, v_ref, o_ref, kbf_ref, vaug_ref, sel_ref, alibi_ref):
    t, hd = q_ref.shape
    blk = MOBA_BLOCK
    nb = t // blk
    slope = slopes_ref[pl.program_id(1)]
    scale = hd ** -0.5

    k = k_ref[...]
    kbf_ref[...] = k.astype(BF16)
    vaug_ref[:, :hd] = v_ref[...].astype(BF16)
    vaug_ref[:, hd:] = jnp.ones((t, hd), BF16)

    kmean = jnp.mean(k.reshape(nb, blk, hd), axis=1)
    gate = _dot3(kmean, q_ref[...], _dot_nt)
    kb = lax.broadcasted_iota(jnp.int32, (nb, t), 0)
    qb = lax.broadcasted_iota(jnp.int32, (nb, t), 1) // blk
    valid = kb < qb
    g = jnp.where(valid, gate, -jnp.inf)
    rank = jnp.zeros((nb, t), jnp.int32)
    for n2 in range(nb):
        gn = g[n2:n2 + 1, :]
        rank += ((gn > g) | ((gn == g) & (n2 < kb))).astype(jnp.int32)
    allowed = (valid & (rank < MOBA_TOPK)).astype(F32)
    sel_ref[...] = jnp.concatenate([allowed, jnp.zeros((LANES - nb, t), F32)], axis=0).T.astype(BF16)
    expand = (lax.broadcasted_iota(jnp.int32, (LANES, t), 0)
              == lax.broadcasted_iota(jnp.int32, (LANES, t), 1) // blk).astype(BF16)
    alibi_ref[...] = slope * (lax.broadcasted_iota(jnp.int32, (blk, t), 0)
                              - lax.broadcasted_iota(jnp.int32, (blk, t), 1)).astype(F32)
    causal = (lax.broadcasted_iota(jnp.int32, (blk, blk), 0) >= lax.broadcasted_iota(jnp.int32, (blk, blk), 1))

    for i in range(nb):
        w = (i + 1) * blk
        rows = pl.ds(i * blk, blk)
        qt = (q_ref[rows, :] * scale).astype(BF16)
        s = _dot_nt(qt, kbf_ref[0:w, :]) - alibi_ref[:, 0:w]
        ok = causal
        if i > 0:
            past_ok = _dot(sel_ref[rows, :], expand[:, 0:i * blk]) > 0.5
            ok = jnp.concatenate([past_ok, causal], axis=1)
        s = jnp.where(ok, s, NEG)
        p = jnp.exp(s - jnp.max(s, axis=1, keepdims=True)).astype(BF16)
        out = _dot(p, vaug_ref[0:w, :])
        o_ref[rows, :] = (out[:, :hd] / out[:, hd:]).astype(o_ref.dtype)


def _attn_prompt(q, k, v, slopes, batch, t, n_heads, hd):
    head = pl.BlockSpec((t, hd), lambda b, h, s: (b, h))
    grid_spec = pltpu.PrefetchScalarGridSpec(
        num_scalar_prefetch=1,
        grid=(batch, n_heads),
        in_specs=[head, head, head],
        out_specs=head,
        scratch_shapes=[pltpu.VMEM((t, hd), BF16), pltpu.VMEM((t, 2 * hd), BF16), pltpu.VMEM((t, LANES), BF16),
                        pltpu.VMEM((MOBA_BLOCK, t), F32)],
    )
    return pl.pallas_call(
        _attn_prompt_kernel,
        grid_spec=grid_spec,
        out_shape=jax.ShapeDtypeStruct((batch * t, n_heads * hd), BF16),
        compiler_params=_cparams(("arbitrary", "arbitrary")),
        name="attn_prompt",
    )(slopes, q, k, v)


DECODE_BLOCKS_PER_STEP = 2
PACKED_ROWS = 16


def _attn_decode_kernel(pt_ref, q_ref, *refs, n_heads, hd, past_len, bps, slopes):
    del pt_ref
    npg = 2 * bps
    k_refs, v_refs = refs[:npg], refs[npg:2 * npg]
    kn_ref, vn_ref, o_ref, qflat_ref, mb_ref, ksum_ref, m_ref, l_ref, acc_ref = refs[2 * npg:]
    step = pl.program_id(1)
    tq = q_ref.shape[0]
    blk = MOBA_BLOCK
    scale = hd ** -0.5
    n_past = acc_ref.shape[0]
    rowpad = jnp.zeros((PACKED_ROWS - tq, hd), F32)
    rows = n_heads * tq
    bw = blk * n_heads
    row_head = lax.broadcasted_iota(jnp.int32, (rows, 1), 0) // tq
    slope_row = jnp.zeros((rows, 1), F32)
    for h in range(n_heads):
        slope_row = jnp.where(row_head == h, slopes[h], slope_row)

    @pl.when(step == 0)
    def _():
        qflat_ref[...] = jnp.concatenate([q_ref[:, h * hd:(h + 1) * hd] for h in range(n_heads)], axis=0) * scale
        ksum_ref[...] = jnp.zeros_like(ksum_ref)
        r = lax.broadcasted_iota(jnp.int32, (rows, bw), 0)
        c = lax.broadcasted_iota(jnp.int32, (rows, bw), 1)
        in_block_dist = (r % tq - c // n_heads).astype(F32)
        mb_ref[...] = jnp.where(c % n_heads == r // tq, -slope_row * in_block_dist, NEG)

    qb = qflat_ref[...].astype(BF16)
    for j in range(bps):
        n = step * bps + j
        k2 = jnp.concatenate([k_refs[2 * j][...], k_refs[2 * j + 1][...]], axis=0)
        v2 = jnp.concatenate([v_refs[2 * j][...], v_refs[2 * j + 1][...]], axis=0)
        ksum_ref[pl.ds(pl.multiple_of(n * n_heads, n_heads), n_heads), :] = jnp.sum(
            k2.reshape(blk, n_heads, hd), axis=0)
        s = _dot_nt(qb, k2.astype(BF16)) + mb_ref[...]
        m = jnp.max(s, axis=1, keepdims=True)
        p = jnp.exp(s - m)
        l = jnp.sum(p, axis=1, keepdims=True)
        acc_ref[n] = _dot(p.astype(BF16), v2.astype(BF16))
        block_dist = (past_len - n * blk).astype(F32)
        m_ref[n] = jnp.broadcast_to(m - slope_row * block_dist, (rows, hd))
        l_ref[n] = jnp.broadcast_to(l, (rows, hd))

    @pl.when(step == pl.num_programs(1) - 1)
    def _():
        lane = lax.broadcasted_iota(jnp.int32, (tq, LANES), 1)
        valid = lane < n_past
        orow = lax.broadcasted_iota(jnp.int32, (PACKED_ROWS, PACKED_ROWS), 0)
        ocol = lax.broadcasted_iota(jnp.int32, (PACKED_ROWS, PACKED_ROWS), 1)
        d_own = (orow - ocol).astype(F32)
        for h in range(n_heads):
            hs = slice(h * hd, (h + 1) * hd)
            rs = slice(h * tq, (h + 1) * tq)
            qf = jnp.concatenate([qflat_ref[rs, :], rowpad], axis=0)
            kmean = ksum_ref[pl.ds(h, LANES, stride=n_heads), :] * (1.0 / blk)
            gate = _dot3(qf[:tq], kmean, _dot_nt)
            g = jnp.where(valid, gate, -jnp.inf)
            rank = jnp.zeros((tq, LANES), jnp.int32)
            for n2 in range(n_past):
                gn = g[:, n2:n2 + 1]
                rank += ((gn > g) | ((gn == g) & (n2 < lane))).astype(jnp.int32)
            sel = (valid & (rank < MOBA_TOPK)).astype(F32)

            kn = jnp.concatenate([kn_ref[:, hs], rowpad], axis=0).astype(BF16)
            vn = jnp.concatenate([vn_ref[:, hs], rowpad], axis=0).astype(BF16)
            s_own = jnp.where(d_own >= 0.0, _dot_nt(qf.astype(BF16), kn) - slopes[h] * d_own, NEG)
            m_own = jnp.max(s_own, axis=1, keepdims=True)
            p_own = jnp.exp(s_own - m_own)
            l_own = jnp.sum(p_own, axis=1, keepdims=True)[:tq]
            acc_own = _dot(p_own.astype(BF16), vn)[:tq]
            m_own = m_own[:tq]

            picked = [sel[:, n2:n2 + 1] > 0.0 for n2 in range(n_past)]
            m_fin = jnp.broadcast_to(m_own, (tq, hd))
            for n2 in range(n_past):
                m_fin = jnp.maximum(m_fin, jnp.where(picked[n2], m_ref[n2, rs, :], NEG))
            w_own = jnp.exp(m_own - m_fin)
            l_fin = w_own * l_own
            out = w_own * acc_own
            for n2 in range(n_past):
                w = jnp.where(picked[n2], jnp.exp(m_ref[n2, rs, :] - m_fin), 0.0)
                l_fin += w * l_ref[n2, rs, :]
                out += w * acc_ref[n2, rs, :]
            o_ref[:, hs] = out / l_fin


def _attn_decode(q_new, k_new, v_new, cache_k, cache_v, page_table, batch, tq, n_heads, hd):
    _, n_phys, page, _, _ = cache_k.shape
    width = n_heads * hd
    bps = DECODE_BLOCKS_PER_STEP
    assert 2 * page == MOBA_BLOCK, "a MoBA block must span exactly two cache pages"
    assert tq <= PACKED_ROWS
    n_pages = page_table.shape[1]
    n_past = n_pages // 2
    assert n_past <= LANES and n_past % bps == 0
    slopes = tuple(2.0 ** (-8.0 * (h + 1) / n_heads) for h in range(n_heads))

    ck = cache_k.reshape(n_phys, page * n_heads, hd)
    cv = cache_v.reshape(n_phys, page * n_heads, hd)

    def page_spec(p):
        return pl.BlockSpec((None, page * n_heads, hd),
                            lambda b, s, pt: (pt[b * n_pages + 2 * bps * s + p], 0, 0))

    new_spec = pl.BlockSpec((tq, width), lambda b, s, pt: (b, 0))

    pages = [page_spec(p) for p in range(2 * bps)]
    rows = n_heads * tq
    per_block = (n_past, rows, hd)
    grid_spec = pltpu.PrefetchScalarGridSpec(
        num_scalar_prefetch=1,
        grid=(batch, n_past // bps),
        in_specs=[new_spec] + pages + pages + [new_spec, new_spec],
        out_specs=pl.BlockSpec((tq, width), lambda b, s, pt: (b, 0)),
        scratch_shapes=[pltpu.VMEM((rows, hd), F32), pltpu.VMEM((rows, MOBA_BLOCK * n_heads), F32),
                        pltpu.VMEM((LANES * n_heads, hd), F32), pltpu.VMEM(per_block, F32),
                        pltpu.VMEM(per_block, F32), pltpu.VMEM(per_block, F32)],
    )
    return pl.pallas_call(
        functools.partial(_attn_decode_kernel, n_heads=n_heads, hd=hd, past_len=n_pages * page, bps=bps,
                          slopes=slopes),
        grid_spec=grid_spec,
        out_shape=jax.ShapeDtypeStruct((batch * tq, width), F32),
        compiler_params=_cparams(("arbitrary", "arbitrary")),
        name="attn_decode",
    )(page_table.reshape(-1), q_new, *([ck] * (2 * bps)), *([cv] * (2 * bps)), k_new, v_new)


def _mlstm_kernel(q_ref, k_ref, v_ref, og_ref, gcol_ref, grow_ref, c0_ref, n0_ref, m0_ref, gain_ref,
                  hm_ref, c_ref, n_ref, m_ref, *, chunk, hps):
    t = q_ref.shape[0]
    d = q_ref.shape[1] // hps
    lc = chunk
    c_ref[...] = c0_ref[...]
    n_ref[...] = n0_ref[...]
    m_ref[...] = m0_ref[...]
    r = lax.broadcasted_iota(jnp.int32, (lc, lc), 0)
    c = lax.broadcasted_iota(jnp.int32, (lc, lc), 1)
    lower = r >= c

    def one_chunk(hh, rows):
        cols = slice(hh * d, (hh + 1) * d)
        q = q_ref[rows, cols]
        k = k_ref[rows, cols] * (d ** -0.5)
        v = v_ref[rows, cols]
        gc = gcol_ref[hh, rows, :]
        ig_c, lf_c = gc[:, 0:1], gc[:, 1:2]
        gr = grow_ref[hh, :, rows]
        ig_r, lf_r = gr[0:1, :], gr[1:2, :]
        cmat, nvec, m0 = c_ref[hh], n_ref[hh], m_ref[hh]

        b_c = jnp.sum(jnp.where(lower, lf_r, 0.0), axis=1, keepdims=True)
        b_r = jnp.sum(jnp.where(r <= c, lf_c, 0.0), axis=0, keepdims=True)
        dmat = jnp.where(lower, b_c - b_r + ig_r, NEG)
        inter = b_c + m0
        mt = jnp.maximum(inter, jnp.max(dmat, axis=1, keepdims=True))
        w = jnp.exp(dmat - mt)
        a = jnp.exp(inter - mt)
        qb, kb, vb = q.astype(BF16), k.astype(BF16), v.astype(BF16)
        sc = _dot_nt(qb, kb) * w
        num = a * _dot_nt(qb, cmat.astype(BF16)) + _dot(sc.astype(BF16), vb)
        den = a * jnp.sum(q * nvec, axis=1, keepdims=True) + jnp.sum(sc, axis=1, keepdims=True)
        h = num / jnp.maximum(jnp.abs(den), jnp.exp(-mt))
        hn = _rms(h) * gain_ref[hh]
        hm_ref[rows, cols] = (hn * _sigmoid(og_ref[rows, cols])).astype(hm_ref.dtype)

        b_last = b_c[lc - 1:lc, :]
        g = b_last - b_c + ig_c
        m_new = jnp.maximum(b_last + m0, jnp.max(g, axis=0, keepdims=True))
        ws = jnp.exp(g - m_new)
        a_last = jnp.exp(b_last + m0 - m_new)
        c_ref[hh] = a_last * cmat + _dot_tn((v * ws).astype(BF16), kb)
        n_ref[hh] = a_last * nvec + jnp.sum(ws * k, axis=0, keepdims=True)
        m_ref[hh] = m_new

    for hh in range(hps):
        if t == lc:
            one_chunk(hh, pl.ds(0, lc))
        else:
            def body(ci, carry, hh=hh):
                one_chunk(hh, pl.ds(pl.multiple_of(ci * lc, lc), lc))
                return carry

            lax.fori_loop(0, t // lc, body, 0)


def _mlstm(proj, gcol, grow, c0, n0, m0, gain, batch, t, n_heads, d, col0, hps, out_dtype):
    lc = math.gcd(t, MOBA_BLOCK)
    wb = hps * d
    assert col0 % wb == 0 and n_heads % hps == 0

    def col(off):
        return pl.BlockSpec((t, wb), lambda b, h: (b, (col0 + off * n_heads * d) // wb + h))

    per_head = lambda shape: pl.BlockSpec((None, hps) + shape, lambda b, h: (b, h, 0, 0))
    return pl.pallas_call(
        functools.partial(_mlstm_kernel, chunk=lc, hps=hps),
        grid=(batch, n_heads // hps),
        in_specs=[col(0), col(1), col(2), col(3), per_head((t, 2)), per_head((2, t)),
                  per_head((d, d)), per_head((1, d)), per_head((1, 1)),
                  pl.BlockSpec((hps, 1, d), lambda b, h: (h, 0, 0))],
        out_specs=[pl.BlockSpec((t, wb), lambda b, h: (b, h)), per_head((d, d)), per_head((1, d)),
                   per_head((1, 1))],
        out_shape=[jax.ShapeDtypeStruct((batch * t, n_heads * d), out_dtype),
                   jax.ShapeDtypeStruct((batch, n_heads, d, d), F32),
                   jax.ShapeDtypeStruct((batch, n_heads, 1, d), F32),
                   jax.ShapeDtypeStruct((batch, n_heads, 1, 1), F32)],
        compiler_params=_cparams(("arbitrary", "arbitrary")),
        name=f"mlstm_t{t}",
    )(proj, proj, proj, proj, gcol, grow, c0, n0, m0, gain)


def _route(logits, n_groups, n_experts):
    epg = n_experts // n_groups
    lane = lax.broadcasted_iota(jnp.int32, logits.shape, 1).astype(F32)
    big = float(LANES)

    def rmax(x):
        return jnp.max(x, axis=1, keepdims=True)

    def first_lane(mask):
        return jnp.min(jnp.where(mask, lane, big), axis=1, keepdims=True)

    is_g = (lane >= n_experts) & (lane < n_experts + n_groups)
    gmax = rmax(jnp.where(is_g, logits, -jnp.inf))
    g_w = 1.0 / jnp.sum(jnp.where(is_g, jnp.exp(logits - gmax), 0.0), axis=1, keepdims=True)
    group = first_lane(is_g & (logits == gmax)) - n_experts
    lo = group * epg
    in_grp = (lane >= lo) & (lane < lo + epg)
    emax = rmax(jnp.where(in_grp, logits, -jnp.inf))
    pe = jnp.where(in_grp, jnp.exp(logits - emax), 0.0)
    prob = pe / jnp.sum(pe, axis=1, keepdims=True)
    p1 = rmax(jnp.where(in_grp, prob, -1.0))
    i1 = first_lane(in_grp & (prob == p1))
    rest = in_grp & (lane != i1)
    p2 = rmax(jnp.where(rest, prob, -1.0))
    i2 = first_lane(rest & (prob == p2))
    den = p1 + p2
    gate = g_w * jnp.where(lane == i1, p1 / den, jnp.where(lane == i2, p2 / den, 0.0))
    return gate, group


def _outproj_kernel(n_p, att_w, n_groups, n_experts,
                    attp, atts, hmp, hms, xp, xs, g1p, g1s, sh2p, sh2s, sc2p, sc2s,
                    wo_ref, gffn_ref, wr_ref, br_ref, x1_ref, h2g_ref, rinfo_ref, counts_ref, carry_ref):
    d = x1_ref.shape[1]

    @pl.when(pl.program_id(0) == 0)
    def _():
        carry_ref[...] = jnp.zeros_like(carry_ref)

    def body(att_ref, hm_ref, x_ref, g1_ref, sh2_ref, sc2_ref):
        mixed = (_dot(att_ref[...].astype(BF16), wo_ref[0:att_w, :])
                 + _dot(hm_ref[...].astype(BF16), wo_ref[att_w:, :]))
        x1 = x_ref[...] + g1_ref[...] * mixed
        x1_ref[...] = x1
        h2 = _rms(x1) * gffn_ref[...] * (1.0 + sc2_ref[...]) + sh2_ref[...]
        gate, group = _route(_dot3(h2, wr_ref[...]) + br_ref[...], n_groups, n_experts)
        h2g_ref[:, :d] = h2
        h2g_ref[:, d:] = gate
        tm = h2.shape[0]
        lane = lax.broadcasted_iota(jnp.int32, (tm, LANES), 1).astype(F32)
        onehot = (lane == group).astype(F32)
        r = lax.broadcasted_iota(jnp.int32, (tm, tm), 0)
        c = lax.broadcasted_iota(jnp.int32, (tm, tm), 1)
        before = _dot((r > c).astype(BF16), onehot.astype(BF16)) + carry_ref[...]
        rank = jnp.sum(onehot * before, axis=1, keepdims=True)
        rinfo_ref[...] = jnp.where(lane == 0.0, group, jnp.where(lane == 1.0, rank, 0.0))
        carry_ref[...] += jnp.sum(onehot, axis=0, keepdims=True)
        counts_ref[...] = carry_ref[...]

    _dual(n_p, body, (attp, hmp, xp, g1p, sh2p, sc2p), (atts, hms, xs, g1s, sh2s, sc2s))


def _outproj_stage(att_p, att_s, hm_p, hm_s, x_p, x_s, g1_p, g1_s, sh2_p, sh2_s, sc2_p, sc2_s,
                   w_out_bf, g_ffn, w_route, b_route, n_groups, n_experts, t_p):
    tm = TOKEN_TILE
    d = x_p.shape[1]
    att_w = att_p.shape[1]
    ml_w = hm_p.shape[1]
    n_p, n_s = x_p.shape[0] // tm, x_s.shape[0] // tm
    n_tok = x_p.shape[0] + x_s.shape[0]
    prow = lambda w: pl.BlockSpec((tm, w), _p_rows(n_p))
    srow = lambda w: pl.BlockSpec((tm, w), _s_rows(n_p))
    pbat = pl.BlockSpec((None, 1, d), _p_batch(n_p, t_p // tm))
    const = lambda shape: pl.BlockSpec(shape, lambda i: (0,) * len(shape))
    out_row = lambda w: pl.BlockSpec((tm, w), lambda i: (i, 0))
    return pl.pallas_call(
        functools.partial(_outproj_kernel, n_p, att_w, n_groups, n_experts),
        grid=(n_p + n_s,),
        in_specs=[prow(att_w), srow(att_w), prow(ml_w), srow(ml_w), prow(d), srow(d),
                  pbat, srow(d), pbat, srow(d), pbat, srow(d),
                  const((att_w + ml_w, d)), const((1, d)), const((d, LANES)), const((1, LANES))],
        out_specs=[out_row(d), out_row(d + LANES), out_row(LANES), const((1, LANES))],
        out_shape=[jax.ShapeDtypeStruct((n_tok, d), F32), jax.ShapeDtypeStruct((n_tok, d + LANES), F32),
                   jax.ShapeDtypeStruct((n_tok, LANES), F32), jax.ShapeDtypeStruct((1, LANES), F32)],
        scratch_shapes=[pltpu.VMEM((1, LANES), F32)],
        compiler_params=_cparams(("arbitrary",)),
        name="out_projection_router",
    )(att_p, att_s, hm_p, hm_s, x_p, x_s, g1_p, g1_s, sh2_p, sh2_s, sc2_p, sc2_s,
      w_out_bf, g_ffn, w_route, b_route)


MOE_TILE = 512
ROW_DMA_UNROLL = 8


class _RowGather:
    def __init__(self, idx_ref, src_hbm, buf_ref, sem):
        self.idx_ref, self.src_hbm, self.buf_ref, self.sem = idx_ref, src_hbm, buf_ref, sem
        self.tm = buf_ref.shape[1]

    def _copy(self, row_index, r, slot):
        return pltpu.make_async_copy(self.src_hbm.at[pl.ds(row_index, 1)],
                                     self.buf_ref.at[slot, pl.ds(r, 1)], self.sem.at[slot])

    def start(self, tile, slot):
        def body(r, carry):
            self._copy(self.idx_ref[tile * self.tm + r], r, slot).start()
            return carry

        lax.fori_loop(0, self.tm, body, 0, unroll=ROW_DMA_UNROLL)

    def wait(self, slot):
        def body(r, carry):
            self._copy(0, r, slot).wait()
            return carry

        lax.fori_loop(0, self.tm, body, 0, unroll=ROW_DMA_UNROLL)


def _moe_kernel(tg_ref, nu_ref, src_ref, h2g_hbm, wg_ref, wu_ref, wd_ref, o_ref, hbuf_ref, h2_ref, sem, *, epg):
    t = pl.program_id(0)
    j = pl.program_id(1)
    d = o_ref.shape[1]
    n_used = nu_ref[0]
    used = t < n_used
    slot = t % 2
    rows = _RowGather(src_ref, h2g_hbm, hbuf_ref, sem)

    @pl.when((j == 0) & (t == 0))
    def _():
        rows.start(0, 0)

    @pl.when((j == 0) & used)
    def _():
        rows.wait(slot)

        @pl.when(t + 1 < n_used)
        def _():
            rows.start(t + 1, 1 - slot)

        h2_ref[...] = hbuf_ref[slot, :, :d].astype(BF16)

    @pl.when(j == 0)
    def _():
        o_ref[...] = jnp.zeros_like(o_ref)

    @pl.when(used)
    def _():
        h2 = h2_ref[...]
        hg = _dot(h2, wg_ref[...])
        hu = _dot(h2, wu_ref[...])
        gate = hbuf_ref[slot, :, d:]
        lane = lax.broadcasted_iota(jnp.int32, gate.shape, 1)
        ge = jnp.sum(jnp.where(lane == tg_ref[t] * epg + j, gate, 0.0), axis=1, keepdims=True)
        act = hg * _sigmoid(hg) * hu * ge
        o_ref[...] += _dot(act.astype(BF16), wd_ref[...])


def _moe_stage(h2g, rinfo, counts, w_gate_bf, w_up_bf, w_down_bf, n_groups):
    n_tok, dw = h2g.shape
    d = dw - LANES
    n_experts, _, ff = w_gate_bf.shape
    epg = n_experts // n_groups
    tm = MOE_TILE
    n_tiles = (n_tok + n_groups * (tm - 1)) // tm

    group = rinfo[:, 0].astype(jnp.int32)
    rank = rinfo[:, 1].astype(jnp.int32)
    cnt = counts[0, :n_groups].astype(jnp.int32)
    tiles_g = (cnt + tm - 1) // tm
    tile_end = jnp.cumsum(tiles_g)
    row_start = (tile_end - tiles_g) * tm
    dest = row_start[group] + rank
    tile_group = jnp.minimum(jnp.searchsorted(tile_end, jnp.arange(n_tiles, dtype=jnp.int32), side="right"),
                             n_groups - 1).astype(jnp.int32)
    n_used = tile_end[-1:].astype(jnp.int32)
    src = jnp.zeros((n_tiles * tm,), jnp.int32).at[dest].set(jnp.arange(n_tok, dtype=jnp.int32))

    def w_index(t, j, tg, nu, sr):
        return (jnp.where(t < nu[0], tg[t] * epg + j, n_experts - 1), 0, 0)

    grid_spec = pltpu.PrefetchScalarGridSpec(
        num_scalar_prefetch=3,
        grid=(n_tiles, epg),
        in_specs=[pl.BlockSpec(memory_space=pl.ANY),
                  pl.BlockSpec((None, d, ff), w_index), pl.BlockSpec((None, d, ff), w_index),
                  pl.BlockSpec((None, ff, d), w_index)],
        out_specs=pl.BlockSpec((tm, d), lambda t, j, tg, nu, sr: (t, 0)),
        scratch_shapes=[pltpu.VMEM((2, tm, dw), F32), pltpu.VMEM((tm, d), BF16), pltpu.SemaphoreType.DMA((2,))],
    )
    ys = pl.pallas_call(
        functools.partial(_moe_kernel, epg=epg),
        grid_spec=grid_spec,
        out_shape=jax.ShapeDtypeStruct((n_tiles * tm, d), F32),
        compiler_params=_cparams(("arbitrary", "arbitrary")),
        name="moe_experts",
    )(tile_group, n_used, src, h2g, w_gate_bf, w_up_bf, w_down_bf)
    return ys, dest


def _final_kernel(n_p, dest_ref, x1_ref, moe_hbm, g2p, g2s, gfin_ref, yp_ref, ys_ref, mbuf_ref, sem):
    i = pl.program_id(0)
    slot = i % 2
    rows = _RowGather(dest_ref, moe_hbm, mbuf_ref, sem)

    @pl.when(i == 0)
    def _():
        rows.start(0, 0)

    rows.wait(slot)

    @pl.when(i + 1 < pl.num_programs(0))
    def _():
        rows.start(i + 1, 1 - slot)

    def body(g2_ref, y_ref):
        x2 = x1_ref[...] + g2_ref[...] * mbuf_ref[slot]
        y_ref[...] = _rms(x2) * gfin_ref[...]

    _dual(n_p, body, (g2p, yp_ref), (g2s, ys_ref))


def _final_stage(x1, moe_rows, dest, g2_p, g2_s, g_final, n_tok_p, t_p):
    tm = TOKEN_TILE
    n_tok, d = x1.shape
    n_p = n_tok_p // tm
    n_s = (n_tok - n_tok_p) // tm
    grid_spec = pltpu.PrefetchScalarGridSpec(
        num_scalar_prefetch=1,
        grid=(n_p + n_s,),
        in_specs=[pl.BlockSpec((tm, d), lambda i, ds: (i, 0)), pl.BlockSpec(memory_space=pl.ANY),
                  pl.BlockSpec((None, 1, d), lambda i, ds: _p_batch(n_p, t_p // tm)(i)),
                  pl.BlockSpec((tm, d), lambda i, ds: _s_rows(n_p)(i)), pl.BlockSpec((1, d), lambda i, ds: (0, 0))],
        out_specs=[pl.BlockSpec((tm, d), lambda i, ds: _p_rows(n_p)(i)),
                   pl.BlockSpec((tm, d), lambda i, ds: _s_rows(n_p)(i))],
        scratch_shapes=[pltpu.VMEM((2, tm, d), F32), pltpu.SemaphoreType.DMA((2,))],
    )
    return pl.pallas_call(
        functools.partial(_final_kernel, n_p),
        grid_spec=grid_spec,
        out_shape=[jax.ShapeDtypeStruct((n_tok_p, d), F32), jax.ShapeDtypeStruct((n_tok - n_tok_p, d), F32)],
        compiler_params=_cparams(("arbitrary",)),
        name="final_norm",
    )(dest, x1, moe_rows, g2_p, g2_s, g_final)


def kernel(x_prompt, x_sample, cache_k, cache_v, page_table, state_C, state_n, state_m, c_prompt, c_sample,
           w_in, b_ig, b_fg, ml_gain, w_out, g_mix, g_ffn, w_mod, b_mod, w_grp, b_grp, w_exp, b_exp,
           w_gate, w_up, w_down, g_final):
    bp, tp, d = x_prompt.shape
    bd, td, _ = x_sample.shape
    depth = w_in.shape[0]
    assert depth == 1, "one trunk layer"
    n_heads, hd = cache_k.shape[-2:]
    att_w = n_heads * hd
    nh_ml, d_ml = state_n.shape[-2:]
    ml_w = nh_ml * d_ml
    assert att_w == ml_w and att_w + ml_w == d
    n_groups, n_experts = w_grp.shape[-1], w_exp.shape[-1]
    assert n_experts + n_groups <= LANES and 2 * nh_ml <= LANES
    main_w = 3 * att_w + 4 * ml_w
    ntp, nts = bp * tp, bd * td

    c_rows = bp + bd
    c_pad = -c_rows % 16
    c_all = jnp.concatenate([c_prompt, c_sample, jnp.zeros((c_pad, d), F32)], axis=0)
    mod = _modulation(c_all, w_mod[0], b_mod[0])
    mod_p = mod[:bp].reshape(bp, N_MOD, 1, d)
    mod_s = jnp.repeat(mod[bp:c_rows].reshape(bd, N_MOD, d), td, axis=0)
    shift1_p, scale1_p, gate1_p, shift2_p, scale2_p, gate2_p = (mod_p[:, i] for i in range(N_MOD))
    shift1_s, scale1_s, gate1_s, shift2_s, scale2_s, gate2_s = (mod_s[:, i] for i in range(N_MOD))

    x_p = x_prompt.reshape(ntp, d)
    x_s = x_sample.reshape(nts, d)

    w_gates = jnp.pad(w_in[0][:, main_w:], ((0, 0), (0, LANES - 2 * nh_ml)))
    b_gates = jnp.pad(jnp.concatenate([b_ig[0], b_fg[0]]), (0, LANES - 2 * nh_ml)).reshape(1, LANES)
    h_all, gates = _norm_stage(x_p, x_s, shift1_p, shift1_s, scale1_p, scale1_s, g_mix[0].reshape(1, d),
                               w_gates, b_gates, nh_ml, tp)

    n_col_tiles = main_w // att_w
    rest_p, k_p, v_p = _in_projection(h_all, w_in[0], 0, ntp, n_col_tiles, att_w, "in_projection_prompt")
    rest_s, k_s, v_s = _in_projection(h_all, w_in[0], ntp, nts, n_col_tiles, att_w, "in_projection_decode")

    slopes = 2.0 ** (-8.0 * jnp.arange(1, n_heads + 1, dtype=F32) / n_heads)
    att_p = _attn_prompt(rest_p, k_p, v_p, slopes, bp, tp, n_heads, hd)
    att_s = _attn_decode(rest_s, k_s, v_s, cache_k, cache_v, page_table, bd, td, n_heads, hd)

    def gate_layouts(g, b, t):
        g = g[:, :2 * nh_ml].reshape(b, t, 2, nh_ml)
        return g.transpose(0, 3, 1, 2), g.transpose(0, 3, 2, 1)

    gcol_p, grow_p = gate_layouts(gates[:ntp], bp, tp)
    gcol_s, grow_s = gate_layouts(gates[ntp:], bd, td)
    gain = ml_gain[0].reshape(nh_ml, 1, d_ml)
    zeros = lambda *s: jnp.zeros(s, F32)
    hm_p, c_p, n_p, m_p = _mlstm(rest_p, gcol_p, grow_p, zeros(bp, nh_ml, d_ml, d_ml), zeros(bp, nh_ml, 1, d_ml),
                                 zeros(bp, nh_ml, 1, 1), gain, bp, tp, nh_ml, d_ml, att_w, 1, BF16)
    hm_s, c_s, n_s, m_s = _mlstm(rest_s, gcol_s, grow_s, state_C[0], state_n[0].reshape(bd, nh_ml, 1, d_ml),
                                 state_m[0].reshape(bd, nh_ml, 1, 1), gain, bd, td, nh_ml, d_ml, att_w, nh_ml, F32)

    w_route = jnp.pad(jnp.concatenate([w_exp[0], w_grp[0]], axis=1), ((0, 0), (0, LANES - n_experts - n_groups)))
    b_route = jnp.pad(jnp.concatenate([b_exp[0], b_grp[0]]), (0, LANES - n_experts - n_groups)).reshape(1, LANES)
    x1, h2g, rinfo, counts = _outproj_stage(att_p, att_s, hm_p, hm_s, x_p, x_s, gate1_p, gate1_s, shift2_p,
                                            shift2_s, scale2_p, scale2_s, w_out[0].astype(BF16),
                                            g_ffn[0].reshape(1, d), w_route, b_route, n_groups, n_experts, tp)

    moe_rows, dest = _moe_stage(h2g, rinfo, counts, w_gate[0].astype(BF16), w_up[0].astype(BF16),
                                w_down[0].astype(BF16), n_groups)
    y_p, y_s = _final_stage(x1, moe_rows, dest, gate2_p, gate2_s, g_final.reshape(1, d), ntp, tp)

    kv_shape_p = (1, bp, tp, n_heads, hd)
    kv_shape_s = (1, bd, td, n_heads, hd)
    return (y_p.reshape(bp, tp, d), y_s.reshape(bd, td, d),
            k_p.reshape(kv_shape_p), v_p.reshape(kv_shape_p), k_s.reshape(kv_shape_s), v_s.reshape(kv_shape_s),
            c_p[None], n_p.reshape(1, bp, nh_ml, d_ml), m_p.reshape(1, bp, nh_ml),
            c_s[None], n_s.reshape(1, bd, nh_ml, d_ml), m_s.reshape(1, bd, nh_ml))
```

```python
import functools
import math

import jax
import jax.numpy as jnp
from jax import lax
from jax.experimental import pallas as pl
from jax.experimental.pallas import tpu as pltpu

F32 = jnp.float32
BF16 = jnp.bfloat16

MOBA_BLOCK = 256
MOBA_TOPK = 3
EXPERT_TOPK = 2
N_MOD = 6
EPS = 1e-6
NEG = -1e30

LANES = 128
V7X_VMEM_LIMIT = 56 * 1024 * 1024
TOKEN_TILE = 256


def _cparams(sem):
    return pltpu.CompilerParams(dimension_semantics=sem, vmem_limit_bytes=V7X_VMEM_LIMIT)


def _pick_tile(n, candidates):
    for c in candidates:
        if n % c == 0:
            return c
    raise ValueError(f"no tile in {candidates} divides {n}")


def _dot(a, b):
    return jnp.dot(a, b, preferred_element_type=F32)


def _dot_nt(a, b):
    return lax.dot_general(a, b, (((1,), (1,)), ((), ())), preferred_element_type=F32)


def _dot_tn(a, b):
    return lax.dot_general(a, b, (((0,), (0,)), ((), ())), preferred_element_type=F32)


def _split(x):
    hi = x.astype(BF16)
    lo = (x - hi.astype(F32)).astype(BF16)
    return hi, lo


def _dot3(a, b, dot=_dot):
    a_hi, a_lo = _split(a)
    b_hi, b_lo = _split(b)
    return dot(a_hi, b_hi) + dot(a_lo, b_hi) + dot(a_hi, b_lo)


def _sigmoid(x):
    return 1.0 / (1.0 + jnp.exp(-x))


def _log_sigmoid(x):
    return jnp.minimum(x, 0.0) - jnp.log(1.0 + jnp.exp(-jnp.abs(x)))


def _rms(x):
    return x * lax.rsqrt(jnp.mean(x * x, axis=-1, keepdims=True) + EPS)


def _mod_kernel(c_ref, w_ref, b_ref, o_ref):
    o_ref[...] = _dot3(c_ref[...], w_ref[...]) + b_ref[...]


def _modulation(c_all, w_mod, b_mod):
    rows, d = c_all.shape
    n = w_mod.shape[1]
    tn = _pick_tile(n, (1024, 512, 256, 128))
    return pl.pallas_call(
        _mod_kernel,
        grid=(n // tn,),
        in_specs=[
            pl.BlockSpec((rows, d), lambda j: (0, 0)),
            pl.BlockSpec((d, tn), lambda j: (0, j)),
            pl.BlockSpec((1, tn), lambda j: (0, j)),
        ],
        out_specs=pl.BlockSpec((rows, tn), lambda j: (0, j)),
        out_shape=jax.ShapeDtypeStruct((rows, n), F32),
        compiler_params=_cparams(("arbitrary",)),
        name="modulation",
    )(c_all, w_mod, b_mod.reshape(1, n))


def _p_rows(n_p):
    return lambda i: (jnp.minimum(i, n_p - 1), 0)


def _p_batch(n_p, tiles_per_batch):
    return lambda i: (jnp.minimum(i, n_p - 1) // tiles_per_batch, 0, 0)


def _s_rows(n_p):
    return lambda i: (jnp.maximum(i - n_p, 0), 0)


def _dual(n_p, body, p_refs, s_refs):
    i = pl.program_id(0)

    @pl.when(i < n_p)
    def _():
        body(*p_refs)

    @pl.when(i >= n_p)
    def _():
        body(*s_refs)


def _norm_kernel(n_p, n_heads, xp, xs, shp, shs, scp, scs, g_ref, wg_ref, bg_ref, h_ref, gates_ref):
    def body(x_ref, sh_ref, sc_ref):
        h = _rms(x_ref[...]) * g_ref[...] * (1.0 + sc_ref[...]) + sh_ref[...]
        h_ref[...] = h.astype(BF16)
        w_lane = lax.broadcasted_iota(jnp.int32, wg_ref.shape, 1)
        g = _dot3(h, jnp.where(w_lane < 2 * n_heads, wg_ref[...], 0.0)) + bg_ref[...]
        lane = lax.broadcasted_iota(jnp.int32, g.shape, 1)
        gates_ref[...] = jnp.where(lane < n_heads, g, _log_sigmoid(g))

    _dual(n_p, body, (xp, shp, scp), (xs, shs, scs))


def _norm_stage(x_p, x_s, sh_p, sh_s, sc_p, sc_s, g_mix, w_in, gate_col0, b_gates, n_heads, t_p):
    tm = TOKEN_TILE
    d = x_p.shape[1]
    n_p, n_s = x_p.shape[0] // tm, x_s.shape[0] // tm
    n_tok = x_p.shape[0] + x_s.shape[0]
    assert gate_col0 % LANES == 0 and w_in.shape[1] - gate_col0 == 2 * n_heads
    row = pl.BlockSpec((tm, d), _p_rows(n_p))
    srow = pl.BlockSpec((tm, d), _s_rows(n_p))
    pbat = pl.BlockSpec((None, 1, d), _p_batch(n_p, t_p // tm))
    const = lambda shape: pl.BlockSpec(shape, lambda i: (0,) * len(shape))
    w_gate_cols = pl.BlockSpec((d, LANES), lambda i: (0, gate_col0 // LANES))
    return pl.pallas_call(
        functools.partial(_norm_kernel, n_p, n_heads),
        grid=(n_p + n_s,),
        in_specs=[row, srow, pbat, srow, pbat, srow, const((1, d)), w_gate_cols, const((1, LANES))],
        out_specs=[pl.BlockSpec((tm, d), lambda i: (i, 0)), pl.BlockSpec((tm, LANES), lambda i: (i, 0))],
        out_shape=[jax.ShapeDtypeStruct((n_tok, d), BF16), jax.ShapeDtypeStruct((n_tok, LANES), F32)],
        compiler_params=_cparams(("arbitrary",)),
        name="norm_gates",
    )(x_p, x_s, sh_p, sh_s, sc_p, sc_s, g_mix, w_in, b_gates)


INPROJ_COL_TILE = 512


def _inproj_kernel(h_ref, w_ref, rest_ref, k_ref, v_ref, wbf_ref, *, k_tiles, v_tiles):
    j = pl.program_id(0)

    @pl.when(pl.program_id(1) == 0)
    def _():
        wbf_ref[...] = w_ref[...].astype(BF16)

    acc = _dot(h_ref[...], wbf_ref[...])
    is_k = (j >= k_tiles[0]) & (j < k_tiles[1])
    is_v = (j >= v_tiles[0]) & (j < v_tiles[1])

    @pl.when(is_k)
    def _():
        k_ref[...] = acc

    @pl.when(is_v)
    def _():
        v_ref[...] = acc

    @pl.when(jnp.logical_not(is_k | is_v))
    def _():
        rest_ref[...] = acc


def _in_projection(h_all, w_in, row0, n_rows, width, att_w, name):
    d = h_all.shape[1]
    tn = math.gcd(att_w, INPROJ_COL_TILE)
    tm = _pick_tile(n_rows, (1024, 512, 256))
    n_i = n_rows // tm
    rb = row0 // tm
    assert row0 % tm == 0
    last = n_i - 1
    per = att_w // tn
    k_tiles, v_tiles = (per, 2 * per), (2 * per, 3 * per)

    def rest_index(j, i):
        skipping = (j >= k_tiles[0]) & (j < v_tiles[1])
        return (jnp.where(skipping, last, i),
                jnp.where(j < k_tiles[0], j, jnp.where(skipping, k_tiles[0] - 1, j - 2 * per)))

    def own_index(tiles):
        lo, hi = tiles
        return lambda j, i: (jnp.where(j < lo, 0, jnp.where(j < hi, i, last)),
                             jnp.where(j < lo, 0, jnp.where(j < hi, j - lo, hi - lo - 1)))

    return pl.pallas_call(
        functools.partial(_inproj_kernel, k_tiles=k_tiles, v_tiles=v_tiles),
        grid=(width // tn, n_i),
        in_specs=[pl.BlockSpec((tm, d), lambda j, i: (rb + i, 0)), pl.BlockSpec((d, tn), lambda j, i: (0, j))],
        out_specs=[pl.BlockSpec((tm, tn), rest_index), pl.BlockSpec((tm, tn), own_index(k_tiles)),
                   pl.BlockSpec((tm, tn), own_index(v_tiles))],
        out_shape=[jax.ShapeDtypeStruct((n_rows, width - 2 * att_w), F32),
                   jax.ShapeDtypeStruct((n_rows, att_w), F32), jax.ShapeDtypeStruct((n_rows, att_w), F32)],
        scratch_shapes=[pltpu.VMEM((d, tn), BF16)],
        compiler_params=_cparams(("arbitrary", "arbitrary")),
        name=name,
    )(h_all, w_in)


def _attn_prompt_kernel(slopes_ref, q_ref, k_ref, v_ref, o_ref, kbf_ref, vaug_ref, sel_ref, alibi_ref):
    t, hd = q_ref.shape
    blk = MOBA_BLOCK
    nb = t // blk
    slope = slopes_ref[pl.program_id(1)]
    scale = hd ** -0.5

    k = k_ref[...]
    kbf_ref[...] = k.astype(BF16)
    vaug_ref[:, :hd] = v_ref[...].astype(BF16)
    vaug_ref[:, hd:] = jnp.ones((t, hd), BF16)

    kmean = jnp.mean(k.reshape(nb, blk, hd), axis=1)
    gate = _dot3(kmean, q_ref[...], _dot_nt)
    kb = lax.broadcasted_iota(jnp.int32, (nb, t), 0)
    qb = lax.broadcasted_iota(jnp.int32, (nb, t), 1) // blk
    valid = kb < qb
    g = jnp.where(valid, gate, -jnp.inf)
    rank = jnp.zeros((nb, t), jnp.int32)
    for n2 in range(nb):
        gn = g[n2:n2 + 1, :]
        rank += ((gn > g) | ((gn == g) & (n2 < kb))).astype(jnp.int32)
    allowed = (valid & (rank < MOBA_TOPK)).astype(F32)
    sel_ref[...] = jnp.concatenate([allowed, jnp.zeros((LANES - nb, t), F32)], axis=0).T.astype(BF16)
    expand = (lax.broadcasted_iota(jnp.int32, (LANES, t), 0)
              == lax.broadcasted_iota(jnp.int32, (LANES, t), 1) // blk).astype(BF16)
    alibi_ref[...] = slope * (lax.broadcasted_iota(jnp.int32, (blk, t), 0)
                              - lax.broadcasted_iota(jnp.int32, (blk, t), 1)).astype(F32)
    causal = (lax.broadcasted_iota(jnp.int32, (blk, blk), 0) >= lax.broadcasted_iota(jnp.int32, (blk, blk), 1))

    for i in range(nb):
        w = (i + 1) * blk
        rows = pl.ds(i * blk, blk)
        qt = (q_ref[rows, :] * scale).astype(BF16)
        s = _dot_nt(qt, kbf_ref[0:w, :]) - alibi_ref[:, 0:w]
        ok = causal
        if i > 0:
            past_ok = _dot(sel_ref[rows, :], expand[:, 0:i * blk]) > 0.5
            ok = jnp.concatenate([past_ok, causal], axis=1)
        s = jnp.where(ok, s, NEG)
        p = jnp.exp(s - jnp.max(s, axis=1, keepdims=True)).astype(BF16)
        out = _dot(p, vaug_ref[0:w, :])
        o_ref[rows, :] = (out[:, :hd] / out[:, hd:]).astype(o_ref.dtype)


def _attn_prompt(q, k, v, slopes, batch, t, n_heads, hd):
    head = pl.BlockSpec((t, hd), lambda b, h, s: (b, h))
    grid_spec = pltpu.PrefetchScalarGridSpec(
        num_scalar_prefetch=1,
        grid=(batch, n_heads),
        in_specs=[head, head, head],
        out_specs=head,
        scratch_shapes=[pltpu.VMEM((t, hd), BF16), pltpu.VMEM((t, 2 * hd), BF16), pltpu.VMEM((t, LANES), BF16),
                        pltpu.VMEM((MOBA_BLOCK, t), F32)],
    )
    return pl.pallas_call(
        _attn_prompt_kernel,
        grid_spec=grid_spec,
        out_shape=jax.ShapeDtypeStruct((batch * t, n_heads * hd), BF16),
        compiler_params=_cparams(("arbitrary", "arbitrary")),
        name="attn_prompt",
    )(slopes, q, k, v)


DECODE_BLOCKS_PER_STEP = 4
PACKED_ROWS = 16


def _attn_decode_kernel(pt_ref, q_ref, *refs, n_heads, hd, past_len, bps, slopes):
    del pt_ref
    npg = 2 * bps
    k_refs, v_refs = refs[:npg], refs[npg:2 * npg]
    kn_ref, vn_ref, o_ref, qflat_ref, mb_ref, ksum_ref, m_ref, l_ref, acc_ref = refs[2 * npg:]
    step = pl.program_id(1)
    tq = q_ref.shape[0]
    blk = MOBA_BLOCK
    scale = hd ** -0.5
    n_past = acc_ref.shape[0]
    rowpad = jnp.zeros((PACKED_ROWS - tq, hd), F32)
    rows = n_heads * tq
    bw = blk * n_heads
    row_head = lax.broadcasted_iota(jnp.int32, (rows, 1), 0) // tq
    slope_row = jnp.zeros((rows, 1), F32)
    for h in range(n_heads):
        slope_row = jnp.where(row_head == h, slopes[h], slope_row)

    @pl.when(step == 0)
    def _():
        qflat_ref[...] = jnp.concatenate([q_ref[:, h * hd:(h + 1) * hd] for h in range(n_heads)], axis=0) * scale
        ksum_ref[...] = jnp.zeros_like(ksum_ref)
        r = lax.broadcasted_iota(jnp.int32, (rows, bw), 0)
        c = lax.broadcasted_iota(jnp.int32, (rows, bw), 1)
        in_block_dist = (r % tq - c // n_heads).astype(F32)
        mb_ref[...] = jnp.where(c % n_heads == r // tq, -slope_row * in_block_dist, NEG)

    qb = qflat_ref[...].astype(BF16)
    for j in range(bps):
        n = step * bps + j
        k2 = jnp.concatenate([k_refs[2 * j][...], k_refs[2 * j + 1][...]], axis=0)
        v2 = jnp.concatenate([v_refs[2 * j][...], v_refs[2 * j + 1][...]], axis=0)
        ksum_ref[pl.ds(pl.multiple_of(n * n_heads, n_heads), n_heads), :] = jnp.sum(
            k2.reshape(blk, n_heads, hd), axis=0)
        s = _dot_nt(qb, k2.astype(BF16)) + mb_ref[...]
        m = jnp.max(s, axis=1, keepdims=True)
        p = jnp.exp(s - m)
        l = jnp.sum(p, axis=1, keepdims=True)
        acc_ref[n] = _dot(p.astype(BF16), v2.astype(BF16))
        block_dist = (past_len - n * blk).astype(F32)
        m_ref[n] = jnp.broadcast_to(m - slope_row * block_dist, (rows, hd))
        l_ref[n] = jnp.broadcast_to(l, (rows, hd))

    @pl.when(step == pl.num_programs(1) - 1)
    def _():
        lane = lax.broadcasted_iota(jnp.int32, (tq, LANES), 1)
        valid = lane < n_past
        orow = lax.broadcasted_iota(jnp.int32, (PACKED_ROWS, PACKED_ROWS), 0)
        ocol = lax.broadcasted_iota(jnp.int32, (PACKED_ROWS, PACKED_ROWS), 1)
        d_own = (orow - ocol).astype(F32)
        for h in range(n_heads):
            hs = slice(h * hd, (h + 1) * hd)
            rs = slice(h * tq, (h + 1) * tq)
            qf = jnp.concatenate([qflat_ref[rs, :], rowpad], axis=0)
            kmean = ksum_ref[pl.ds(h, LANES, stride=n_heads), :] * (1.0 / blk)
            gate = _dot3(qf[:tq], kmean, _dot_nt)
            g = jnp.where(valid, gate, -jnp.inf)
            rank = jnp.zeros((tq, LANES), jnp.int32)
            for n2 in range(n_past):
                gn = g[:, n2:n2 + 1]
                rank += ((gn > g) | ((gn == g) & (n2 < lane))).astype(jnp.int32)
            sel = (valid & (rank < MOBA_TOPK)).astype(F32)

            kn = jnp.concatenate([kn_ref[:, hs], rowpad], axis=0).astype(BF16)
            vn = jnp.concatenate([vn_ref[:, hs], rowpad], axis=0).astype(BF16)
            s_own = jnp.where(d_own >= 0.0, _dot_nt(qf.astype(BF16), kn) - slopes[h] * d_own, NEG)
            m_own = jnp.max(s_own, axis=1, keepdims=True)
            p_own = jnp.exp(s_own - m_own)
            l_own = jnp.sum(p_own, axis=1, keepdims=True)[:tq]
            acc_own = _dot(p_own.astype(BF16), vn)[:tq]
            m_own = m_own[:tq]

            picked = [sel[:, n2:n2 + 1] > 0.0 for n2 in range(n_past)]
            m_fin = jnp.broadcast_to(m_own, (tq, hd))
            for n2 in range(n_past):
                m_fin = jnp.maximum(m_fin, jnp.where(picked[n2], m_ref[n2, rs, :], NEG))
            w_own = jnp.exp(m_own - m_fin)
            l_fin = w_own * l_own
            out = w_own * acc_own
            for n2 in range(n_past):
                w = jnp.where(picked[n2], jnp.exp(m_ref[n2, rs, :] - m_fin), 0.0)
                l_fin += w * l_ref[n2, rs, :]
                out += w * acc_ref[n2, rs, :]
            o_ref[:, hs] = out / l_fin


def _attn_decode(q_new, k_new, v_new, cache_k, cache_v, page_table, batch, tq, n_heads, hd):
    _, n_phys, page, _, _ = cache_k.shape
    width = n_heads * hd
    bps = DECODE_BLOCKS_PER_STEP
    assert 2 * page == MOBA_BLOCK, "a MoBA block must span exactly two cache pages"
    assert tq <= PACKED_ROWS
    n_pages = page_table.shape[1]
    n_past = n_pages // 2
    assert n_past <= LANES and n_past % bps == 0
    slopes = tuple(2.0 ** (-8.0 * (h + 1) / n_heads) for h in range(n_heads))

    ck = cache_k.reshape(n_phys, page * n_heads, hd)
    cv = cache_v.reshape(n_phys, page * n_heads, hd)

    def page_spec(p):
        return pl.BlockSpec((None, page * n_heads, hd),
                            lambda b, s, pt: (pt[b * n_pages + 2 * bps * s + p], 0, 0))

    new_spec = pl.BlockSpec((tq, width), lambda b, s, pt: (b, 0))

    pages = [page_spec(p) for p in range(2 * bps)]
    rows = n_heads * tq
    per_block = (n_past, rows, hd)
    grid_spec = pltpu.PrefetchScalarGridSpec(
        num_scalar_prefetch=1,
        grid=(batch, n_past // bps),
        in_specs=[new_spec] + pages + pages + [new_spec, new_spec],
        out_specs=pl.BlockSpec((tq, width), lambda b, s, pt: (b, 0)),
        scratch_shapes=[pltpu.VMEM((rows, hd), F32), pltpu.VMEM((rows, MOBA_BLOCK * n_heads), F32),
                        pltpu.VMEM((LANES * n_heads, hd), F32), pltpu.VMEM(per_block, F32),
                        pltpu.VMEM(per_block, F32), pltpu.VMEM(per_block, F32)],
    )
    return pl.pallas_call(
        functools.partial(_attn_decode_kernel, n_heads=n_heads, hd=hd, past_len=n_pages * page, bps=bps,
                          slopes=slopes),
        grid_spec=grid_spec,
        out_shape=jax.ShapeDtypeStruct((batch * tq, width), F32),
        compiler_params=_cparams(("arbitrary", "arbitrary")),
        name="attn_decode",
    )(page_table.reshape(-1), q_new, *([ck] * (2 * bps)), *([cv] * (2 * bps)), k_new, v_new)


def _mlstm_kernel(q_ref, k_ref, v_ref, og_ref, gcol_ref, grow_ref, c0_ref, n0_ref, m0_ref, gain_ref,
                  hm_ref, c_ref, n_ref, m_ref, *, chunk, hps):
    t = q_ref.shape[0]
    d = q_ref.shape[1] // hps
    lc = chunk
    c_ref[...] = c0_ref[...]
    n_ref[...] = n0_ref[...]
    m_ref[...] = m0_ref[...]
    r = lax.broadcasted_iota(jnp.int32, (lc, lc), 0)
    c = lax.broadcasted_iota(jnp.int32, (lc, lc), 1)
    lower = r >= c

    def one_chunk(hh, rows):
        cols = slice(hh * d, (hh + 1) * d)
        q = q_ref[rows, cols]
        k = k_ref[rows, cols] * (d ** -0.5)
        v = v_ref[rows, cols]
        gc = gcol_ref[hh, rows, :]
        ig_c, lf_c = gc[:, 0:1], gc[:, 1:2]
        gr = grow_ref[hh, :, rows]
        ig_r, lf_r = gr[0:1, :], gr[1:2, :]
        cmat, nvec, m0 = c_ref[hh], n_ref[hh], m_ref[hh]

        b_c = jnp.sum(jnp.where(lower, lf_r, 0.0), axis=1, keepdims=True)
        b_r = jnp.sum(jnp.where(r <= c, lf_c, 0.0), axis=0, keepdims=True)
        dmat = jnp.where(lower, b_c - b_r + ig_r, NEG)
        inter = b_c + m0
        mt = jnp.maximum(inter, jnp.max(dmat, axis=1, keepdims=True))
        w = jnp.exp(dmat - mt)
        a = jnp.exp(inter - mt)
        qb, kb, vb = q.astype(BF16), k.astype(BF16), v.astype(BF16)
        sc = _dot_nt(qb, kb) * w
        num = a * _dot_nt(qb, cmat.astype(BF16)) + _dot(sc.astype(BF16), vb)
        den = a * jnp.sum(q * nvec, axis=1, keepdims=True) + jnp.sum(sc, axis=1, keepdims=True)
        h = num / jnp.maximum(jnp.abs(den), jnp.exp(-mt))
        hn = _rms(h) * gain_ref[hh]
        hm_ref[rows, cols] = (hn * _sigmoid(og_ref[rows, cols])).astype(hm_ref.dtype)

        b_last = b_c[lc - 1:lc, :]
        g = b_last - b_c + ig_c
        m_new = jnp.maximum(b_last + m0, jnp.max(g, axis=0, keepdims=True))
        ws = jnp.exp(g - m_new)
        a_last = jnp.exp(b_last + m0 - m_new)
        c_ref[hh] = a_last * cmat + _dot_tn((v * ws).astype(BF16), kb)
        n_ref[hh] = a_last * nvec + jnp.sum(ws * k, axis=0, keepdims=True)
        m_ref[hh] = m_new

    for hh in range(hps):
        if t == lc:
            one_chunk(hh, pl.ds(0, lc))
        else:
            def body(ci, carry, hh=hh):
                one_chunk(hh, pl.ds(pl.multiple_of(ci * lc, lc), lc))
                return carry

            lax.fori_loop(0, t // lc, body, 0)


def _mlstm(proj, gcol, grow, c0, n0, m0, gain, batch, t, n_heads, d, col0, hps, out_dtype):
    lc = math.gcd(t, MOBA_BLOCK)
    wb = hps * d
    assert col0 % wb == 0 and n_heads % hps == 0

    def col(off):
        return pl.BlockSpec((t, wb), lambda b, h: (b, (col0 + off * n_heads * d) // wb + h))

    per_head = lambda shape: pl.BlockSpec((None, hps) + shape, lambda b, h: (b, h, 0, 0))
    return pl.pallas_call(
        functools.partial(_mlstm_kernel, chunk=lc, hps=hps),
        grid=(batch, n_heads // hps),
        in_specs=[col(0), col(1), col(2), col(3), per_head((t, 2)), per_head((2, t)),
                  per_head((d, d)), per_head((1, d)), per_head((1, 1)),
                  pl.BlockSpec((hps, 1, d), lambda b, h: (h, 0, 0))],
        out_specs=[pl.BlockSpec((t, wb), lambda b, h: (b, h)), per_head((d, d)), per_head((1, d)),
                   per_head((1, 1))],
        out_shape=[jax.ShapeDtypeStruct((batch * t, n_heads * d), out_dtype),
                   jax.ShapeDtypeStruct((batch, n_heads, d, d), F32),
                   jax.ShapeDtypeStruct((batch, n_heads, 1, d), F32),
                   jax.ShapeDtypeStruct((batch, n_heads, 1, 1), F32)],
        compiler_params=_cparams(("arbitrary", "arbitrary")),
        name=f"mlstm_t{t}",
    )(proj, proj, proj, proj, gcol, grow, c0, n0, m0, gain)


def _route(logits, n_groups, n_experts):
    epg = n_experts // n_groups
    lane = lax.broadcasted_iota(jnp.int32, logits.shape, 1).astype(F32)
    big = float(LANES)

    def rmax(x):
        return jnp.max(x, axis=1, keepdims=True)

    def first_lane(mask):
        return jnp.min(jnp.where(mask, lane, big), axis=1, keepdims=True)

    is_g = (lane >= n_experts) & (lane < n_experts + n_groups)
    gmax = rmax(jnp.where(is_g, logits, -jnp.inf))
    g_w = 1.0 / jnp.sum(jnp.where(is_g, jnp.exp(logits - gmax), 0.0), axis=1, keepdims=True)
    group = first_lane(is_g & (logits == gmax)) - n_experts
    lo = group * epg
    in_grp = (lane >= lo) & (lane < lo + epg)
    emax = rmax(jnp.where(in_grp, logits, -jnp.inf))
    pe = jnp.where(in_grp, jnp.exp(logits - emax), 0.0)
    prob = pe / jnp.sum(pe, axis=1, keepdims=True)
    p1 = rmax(jnp.where(in_grp, prob, -1.0))
    i1 = first_lane(in_grp & (prob == p1))
    rest = in_grp & (lane != i1)
    p2 = rmax(jnp.where(rest, prob, -1.0))
    i2 = first_lane(rest & (prob == p2))
    den = p1 + p2
    gate = g_w * jnp.where(lane == i1, p1 / den, jnp.where(lane == i2, p2 / den, 0.0))
    return gate, group


def _outproj_kernel(n_p, att_w, n_groups, n_experts,
                    attp, atts, hmp, hms, xp, xs, g1p, g1s, sh2p, sh2s, sc2p, sc2s,
                    wo_ref, gffn_ref, wr_ref, br_ref, x1_ref, h2g_ref, rinfo_ref, counts_ref, carry_ref):
    d = x1_ref.shape[1]

    @pl.when(pl.program_id(0) == 0)
    def _():
        carry_ref[...] = jnp.zeros_like(carry_ref)

    def body(att_ref, hm_ref, x_ref, g1_ref, sh2_ref, sc2_ref):
        mixed = (_dot(att_ref[...].astype(BF16), wo_ref[0:att_w, :])
                 + _dot(hm_ref[...].astype(BF16), wo_ref[att_w:, :]))
        x1 = x_ref[...] + g1_ref[...] * mixed
        x1_ref[...] = x1
        h2 = _rms(x1) * gffn_ref[...] * (1.0 + sc2_ref[...]) + sh2_ref[...]
        gate, group = _route(_dot3(h2, wr_ref[...]) + br_ref[...], n_groups, n_experts)
        h2g_ref[:, :d] = h2
        h2g_ref[:, d:] = gate
        tm = h2.shape[0]
        lane = lax.broadcasted_iota(jnp.int32, (tm, LANES), 1).astype(F32)
        onehot = (lane == group).astype(F32)
        r = lax.broadcasted_iota(jnp.int32, (tm, tm), 0)
        c = lax.broadcasted_iota(jnp.int32, (tm, tm), 1)
        before = _dot((r > c).astype(BF16), onehot.astype(BF16)) + carry_ref[...]
        rank = jnp.sum(onehot * before, axis=1, keepdims=True)
        rinfo_ref[...] = jnp.where(lane == 0.0, group, jnp.where(lane == 1.0, rank, 0.0))
        carry_ref[...] += jnp.sum(onehot, axis=0, keepdims=True)
        counts_ref[...] = carry_ref[...]

    _dual(n_p, body, (attp, hmp, xp, g1p, sh2p, sc2p), (atts, hms, xs, g1s, sh2s, sc2s))


def _outproj_stage(att_p, att_s, hm_p, hm_s, x_p, x_s, g1_p, g1_s, sh2_p, sh2_s, sc2_p, sc2_s,
                   w_out_bf, g_ffn, w_route, b_route, n_groups, n_experts, t_p):
    tm = TOKEN_TILE
    d = x_p.shape[1]
    att_w = att_p.shape[1]
    ml_w = hm_p.shape[1]
    n_p, n_s = x_p.shape[0] // tm, x_s.shape[0] // tm
    n_tok = x_p.shape[0] + x_s.shape[0]
    prow = lambda w: pl.BlockSpec((tm, w), _p_rows(n_p))
    srow = lambda w: pl.BlockSpec((tm, w), _s_rows(n_p))
    pbat = pl.BlockSpec((None, 1, d), _p_batch(n_p, t_p // tm))
    const = lambda shape: pl.BlockSpec(shape, lambda i: (0,) * len(shape))
    out_row = lambda w: pl.BlockSpec((tm, w), lambda i: (i, 0))
    return pl.pallas_call(
        functools.partial(_outproj_kernel, n_p, att_w, n_groups, n_experts),
        grid=(n_p + n_s,),
        in_specs=[prow(att_w), srow(att_w), prow(ml_w), srow(ml_w), prow(d), srow(d),
                  pbat, srow(d), pbat, srow(d), pbat, srow(d),
                  const((att_w + ml_w, d)), const((1, d)), const((d, LANES)), const((1, LANES))],
        out_specs=[out_row(d), out_row(d + LANES), out_row(LANES), const((1, LANES))],
        out_shape=[jax.ShapeDtypeStruct((n_tok, d), F32), jax.ShapeDtypeStruct((n_tok, d + LANES), F32),
                   jax.ShapeDtypeStruct((n_tok, LANES), F32), jax.ShapeDtypeStruct((1, LANES), F32)],
        scratch_shapes=[pltpu.VMEM((1, LANES), F32)],
        compiler_params=_cparams(("arbitrary",)),
        name="out_projection_router",
    )(att_p, att_s, hm_p, hm_s, x_p, x_s, g1_p, g1_s, sh2_p, sh2_s, sc2_p, sc2_s,
      w_out_bf, g_ffn, w_route, b_route)


MOE_TILE = 512
ROW_DMA_UNROLL = 8


class _RowGather:
    def __init__(self, idx_ref, src_hbm, buf_ref, sem):
        self.idx_ref, self.src_hbm, self.buf_ref, self.sem = idx_ref, src_hbm, buf_ref, sem
        self.tm = buf_ref.shape[1]

    def _copy(self, row_index, r, slot):
        return pltpu.make_async_copy(self.src_hbm.at[pl.ds(row_index, 1)],
                                     self.buf_ref.at[slot, pl.ds(r, 1)], self.sem.at[slot])

    def start(self, tile, slot):
        def body(r, carry):
            self._copy(self.idx_ref[tile * self.tm + r], r, slot).start()
            return carry

        lax.fori_loop(0, self.tm, body, 0, unroll=ROW_DMA_UNROLL)

    def start_inline(self, tile, slot, first_row, count):
        for u in range(count):
            r = first_row + u
            self._copy(self.idx_ref[tile * self.tm + r], r, slot).start()

    def wait(self, slot):
        def body(r, carry):
            self._copy(0, r, slot).wait()
            return carry

        lax.fori_loop(0, self.tm, body, 0, unroll=ROW_DMA_UNROLL)


def _moe_kernel(tg_ref, nu_ref, src_ref, h2g_hbm, wg_ref, wu_ref, wd_ref, o_ref, hbuf_ref, h2_ref, sem, *, epg):
    t = pl.program_id(0)
    j = pl.program_id(1)
    d = o_ref.shape[1]
    n_used = nu_ref[0]
    used = t < n_used
    slot = t % 2
    rows = _RowGather(src_ref, h2g_hbm, hbuf_ref, sem)

    n_tiles = pl.num_programs(0)
    tm = o_ref.shape[0]
    chunk = tm // epg

    @pl.when((j == 0) & (t == 0))
    def _():
        rows.start(0, 0)

    @pl.when((j == 0) & (t <= n_used))
    def _():
        rows.wait(slot)

    @pl.when(j == 0)
    def _():
        o_ref[...] = jnp.zeros_like(o_ref)

    @pl.when((j == 0) & used)
    def _():
        h2_ref[...] = hbuf_ref[slot, :, :d].astype(BF16)

    @pl.when(used)
    def _():
        rows.start_inline(jnp.minimum(t + 1, n_tiles - 1), 1 - slot, j * chunk, chunk)
        h2 = h2_ref[...]
        hg = _dot(h2, wg_ref[...])
        hu = _dot(h2, wu_ref[...])
        gate = hbuf_ref[slot, :, d:]
        lane = lax.broadcasted_iota(jnp.int32, gate.shape, 1)
        ge = jnp.sum(jnp.where(lane == tg_ref[t] * epg + j, gate, 0.0), axis=1, keepdims=True)
        act = hg * _sigmoid(hg) * hu * ge
        o_ref[...] += _dot(act.astype(BF16), wd_ref[...])

    @pl.when(used & (t == n_tiles - 1) & (j == epg - 1))
    def _():
        rows.wait(1 - slot)


def _moe_stage(h2g, rinfo, counts, w_gate_bf, w_up_bf, w_down_bf, n_groups):
    n_tok, dw = h2g.shape
    d = dw - LANES
    n_experts, _, ff = w_gate_bf.shape
    epg = n_experts // n_groups
    tm = MOE_TILE
    n_tiles = (n_tok + n_groups * (tm - 1)) // tm

    group = rinfo[:, 0].astype(jnp.int32)
    rank = rinfo[:, 1].astype(jnp.int32)
    cnt = counts[0, :n_groups].astype(jnp.int32)
    tiles_g = (cnt + tm - 1) // tm
    tile_end = jnp.cumsum(tiles_g)
    row_start = (tile_end - tiles_g) * tm
    dest = row_start[group] + rank
    tile_group = jnp.minimum(jnp.searchsorted(tile_end, jnp.arange(n_tiles, dtype=jnp.int32), side="right"),
                             n_groups - 1).astype(jnp.int32)
    n_used = tile_end[-1:].astype(jnp.int32)
    src = jnp.zeros((n_tiles * tm,), jnp.int32).at[dest].set(jnp.arange(n_tok, dtype=jnp.int32))

    def w_index(t, j, tg, nu, sr):
        return (jnp.where(t < nu[0], tg[t] * epg + j, n_experts - 1), 0, 0)

    grid_spec = pltpu.PrefetchScalarGridSpec(
        num_scalar_prefetch=3,
        grid=(n_tiles, epg),
        in_specs=[pl.BlockSpec(memory_space=pl.ANY),
                  pl.BlockSpec((None, d, ff), w_index), pl.BlockSpec((None, d, ff), w_index),
                  pl.BlockSpec((None, ff, d), w_index)],
        out_specs=pl.BlockSpec((tm, d), lambda t, j, tg, nu, sr: (t, 0)),
        scratch_shapes=[pltpu.VMEM((2, tm, dw), F32), pltpu.VMEM((tm, d), BF16), pltpu.SemaphoreType.DMA((2,))],
    )
    ys = pl.pallas_call(
        functools.partial(_moe_kernel, epg=epg),
        grid_spec=grid_spec,
        out_shape=jax.ShapeDtypeStruct((n_tiles * tm, d), F32),
        compiler_params=_cparams(("arbitrary", "arbitrary")),
        name="moe_experts",
    )(tile_group, n_used, src, h2g, w_gate_bf, w_up_bf, w_down_bf)
    return ys, dest


def _final_kernel(n_p, dest_ref, x1_ref, moe_hbm, g2p, g2s, gfin_ref, yp_ref, ys_ref, mbuf_ref, sem):
    i = pl.program_id(0)
    slot = i % 2
    rows = _RowGather(dest_ref, moe_hbm, mbuf_ref, sem)

    @pl.when(i == 0)
    def _():
        rows.start(0, 0)

    rows.wait(slot)
    n = pl.num_programs(0)

    def body(g2_ref, y_ref):
        rows.start_inline(jnp.minimum(i + 1, n - 1), 1 - slot, 0, rows.tm)
        x2 = x1_ref[...] + g2_ref[...] * mbuf_ref[slot]
        y_ref[...] = _rms(x2) * gfin_ref[...]

    _dual(n_p, body, (g2p, yp_ref), (g2s, ys_ref))

    @pl.when(i == n - 1)
    def _():
        rows.wait(1 - slot)


def _final_stage(x1, moe_rows, dest, g2_p, g2_s, g_final, n_tok_p, t_p):
    tm = TOKEN_TILE
    n_tok, d = x1.shape
    n_p = n_tok_p // tm
    n_s = (n_tok - n_tok_p) // tm
    grid_spec = pltpu.PrefetchScalarGridSpec(
        num_scalar_prefetch=1,
        grid=(n_p + n_s,),
        in_specs=[pl.BlockSpec((tm, d), lambda i, ds: (i, 0)), pl.BlockSpec(memory_space=pl.ANY),
                  pl.BlockSpec((None, 1, d), lambda i, ds: _p_batch(n_p, t_p // tm)(i)),
                  pl.BlockSpec((tm, d), lambda i, ds: _s_rows(n_p)(i)), pl.BlockSpec((1, d), lambda i, ds: (0, 0))],
        out_specs=[pl.BlockSpec((tm, d), lambda i, ds: _p_rows(n_p)(i)),
                   pl.BlockSpec((tm, d), lambda i, ds: _s_rows(n_p)(i))],
        scratch_shapes=[pltpu.VMEM((2, tm, d), F32), pltpu.SemaphoreType.DMA((2,))],
    )
    return pl.pallas_call(
        functools.partial(_final_kernel, n_p),
        grid_spec=grid_spec,
        out_shape=[jax.ShapeDtypeStruct((n_tok_p, d), F32), jax.ShapeDtypeStruct((n_tok - n_tok_p, d), F32)],
        compiler_params=_cparams(("arbitrary",)),
        name="final_norm",
    )(dest, x1, moe_rows, g2_p, g2_s, g_final)


def kernel(x_prompt, x_sample, cache_k, cache_v, page_table, state_C, state_n, state_m, c_prompt, c_sample,
           w_in, b_ig, b_fg, ml_gain, w_out, g_mix, g_ffn, w_mod, b_mod, w_grp, b_grp, w_exp, b_exp,
           w_gate, w_up, w_down, g_final):
    bp, tp, d = x_prompt.shape
    bd, td, _ = x_sample.shape
    depth = w_in.shape[0]
    assert depth == 1, "one trunk layer"
    n_heads, hd = cache_k.shape[-2:]
    att_w = n_heads * hd
    nh_ml, d_ml = state_n.shape[-2:]
    ml_w = nh_ml * d_ml
    assert att_w == ml_w and att_w + ml_w == d
    n_groups, n_experts = w_grp.shape[-1], w_exp.shape[-1]
    assert n_experts + n_groups <= LANES and 2 * nh_ml <= LANES
    main_w = 3 * att_w + 4 * ml_w
    ntp, nts = bp * tp, bd * td

    c_rows = bp + bd
    c_pad = -c_rows % 16
    c_all = jnp.concatenate([c_prompt, c_sample, jnp.zeros((c_pad, d), F32)], axis=0)
    mod = _modulation(c_all, w_mod[0], b_mod[0])
    mod_p = mod[:bp].reshape(bp, N_MOD, 1, d)
    mod_s = jnp.repeat(mod[bp:c_rows].reshape(bd, N_MOD, d), td, axis=0)
    shift1_p, scale1_p, gate1_p, shift2_p, scale2_p, gate2_p = (mod_p[:, i] for i in range(N_MOD))
    shift1_s, scale1_s, gate1_s, shift2_s, scale2_s, gate2_s = (mod_s[:, i] for i in range(N_MOD))

    x_p = x_prompt.reshape(ntp, d)
    x_s = x_sample.reshape(nts, d)

    b_gates = jnp.pad(jnp.concatenate([b_ig[0], b_fg[0]]), (0, LANES - 2 * nh_ml)).reshape(1, LANES)
    h_all, gates = _norm_stage(x_p, x_s, shift1_p, shift1_s, scale1_p, scale1_s, g_mix[0].reshape(1, d),
                               w_in[0], main_w, b_gates, nh_ml, tp)

    rest_p, k_p, v_p = _in_projection(h_all, w_in[0], 0, ntp, main_w, att_w, "in_projection_prompt")
    rest_s, k_s, v_s = _in_projection(h_all, w_in[0], ntp, nts, main_w, att_w, "in_projection_decode")

    slopes = 2.0 ** (-8.0 * jnp.arange(1, n_heads + 1, dtype=F32) / n_heads)
    att_p = _attn_prompt(rest_p, k_p, v_p, slopes, bp, tp, n_heads, hd)
    att_s = _attn_decode(rest_s, k_s, v_s, cache_k, cache_v, page_table, bd, td, n_heads, hd)

    def gate_layouts(g, b, t):
        g = g[:, :2 * nh_ml].reshape(b, t, 2, nh_ml)
        return g.transpose(0, 3, 1, 2), g.transpose(0, 3, 2, 1)

    gcol_p, grow_p = gate_layouts(gates[:ntp], bp, tp)
    gcol_s, grow_s = gate_layouts(gates[ntp:], bd, td)
    gain = ml_gain[0].reshape(nh_ml, 1, d_ml)
    zeros = lambda *s: jnp.zeros(s, F32)
    hm_p, c_p, n_p, m_p = _mlstm(rest_p, gcol_p, grow_p, zeros(bp, nh_ml, d_ml, d_ml), zeros(bp, nh_ml, 1, d_ml),
                                 zeros(bp, nh_ml, 1, 1), gain, bp, tp, nh_ml, d_ml, att_w, 1, BF16)
    hm_s, c_s, n_s, m_s = _mlstm(rest_s, gcol_s, grow_s, state_C[0], state_n[0].reshape(bd, nh_ml, 1, d_ml),
                                 state_m[0].reshape(bd, nh_ml, 1, 1), gain, bd, td, nh_ml, d_ml, att_w, nh_ml, F32)

    w_route = jnp.pad(jnp.concatenate([w_exp[0], w_grp[0]], axis=1), ((0, 0), (0, LANES - n_experts - n_groups)))
    b_route = jnp.pad(jnp.concatenate([b_exp[0], b_grp[0]]), (0, LANES - n_experts - n_groups)).reshape(1, LANES)
    x1, h2g, rinfo, counts = _outproj_stage(att_p, att_s, hm_p, hm_s, x_p, x_s, gate1_p, gate1_s, shift2_p,
                                            shift2_s, scale2_p, scale2_s, w_out[0].astype(BF16),
                                            g_ffn[0].reshape(1, d), w_route, b_route, n_groups, n_experts, tp)

    moe_rows, dest = _moe_stage(h2g, rinfo, counts, w_gate[0].astype(BF16), w_up[0].astype(BF16),
                                w_down[0].astype(BF16), n_groups)
    y_p, y_s = _final_stage(x1, moe_rows, dest, gate2_p, gate2_s, g_final.reshape(1, d), ntp, tp)

    kv_shape_p = (1, bp, tp, n_heads, hd)
    kv_shape_s = (1, bd, td, n_heads, hd)
    return (y_p.reshape(bp, tp, d), y_s.reshape(bd, td, d),
            k_p.reshape(kv_shape_p), v_p.reshape(kv_shape_p), k_s.reshape(kv_shape_s), v_s.reshape(kv_shape_s),
            c_p[None], n_p.reshape(1, bp, nh_ml, d_ml), m_p.reshape(1, bp, nh_ml),
            c_s[None], n_s.reshape(1, bd, nh_ml, d_ml), m_s.reshape(1, bd, nh_ml))
```

```python
import functools
import math

import jax
import jax.numpy as jnp
from jax import lax
from jax.experimental import pallas as pl
from jax.experimental.pallas import tpu as pltpu

F32 = jnp.float32
BF16 = jnp.bfloat16

MOBA_BLOCK = 256
MOBA_TOPK = 3
EXPERT_TOPK = 2
N_MOD = 6
EPS = 1e-6
NEG = -1e30

LANES = 128
V7X_VMEM_LIMIT = 56 * 1024 * 1024
TOKEN_TILE = 256


def _cparams(sem):
    return pltpu.CompilerParams(dimension_semantics=sem, vmem_limit_bytes=V7X_VMEM_LIMIT)


def _pick_tile(n, candidates):
    for c in candidates:
        if n % c == 0:
            return c
    raise ValueError(f"no tile in {candidates} divides {n}")


def _dot(a, b):
    return jnp.dot(a, b, preferred_element_type=F32)


def _dot_nt(a, b):
    return lax.dot_general(a, b, (((1,), (1,)), ((), ())), preferred_element_type=F32)


def _dot_tn(a, b):
    return lax.dot_general(a, b, (((0,), (0,)), ((), ())), preferred_element_type=F32)


def _split(x):
    hi = x.astype(BF16)
    lo = (x - hi.astype(F32)).astype(BF16)
    return hi, lo


def _dot3(a, b, dot=_dot):
    a_hi, a_lo = _split(a)
    b_hi, b_lo = _split(b)
    return dot(a_hi, b_hi) + dot(a_lo, b_hi) + dot(a_hi, b_lo)


def _sigmoid(x):
    return 1.0 / (1.0 + jnp.exp(-x))


def _log_sigmoid(x):
    return jnp.minimum(x, 0.0) - jnp.log(1.0 + jnp.exp(-jnp.abs(x)))


def _rms(x):
    return x * lax.rsqrt(jnp.mean(x * x, axis=-1, keepdims=True) + EPS)


def _mod_kernel(c_ref, w_ref, b_ref, o_ref):
    o_ref[...] = _dot3(c_ref[...], w_ref[...]) + b_ref[...]


def _modulation(c_all, w_mod, b_mod):
    rows, d = c_all.shape
    n = w_mod.shape[1]
    tn = _pick_tile(n, (1024, 512, 256, 128))
    return pl.pallas_call(
        _mod_kernel,
        grid=(n // tn,),
        in_specs=[
            pl.BlockSpec((rows, d), lambda j: (0, 0)),
            pl.BlockSpec((d, tn), lambda j: (0, j)),
            pl.BlockSpec((1, tn), lambda j: (0, j)),
        ],
        out_specs=pl.BlockSpec((rows, tn), lambda j: (0, j)),
        out_shape=jax.ShapeDtypeStruct((rows, n), F32),
        compiler_params=_cparams(("arbitrary",)),
        name="modulation",
    )(c_all, w_mod, b_mod.reshape(1, n))


def _p_rows(n_p):
    return lambda i: (jnp.minimum(i, n_p - 1), 0)


def _p_batch(n_p, tiles_per_batch):
    return lambda i: (jnp.minimum(i, n_p - 1) // tiles_per_batch, 0, 0)


def _s_rows(n_p):
    return lambda i: (jnp.maximum(i - n_p, 0), 0)


def _dual(n_p, body, p_refs, s_refs):
    i = pl.program_id(0)

    @pl.when(i < n_p)
    def _():
        body(*p_refs)

    @pl.when(i >= n_p)
    def _():
        body(*s_refs)


def _norm_kernel(n_p, n_heads, xp, xs, shp, shs, scp, scs, g_ref, wg_ref, bg_ref, h_ref, gates_ref):
    def body(x_ref, sh_ref, sc_ref):
        h = _rms(x_ref[...]) * g_ref[...] * (1.0 + sc_ref[...]) + sh_ref[...]
        h_ref[...] = h.astype(BF16)
        wg = wg_ref[...]
        wg = jnp.concatenate([wg, jnp.zeros((LANES - wg.shape[0], wg.shape[1]), F32)], axis=0)
        g = _dot3(h, wg, _dot_nt) + bg_ref[...]
        lane = lax.broadcasted_iota(jnp.int32, g.shape, 1)
        gates_ref[...] = jnp.where(lane < n_heads, g, _log_sigmoid(g))

    _dual(n_p, body, (xp, shp, scp), (xs, shs, scs))


def _norm_stage(x_p, x_s, sh_p, sh_s, sc_p, sc_s, g_mix, w_in_t, gate_row0, b_gates, n_heads, t_p):
    tm = TOKEN_TILE
    d = x_p.shape[1]
    n_p, n_s = x_p.shape[0] // tm, x_s.shape[0] // tm
    n_tok = x_p.shape[0] + x_s.shape[0]
    n_gate = 2 * n_heads
    assert gate_row0 % n_gate == 0 and w_in_t.shape[0] - gate_row0 == n_gate and n_gate % 8 == 0
    row = pl.BlockSpec((tm, d), _p_rows(n_p))
    srow = pl.BlockSpec((tm, d), _s_rows(n_p))
    pbat = pl.BlockSpec((None, 1, d), _p_batch(n_p, t_p // tm))
    const = lambda shape: pl.BlockSpec(shape, lambda i: (0,) * len(shape))
    w_gate_cols = pl.BlockSpec((n_gate, d), lambda i: (gate_row0 // n_gate, 0))
    return pl.pallas_call(
        functools.partial(_norm_kernel, n_p, n_heads),
        grid=(n_p + n_s,),
        in_specs=[row, srow, pbat, srow, pbat, srow, const((1, d)), w_gate_cols, const((1, LANES))],
        out_specs=[pl.BlockSpec((tm, d), lambda i: (i, 0)), pl.BlockSpec((tm, LANES), lambda i: (i, 0))],
        out_shape=[jax.ShapeDtypeStruct((n_tok, d), BF16), jax.ShapeDtypeStruct((n_tok, LANES), F32)],
        compiler_params=_cparams(("arbitrary",)),
        name="norm_gates",
    )(x_p, x_s, sh_p, sh_s, sc_p, sc_s, g_mix, w_in_t, b_gates)


INPROJ_COL_TILE = 1024


def _inproj_kernel(h_ref, wt_ref, rest_ref, k_ref, v_ref, wbf_ref, *, k_tiles, v_tiles):
    j = pl.program_id(0)

    @pl.when(pl.program_id(1) == 0)
    def _():
        wbf_ref[...] = wt_ref[...].T.astype(BF16)

    acc = _dot(h_ref[...], wbf_ref[...])
    is_k = (j >= k_tiles[0]) & (j < k_tiles[1])
    is_v = (j >= v_tiles[0]) & (j < v_tiles[1])

    @pl.when(is_k)
    def _():
        k_ref[...] = acc

    @pl.when(is_v)
    def _():
        v_ref[...] = acc

    @pl.when(jnp.logical_not(is_k | is_v))
    def _():
        rest_ref[...] = acc.astype(rest_ref.dtype)


def _in_projection(h_all, w_in_t, row0, n_rows, width, att_w, rest_dtype, name):
    d = h_all.shape[1]
    tn = math.gcd(att_w, INPROJ_COL_TILE)
    tm = _pick_tile(n_rows, (1024, 512, 256))
    n_i = n_rows // tm
    rb = row0 // tm
    assert row0 % tm == 0
    last = n_i - 1
    per = att_w // tn
    k_tiles, v_tiles = (per, 2 * per), (2 * per, 3 * per)

    def rest_index(j, i):
        skipping = (j >= k_tiles[0]) & (j < v_tiles[1])
        return (jnp.where(skipping, last, i),
                jnp.where(j < k_tiles[0], j, jnp.where(skipping, k_tiles[0] - 1, j - 2 * per)))

    def own_index(tiles):
        lo, hi = tiles
        return lambda j, i: (jnp.where(j < lo, 0, jnp.where(j < hi, i, last)),
                             jnp.where(j < lo, 0, jnp.where(j < hi, j - lo, hi - lo - 1)))

    return pl.pallas_call(
        functools.partial(_inproj_kernel, k_tiles=k_tiles, v_tiles=v_tiles),
        grid=(width // tn, n_i),
        in_specs=[pl.BlockSpec((tm, d), lambda j, i: (rb + i, 0)), pl.BlockSpec((tn, d), lambda j, i: (j, 0))],
        out_specs=[pl.BlockSpec((tm, tn), rest_index), pl.BlockSpec((tm, tn), own_index(k_tiles)),
                   pl.BlockSpec((tm, tn), own_index(v_tiles))],
        out_shape=[jax.ShapeDtypeStruct((n_rows, width - 2 * att_w), rest_dtype),
                   jax.ShapeDtypeStruct((n_rows, att_w), F32), jax.ShapeDtypeStruct((n_rows, att_w), F32)],
        scratch_shapes=[pltpu.VMEM((d, tn), BF16)],
        compiler_params=_cparams(("arbitrary", "arbitrary")),
        name=name,
    )(h_all, w_in_t)


def _attn_prompt_kernel(slopes_ref, q_ref, k_ref, v_ref, o_ref, kbf_ref, vaug_ref, sel_ref, alibi_ref):
    t, hd = q_ref.shape
    blk = MOBA_BLOCK
    nb = t // blk
    slope = slopes_ref[pl.program_id(1)]
    scale = hd ** -0.5

    k = k_ref[...]
    kbf_ref[...] = (k * scale).astype(BF16)
    vaug_ref[:, :hd] = v_ref[...].astype(BF16)
    vaug_ref[:, hd:] = jnp.ones((t, hd), BF16)

    kmean = jnp.mean(k.reshape(nb, blk, hd), axis=1)
    gate = _dot3(kmean, q_ref[...].astype(F32), _dot_nt)
    kb = lax.broadcasted_iota(jnp.int32, (nb, t), 0)
    qb = lax.broadcasted_iota(jnp.int32, (nb, t), 1) // blk
    valid = kb < qb
    g = jnp.where(valid, gate, -jnp.inf)
    rank = jnp.zeros((nb, t), jnp.int32)
    for n2 in range(nb):
        gn = g[n2:n2 + 1, :]
        rank += ((gn > g) | ((gn == g) & (n2 < kb))).astype(jnp.int32)
    allowed = (valid & (rank < MOBA_TOPK)).astype(F32)
    sel_ref[...] = jnp.concatenate([allowed, jnp.zeros((LANES - nb, t), F32)], axis=0).T.astype(BF16)
    expand = (lax.broadcasted_iota(jnp.int32, (LANES, t), 0)
              == lax.broadcasted_iota(jnp.int32, (LANES, t), 1) // blk).astype(BF16)
    alibi_ref[...] = slope * (lax.broadcasted_iota(jnp.int32, (blk, t), 0)
                              - lax.broadcasted_iota(jnp.int32, (blk, t), 1)).astype(F32)
    causal = (lax.broadcasted_iota(jnp.int32, (blk, blk), 0) >= lax.broadcasted_iota(jnp.int32, (blk, blk), 1))

    for i in range(nb):
        w = (i + 1) * blk
        rows = pl.ds(i * blk, blk)
        s = _dot_nt(q_ref[rows, :].astype(BF16), kbf_ref[0:w, :]) - alibi_ref[:, 0:w]
        ok = causal
        if i > 0:
            past_ok = _dot(sel_ref[rows, :], expand[:, 0:i * blk]) > 0.5
            ok = jnp.concatenate([past_ok, causal], axis=1)
        s = jnp.where(ok, s, NEG)
        p = jnp.exp(s - jnp.max(s, axis=1, keepdims=True)).astype(BF16)
        out = _dot(p, vaug_ref[0:w, :])
        o_ref[rows, :] = (out[:, :hd] / out[:, hd:]).astype(o_ref.dtype)


def _attn_prompt(q, k, v, slopes, batch, t, n_heads, hd):
    head = pl.BlockSpec((t, hd), lambda b, h, s: (b, h))
    grid_spec = pltpu.PrefetchScalarGridSpec(
        num_scalar_prefetch=1,
        grid=(batch, n_heads),
        in_specs=[head, head, head],
        out_specs=head,
        scratch_shapes=[pltpu.VMEM((t, hd), BF16), pltpu.VMEM((t, 2 * hd), BF16), pltpu.VMEM((t, LANES), BF16),
                        pltpu.VMEM((MOBA_BLOCK, t), F32)],
    )
    return pl.pallas_call(
        _attn_prompt_kernel,
        grid_spec=grid_spec,
        out_shape=jax.ShapeDtypeStruct((batch * t, n_heads * hd), BF16),
        compiler_params=_cparams(("arbitrary", "arbitrary")),
        name="attn_prompt",
    )(slopes, q, k, v)


DECODE_BLOCKS_PER_STEP = 8
PACKED_ROWS = 16


def _attn_decode_kernel(pt_ref, q_ref, *refs, n_heads, hd, past_len, bps, slopes):
    del pt_ref
    npg = 2 * bps
    k_refs, v_refs = refs[:npg], refs[npg:2 * npg]
    kn_ref, vn_ref, o_ref, qflat_ref, mb_ref, ksum_ref, m_ref, l_ref, acc_ref = refs[2 * npg:]
    step = pl.program_id(1)
    tq = q_ref.shape[0]
    blk = MOBA_BLOCK
    scale = hd ** -0.5
    n_past = acc_ref.shape[0]
    rowpad = jnp.zeros((PACKED_ROWS - tq, hd), F32)
    rows = n_heads * tq
    bw = blk * n_heads
    row_head = lax.broadcasted_iota(jnp.int32, (rows, 1), 0) // tq
    slope_row = jnp.zeros((rows, 1), F32)
    for h in range(n_heads):
        slope_row = jnp.where(row_head == h, slopes[h], slope_row)

    @pl.when(step == 0)
    def _():
        qflat_ref[...] = jnp.concatenate([q_ref[:, h * hd:(h + 1) * hd] for h in range(n_heads)], axis=0) * scale
        ksum_ref[...] = jnp.zeros_like(ksum_ref)
        r = lax.broadcasted_iota(jnp.int32, (rows, bw), 0)
        c = lax.broadcasted_iota(jnp.int32, (rows, bw), 1)
        in_block_dist = (r % tq - c // n_heads).astype(F32)
        mb_ref[...] = jnp.where(c % n_heads == r // tq, -slope_row * in_block_dist, NEG)

    qb = qflat_ref[...].astype(BF16)
    for j in range(bps):
        n = step * bps + j
        k2 = jnp.concatenate([k_refs[2 * j][...], k_refs[2 * j + 1][...]], axis=0)
        v2 = jnp.concatenate([v_refs[2 * j][...], v_refs[2 * j + 1][...]], axis=0)
        ksum_ref[pl.ds(pl.multiple_of(n * n_heads, n_heads), n_heads), :] = jnp.sum(
            k2.reshape(blk, n_heads, hd), axis=0)
        s = _dot_nt(qb, k2.astype(BF16)) + mb_ref[...]
        m = jnp.max(s, axis=1, keepdims=True)
        p = jnp.exp(s - m)
        l = jnp.sum(p, axis=1, keepdims=True)
        acc_ref[n] = _dot(p.astype(BF16), v2.astype(BF16))
        block_dist = (past_len - n * blk).astype(F32)
        m_ref[n] = jnp.broadcast_to(m - slope_row * block_dist, (rows, hd))
        l_ref[n] = jnp.broadcast_to(l, (rows, hd))

    @pl.when(step == pl.num_programs(1) - 1)
    def _():
        lane = lax.broadcasted_iota(jnp.int32, (tq, LANES), 1)
        valid = lane < n_past
        orow = lax.broadcasted_iota(jnp.int32, (PACKED_ROWS, PACKED_ROWS), 0)
        ocol = lax.broadcasted_iota(jnp.int32, (PACKED_ROWS, PACKED_ROWS), 1)
        d_own = (orow - ocol).astype(F32)
        for h in range(n_heads):
            hs = slice(h * hd, (h + 1) * hd)
            rs = slice(h * tq, (h + 1) * tq)
            qf = jnp.concatenate([qflat_ref[rs, :], rowpad], axis=0)
            kmean = ksum_ref[pl.ds(h, LANES, stride=n_heads), :] * (1.0 / blk)
            gate = _dot3(qf[:tq], kmean, _dot_nt)
            g = jnp.where(valid, gate, -jnp.inf)
            rank = jnp.zeros((tq, LANES), jnp.int32)
            for n2 in range(n_past):
                gn = g[:, n2:n2 + 1]
                rank += ((gn > g) | ((gn == g) & (n2 < lane))).astype(jnp.int32)
            sel = (valid & (rank < MOBA_TOPK)).astype(F32)

            kn = jnp.concatenate([kn_ref[:, hs], rowpad], axis=0).astype(BF16)
            vn = jnp.concatenate([vn_ref[:, hs], rowpad], axis=0).astype(BF16)
            s_own = jnp.where(d_own >= 0.0, _dot_nt(qf.astype(BF16), kn) - slopes[h] * d_own, NEG)
            m_own = jnp.max(s_own, axis=1, keepdims=True)
            p_own = jnp.exp(s_own - m_own)
            l_own = jnp.sum(p_own, axis=1, keepdims=True)[:tq]
            acc_own = _dot(p_own.astype(BF16), vn)[:tq]
            m_own = m_own[:tq]

            picked = [sel[:, n2:n2 + 1] > 0.0 for n2 in range(n_past)]
            m_fin = jnp.broadcast_to(m_own, (tq, hd))
            for n2 in range(n_past):
                m_fin = jnp.maximum(m_fin, jnp.where(picked[n2], m_ref[n2, rs, :], NEG))
            w_own = jnp.exp(m_own - m_fin)
            l_fin = w_own * l_own
            out = w_own * acc_own
            for n2 in range(n_past):
                w = jnp.where(picked[n2], jnp.exp(m_ref[n2, rs, :] - m_fin), 0.0)
                l_fin += w * l_ref[n2, rs, :]
                out += w * acc_ref[n2, rs, :]
            o_ref[:, hs] = out / l_fin


def _attn_decode(q_new, k_new, v_new, cache_k, cache_v, page_table, batch, tq, n_heads, hd):
    _, n_phys, page, _, _ = cache_k.shape
    width = n_heads * hd
    assert 2 * page == MOBA_BLOCK, "a MoBA block must span exactly two cache pages"
    assert tq <= PACKED_ROWS
    n_pages = page_table.shape[1]
    n_past = n_pages // 2
    bps = math.gcd(n_past, DECODE_BLOCKS_PER_STEP)
    assert n_past <= LANES
    slopes = tuple(2.0 ** (-8.0 * (h + 1) / n_heads) for h in range(n_heads))

    ck = cache_k.reshape(n_phys, page * n_heads, hd)
    cv = cache_v.reshape(n_phys, page * n_heads, hd)

    def page_spec(p):
        return pl.BlockSpec((None, page * n_heads, hd),
                            lambda b, s, pt: (pt[b * n_pages + 2 * bps * s + p], 0, 0))

    new_spec = pl.BlockSpec((tq, width), lambda b, s, pt: (b, 0))

    pages = [page_spec(p) for p in range(2 * bps)]
    rows = n_heads * tq
    per_block = (n_past, rows, hd)
    grid_spec = pltpu.PrefetchScalarGridSpec(
        num_scalar_prefetch=1,
        grid=(batch, n_past // bps),
        in_specs=[new_spec] + pages + pages + [new_spec, new_spec],
        out_specs=pl.BlockSpec((tq, width), lambda b, s, pt: (b, 0)),
        scratch_shapes=[pltpu.VMEM((rows, hd), F32), pltpu.VMEM((rows, MOBA_BLOCK * n_heads), F32),
                        pltpu.VMEM((LANES * n_heads, hd), F32), pltpu.VMEM(per_block, F32),
                        pltpu.VMEM(per_block, F32), pltpu.VMEM(per_block, F32)],
    )
    return pl.pallas_call(
        functools.partial(_attn_decode_kernel, n_heads=n_heads, hd=hd, past_len=n_pages * page, bps=bps,
                          slopes=slopes),
        grid_spec=grid_spec,
        out_shape=jax.ShapeDtypeStruct((batch * tq, width), F32),
        compiler_params=_cparams(("arbitrary", "arbitrary")),
        name="attn_decode",
    )(page_table.reshape(-1), q_new, *([ck] * (2 * bps)), *([cv] * (2 * bps)), k_new, v_new)


def _mlstm_kernel(q_ref, k_ref, v_ref, og_ref, gcol_ref, grow_ref, c0_ref, n0_ref, m0_ref, gain_ref,
                  hm_ref, c_ref, n_ref, m_ref, *, chunk, hps):
    t = q_ref.shape[0]
    d = q_ref.shape[1] // hps
    lc = chunk
    c_ref[...] = c0_ref[...]
    n_ref[...] = n0_ref[...]
    m_ref[...] = m0_ref[...]
    r = lax.broadcasted_iota(jnp.int32, (lc, lc), 0)
    c = lax.broadcasted_iota(jnp.int32, (lc, lc), 1)
    lower = r >= c

    def one_chunk(hh, rows):
        cols = slice(hh * d, (hh + 1) * d)
        q = q_ref[rows, cols].astype(F32)
        k = k_ref[rows, cols].astype(F32) * (d ** -0.5)
        v = v_ref[rows, cols].astype(F32)
        gc = gcol_ref[hh, rows, :]
        ig_c, lf_c = gc[:, 0:1], gc[:, 1:2]
        gr = grow_ref[hh, :, rows]
        ig_r, lf_r = gr[0:1, :], gr[1:2, :]
        cmat, nvec, m0 = c_ref[hh], n_ref[hh], m_ref[hh]

        b_c = jnp.sum(jnp.where(lower, lf_r, 0.0), axis=1, keepdims=True)
        b_r = jnp.sum(jnp.where(r <= c, lf_c, 0.0), axis=0, keepdims=True)
        dmat = jnp.where(lower, b_c - b_r + ig_r, NEG)
        inter = b_c + m0
        mt = jnp.maximum(inter, jnp.max(dmat, axis=1, keepdims=True))
        w = jnp.exp(dmat - mt)
        a = jnp.exp(inter - mt)
        qb, kb, vb = q.astype(BF16), k.astype(BF16), v.astype(BF16)
        sc = _dot_nt(qb, kb) * w
        num = a * _dot_nt(qb, cmat.astype(BF16)) + _dot(sc.astype(BF16), vb)
        den = a * jnp.sum(q * nvec, axis=1, keepdims=True) + jnp.sum(sc, axis=1, keepdims=True)
        h = num / jnp.maximum(jnp.abs(den), jnp.exp(-mt))
        hn = _rms(h) * gain_ref[hh]
        hm_ref[rows, cols] = (hn * _sigmoid(og_ref[rows, cols].astype(F32))).astype(hm_ref.dtype)

        b_last = b_c[lc - 1:lc, :]
        g = b_last - b_c + ig_c
        m_new = jnp.maximum(b_last + m0, jnp.max(g, axis=0, keepdims=True))
        ws = jnp.exp(g - m_new)
        a_last = jnp.exp(b_last + m0 - m_new)
        c_ref[hh] = a_last * cmat + _dot_tn((v * ws).astype(BF16), kb)
        n_ref[hh] = a_last * nvec + jnp.sum(ws * k, axis=0, keepdims=True)
        m_ref[hh] = m_new

    for hh in range(hps):
        if t == lc:
            one_chunk(hh, pl.ds(0, lc))
        else:
            def body(ci, carry, hh=hh):
                one_chunk(hh, pl.ds(pl.multiple_of(ci * lc, lc), lc))
                return carry

            lax.fori_loop(0, t // lc, body, 0)


def _mlstm(proj, gcol, grow, c0, n0, m0, gain, batch, t, n_heads, d, col0, hps, out_dtype):
    lc = math.gcd(t, MOBA_BLOCK)
    wb = hps * d
    assert col0 % wb == 0 and n_heads % hps == 0

    def col(off):
        return pl.BlockSpec((t, wb), lambda b, h: (b, (col0 + off * n_heads * d) // wb + h))

    per_head = lambda shape: pl.BlockSpec((None, hps) + shape, lambda b, h: (b, h, 0, 0))
    return pl.pallas_call(
        functools.partial(_mlstm_kernel, chunk=lc, hps=hps),
        grid=(batch, n_heads // hps),
        in_specs=[col(0), col(1), col(2), col(3), per_head((t, 2)), per_head((2, t)),
                  per_head((d, d)), per_head((1, d)), per_head((1, 1)),
                  pl.BlockSpec((hps, 1, d), lambda b, h: (h, 0, 0))],
        out_specs=[pl.BlockSpec((t, wb), lambda b, h: (b, h)), per_head((d, d)), per_head((1, d)),
                   per_head((1, 1))],
        out_shape=[jax.ShapeDtypeStruct((batch * t, n_heads * d), out_dtype),
                   jax.ShapeDtypeStruct((batch, n_heads, d, d), F32),
                   jax.ShapeDtypeStruct((batch, n_heads, 1, d), F32),
                   jax.ShapeDtypeStruct((batch, n_heads, 1, 1), F32)],
        compiler_params=_cparams(("arbitrary", "arbitrary")),
        name=f"mlstm_t{t}",
    )(proj, proj, proj, proj, gcol, grow, c0, n0, m0, gain)


def _route(logits, n_groups, n_experts):
    epg = n_experts // n_groups
    lane = lax.broadcasted_iota(jnp.int32, logits.shape, 1).astype(F32)
    big = float(LANES)

    def rmax(x):
        return jnp.max(x, axis=1, keepdims=True)

    def first_lane(mask):
        return jnp.min(jnp.where(mask, lane, big), axis=1, keepdims=True)

    is_g = (lane >= n_experts) & (lane < n_experts + n_groups)
    gmax = rmax(jnp.where(is_g, logits, -jnp.inf))
    g_w = 1.0 / jnp.sum(jnp.where(is_g, jnp.exp(logits - gmax), 0.0), axis=1, keepdims=True)
    group = first_lane(is_g & (logits == gmax)) - n_experts
    lo = group * epg
    in_grp = (lane >= lo) & (lane < lo + epg)
    emax = rmax(jnp.where(in_grp, logits, -jnp.inf))
    pe = jnp.where(in_grp, jnp.exp(logits - emax), 0.0)
    prob = pe / jnp.sum(pe, axis=1, keepdims=True)
    p1 = rmax(jnp.where(in_grp, prob, -1.0))
    i1 = first_lane(in_grp & (prob == p1))
    rest = in_grp & (lane != i1)
    p2 = rmax(jnp.where(rest, prob, -1.0))
    i2 = first_lane(rest & (prob == p2))
    den = p1 + p2
    gate = g_w * jnp.where(lane == i1, p1 / den, jnp.where(lane == i2, p2 / den, 0.0))
    return gate, group


def _outproj_kernel(n_p, att_w, n_groups, n_experts,
                    attp, atts, hmp, hms, xp, xs, g1p, g1s, sh2p, sh2s, sc2p, sc2s,
                    wo_ref, gffn_ref, wr_ref, br_ref, x1_ref, h2g_ref, rinfo_ref, counts_ref, carry_ref):
    d = x1_ref.shape[1]

    @pl.when(pl.program_id(0) == 0)
    def _():
        carry_ref[...] = jnp.zeros_like(carry_ref)

    def body(att_ref, hm_ref, x_ref, g1_ref, sh2_ref, sc2_ref):
        mixed = (_dot(att_ref[...].astype(BF16), wo_ref[0:att_w, :])
                 + _dot(hm_ref[...].astype(BF16), wo_ref[att_w:, :]))
        x1 = x_ref[...] + g1_ref[...] * mixed
        x1_ref[...] = x1
        h2 = _rms(x1) * gffn_ref[...] * (1.0 + sc2_ref[...]) + sh2_ref[...]
        gate, group = _route(_dot3(h2, wr_ref[...]) + br_ref[...], n_groups, n_experts)
        h2g_ref[:, :d] = h2
        h2g_ref[:, d:] = gate
        tm = h2.shape[0]
        lane = lax.broadcasted_iota(jnp.int32, (tm, LANES), 1).astype(F32)
        onehot = (lane == group).astype(F32)
        r = lax.broadcasted_iota(jnp.int32, (tm, tm), 0)
        c = lax.broadcasted_iota(jnp.int32, (tm, tm), 1)
        before = _dot((r > c).astype(BF16), onehot.astype(BF16)) + carry_ref[...]
        rank = jnp.sum(onehot * before, axis=1, keepdims=True)
        rinfo_ref[...] = jnp.where(lane == 0.0, group, jnp.where(lane == 1.0, rank, 0.0))
        carry_ref[...] += jnp.sum(onehot, axis=0, keepdims=True)
        counts_ref[...] = carry_ref[...]

    _dual(n_p, body, (attp, hmp, xp, g1p, sh2p, sc2p), (atts, hms, xs, g1s, sh2s, sc2s))


def _outproj_stage(att_p, att_s, hm_p, hm_s, x_p, x_s, g1_p, g1_s, sh2_p, sh2_s, sc2_p, sc2_s,
                   w_out_bf, g_ffn, w_route, b_route, n_groups, n_experts, t_p):
    tm = TOKEN_TILE
    d = x_p.shape[1]
    att_w = att_p.shape[1]
    ml_w = hm_p.shape[1]
    n_p, n_s = x_p.shape[0] // tm, x_s.shape[0] // tm
    n_tok = x_p.shape[0] + x_s.shape[0]
    prow = lambda w: pl.BlockSpec((tm, w), _p_rows(n_p))
    srow = lambda w: pl.BlockSpec((tm, w), _s_rows(n_p))
    pbat = pl.BlockSpec((None, 1, d), _p_batch(n_p, t_p // tm))
    const = lambda shape: pl.BlockSpec(shape, lambda i: (0,) * len(shape))
    out_row = lambda w: pl.BlockSpec((tm, w), lambda i: (i, 0))
    return pl.pallas_call(
        functools.partial(_outproj_kernel, n_p, att_w, n_groups, n_experts),
        grid=(n_p + n_s,),
        in_specs=[prow(att_w), srow(att_w), prow(ml_w), srow(ml_w), prow(d), srow(d),
                  pbat, srow(d), pbat, srow(d), pbat, srow(d),
                  const((att_w + ml_w, d)), const((1, d)), const((d, LANES)), const((1, LANES))],
        out_specs=[out_row(d), out_row(d + LANES), out_row(LANES), const((1, LANES))],
        out_shape=[jax.ShapeDtypeStruct((n_tok, d), F32), jax.ShapeDtypeStruct((n_tok, d + LANES), F32),
                   jax.ShapeDtypeStruct((n_tok, LANES), F32), jax.ShapeDtypeStruct((1, LANES), F32)],
        scratch_shapes=[pltpu.VMEM((1, LANES), F32)],
        compiler_params=_cparams(("arbitrary",)),
        name="out_projection_router",
    )(att_p, att_s, hm_p, hm_s, x_p, x_s, g1_p, g1_s, sh2_p, sh2_s, sc2_p, sc2_s,
      w_out_bf, g_ffn, w_route, b_route)


MOE_TILE = 512
ROW_DMA_UNROLL = 8


class _RowGather:
    def __init__(self, idx_ref, src_hbm, buf_ref, sem):
        self.idx_ref, self.src_hbm, self.buf_ref, self.sem = idx_ref, src_hbm, buf_ref, sem
        self.tm = buf_ref.shape[1]

    def _copy(self, row_index, r, slot):
        return pltpu.make_async_copy(self.src_hbm.at[pl.ds(row_index, 1)],
                                     self.buf_ref.at[slot, pl.ds(r, 1)], self.sem.at[slot])

    def start(self, tile, slot):
        def body(r, carry):
            self._copy(self.idx_ref[tile * self.tm + r], r, slot).start()
            return carry

        lax.fori_loop(0, self.tm, body, 0, unroll=ROW_DMA_UNROLL)

    def start_inline(self, tile, slot, first_row, count):
        for u in range(count):
            r = first_row + u
            self._copy(self.idx_ref[tile * self.tm + r], r, slot).start()

    def wait(self, slot):
        def body(r, carry):
            self._copy(0, r, slot).wait()
            return carry

        lax.fori_loop(0, self.tm, body, 0, unroll=ROW_DMA_UNROLL)


def _moe_kernel(tg_ref, nu_ref, src_ref, h2g_hbm, wg_ref, wu_ref, wd_ref, o_ref, hbuf_ref, sem, *, epg):
    t = pl.program_id(0)
    n_tiles = pl.num_programs(0)
    tm, d = o_ref.shape
    n_used = nu_ref[0]
    used = t < n_used
    slot = t % 2
    rows = _RowGather(src_ref, h2g_hbm, hbuf_ref, sem)

    @pl.when(t == 0)
    def _():
        rows.start(0, 0)

    @pl.when(t <= n_used)
    def _():
        rows.wait(slot)

    @pl.when(jnp.logical_not(used))
    def _():
        o_ref[...] = jnp.zeros_like(o_ref)

    @pl.when(used)
    def _():
        rows.start_inline(jnp.minimum(t + 1, n_tiles - 1), 1 - slot, 0, tm)
        h2 = hbuf_ref[slot, :, :d].astype(BF16)
        gate = hbuf_ref[slot, :, d:]
        lane = lax.broadcasted_iota(jnp.int32, gate.shape, 1)
        for j in range(epg):
            hg = _dot(h2, wg_ref[j])
            hu = _dot(h2, wu_ref[j])
            ge = jnp.sum(jnp.where(lane == tg_ref[t] * epg + j, gate, 0.0), axis=1, keepdims=True)
            y = _dot((hg * _sigmoid(hg) * hu * ge).astype(BF16), wd_ref[j])
            if j == 0:
                o_ref[...] = y
            else:
                o_ref[...] += y

    @pl.when(used & (t == n_tiles - 1))
    def _():
        rows.wait(1 - slot)


def _moe_stage(h2g, rinfo, counts, w_gate_bf, w_up_bf, w_down_bf, n_groups):
    n_tok, dw = h2g.shape
    d = dw - LANES
    n_experts, _, ff = w_gate_bf.shape
    epg = n_experts // n_groups
    tm = MOE_TILE
    n_tiles = (n_tok + n_groups * (tm - 1)) // tm

    group = rinfo[:, 0].astype(jnp.int32)
    rank = rinfo[:, 1].astype(jnp.int32)
    cnt = counts[0, :n_groups].astype(jnp.int32)
    tiles_g = (cnt + tm - 1) // tm
    tile_end = jnp.cumsum(tiles_g)
    row_start = (tile_end - tiles_g) * tm
    dest = row_start[group] + rank
    tile_group = jnp.minimum(jnp.searchsorted(tile_end, jnp.arange(n_tiles, dtype=jnp.int32), side="right"),
                             n_groups - 1).astype(jnp.int32)
    n_used = tile_end[-1:].astype(jnp.int32)
    src = jnp.zeros((n_tiles * tm,), jnp.int32).at[dest].set(jnp.arange(n_tok, dtype=jnp.int32))

    def w_spec(rows_, cols_):
        return pl.BlockSpec((None, epg, rows_, cols_),
                            lambda t, tg, nu, sr: (jnp.where(t < nu[0], tg[t], n_groups - 1), 0, 0, 0),
                            pipeline_mode=pl.Buffered(1))

    by_group = lambda w: w.reshape((n_groups, epg) + w.shape[1:])
    grid_spec = pltpu.PrefetchScalarGridSpec(
        num_scalar_prefetch=3,
        grid=(n_tiles,),
        in_specs=[pl.BlockSpec(memory_space=pl.ANY), w_spec(d, ff), w_spec(d, ff), w_spec(ff, d)],
        out_specs=pl.BlockSpec((tm, d), lambda t, tg, nu, sr: (t, 0)),
        scratch_shapes=[pltpu.VMEM((2, tm, dw), F32), pltpu.SemaphoreType.DMA((2,))],
    )
    ys = pl.pallas_call(
        functools.partial(_moe_kernel, epg=epg),
        grid_spec=grid_spec,
        out_shape=jax.ShapeDtypeStruct((n_tiles * tm, d), F32),
        compiler_params=_cparams(("arbitrary",)),
        name="moe_experts",
    )(tile_group, n_used, src, h2g, by_group(w_gate_bf), by_group(w_up_bf), by_group(w_down_bf))
    return ys, dest


def _final_kernel(n_p, dest_ref, x1_ref, moe_hbm, g2p, g2s, gfin_ref, yp_ref, ys_ref, mbuf_ref, sem):
    i = pl.program_id(0)
    slot = i % 2
    rows = _RowGather(dest_ref, moe_hbm, mbuf_ref, sem)

    @pl.when(i == 0)
    def _():
        rows.start(0, 0)

    rows.wait(slot)
    n = pl.num_programs(0)

    def body(g2_ref, y_ref):
        rows.start_inline(jnp.minimum(i + 1, n - 1), 1 - slot, 0, rows.tm)
        x2 = x1_ref[...] + g2_ref[...] * mbuf_ref[slot]
        y_ref[...] = _rms(x2) * gfin_ref[...]

    _dual(n_p, body, (g2p, yp_ref), (g2s, ys_ref))

    @pl.when(i == n - 1)
    def _():
        rows.wait(1 - slot)


def _final_stage(x1, moe_rows, dest, g2_p, g2_s, g_final, n_tok_p, t_p):
    tm = TOKEN_TILE
    n_tok, d = x1.shape
    n_p = n_tok_p // tm
    n_s = (n_tok - n_tok_p) // tm
    grid_spec = pltpu.PrefetchScalarGridSpec(
        num_scalar_prefetch=1,
        grid=(n_p + n_s,),
        in_specs=[pl.BlockSpec((tm, d), lambda i, ds: (i, 0)), pl.BlockSpec(memory_space=pl.ANY),
                  pl.BlockSpec((None, 1, d), lambda i, ds: _p_batch(n_p, t_p // tm)(i)),
                  pl.BlockSpec((tm, d), lambda i, ds: _s_rows(n_p)(i)), pl.BlockSpec((1, d), lambda i, ds: (0, 0))],
        out_specs=[pl.BlockSpec((tm, d), lambda i, ds: _p_rows(n_p)(i)),
                   pl.BlockSpec((tm, d), lambda i, ds: _s_rows(n_p)(i))],
        scratch_shapes=[pltpu.VMEM((2, tm, d), F32), pltpu.SemaphoreType.DMA((2,))],
    )
    return pl.pallas_call(
        functools.partial(_final_kernel, n_p),
        grid_spec=grid_spec,
        out_shape=[jax.ShapeDtypeStruct((n_tok_p, d), F32), jax.ShapeDtypeStruct((n_tok - n_tok_p, d), F32)],
        compiler_params=_cparams(("arbitrary",)),
        name="final_norm",
    )(dest, x1, moe_rows, g2_p, g2_s, g_final)


def kernel(x_prompt, x_sample, cache_k, cache_v, page_table, state_C, state_n, state_m, c_prompt, c_sample,
           w_in, b_ig, b_fg, ml_gain, w_out, g_mix, g_ffn, w_mod, b_mod, w_grp, b_grp, w_exp, b_exp,
           w_gate, w_up, w_down, g_final):
    bp, tp, d = x_prompt.shape
    bd, td, _ = x_sample.shape
    depth = w_in.shape[0]
    assert depth == 1, "one trunk layer"
    n_heads, hd = cache_k.shape[-2:]
    att_w = n_heads * hd
    nh_ml, d_ml = state_n.shape[-2:]
    ml_w = nh_ml * d_ml
    assert att_w == ml_w and att_w + ml_w == d
    n_groups, n_experts = w_grp.shape[-1], w_exp.shape[-1]
    assert n_experts + n_groups <= LANES and 2 * nh_ml <= LANES
    main_w = 3 * att_w + 4 * ml_w
    ntp, nts = bp * tp, bd * td

    c_rows = bp + bd
    c_pad = -c_rows % 16
    c_all = jnp.concatenate([c_prompt, c_sample, jnp.zeros((c_pad, d), F32)], axis=0)
    mod = _modulation(c_all, w_mod[0], b_mod[0])
    mod_p = mod[:bp].reshape(bp, N_MOD, 1, d)
    mod_s = jnp.repeat(mod[bp:c_rows].reshape(bd, N_MOD, d), td, axis=0)
    shift1_p, scale1_p, gate1_p, shift2_p, scale2_p, gate2_p = (mod_p[:, i] for i in range(N_MOD))
    shift1_s, scale1_s, gate1_s, shift2_s, scale2_s, gate2_s = (mod_s[:, i] for i in range(N_MOD))

    x_p = x_prompt.reshape(ntp, d)
    x_s = x_sample.reshape(nts, d)

    b_gates = jnp.pad(jnp.concatenate([b_ig[0], b_fg[0]]), (0, LANES - 2 * nh_ml)).reshape(1, LANES)
    w_in_t = w_in[0].T
    h_all, gates = _norm_stage(x_p, x_s, shift1_p, shift1_s, scale1_p, scale1_s, g_mix[0].reshape(1, d),
                               w_in_t, main_w, b_gates, nh_ml, tp)

    rest_p, k_p, v_p = _in_projection(h_all, w_in_t, 0, ntp, main_w, att_w, BF16, "in_projection_prompt")
    rest_s, k_s, v_s = _in_projection(h_all, w_in_t, ntp, nts, main_w, att_w, F32, "in_projection_decode")

    slopes = 2.0 ** (-8.0 * jnp.arange(1, n_heads + 1, dtype=F32) / n_heads)
    att_p = _attn_prompt(rest_p, k_p, v_p, slopes, bp, tp, n_heads, hd)
    att_s = _attn_decode(rest_s, k_s, v_s, cache_k, cache_v, page_table, bd, td, n_heads, hd)

    def gate_layouts(g, b, t):
        g = g[:, :2 * nh_ml].reshape(b, t, 2, nh_ml)
        return g.transpose(0, 3, 1, 2), g.transpose(0, 3, 2, 1)

    gcol_p, grow_p = gate_layouts(gates[:ntp], bp, tp)
    gcol_s, grow_s = gate_layouts(gates[ntp:], bd, td)
    gain = ml_gain[0].reshape(nh_ml, 1, d_ml)
    zeros = lambda *s: jnp.zeros(s, F32)
    hm_p, c_p, n_p, m_p = _mlstm(rest_p, gcol_p, grow_p, zeros(bp, nh_ml, d_ml, d_ml), zeros(bp, nh_ml, 1, d_ml),
                                 zeros(bp, nh_ml, 1, 1), gain, bp, tp, nh_ml, d_ml, att_w, 1, BF16)
    hm_s, c_s, n_s, m_s = _mlstm(rest_s, gcol_s, grow_s, state_C[0], state_n[0].reshape(bd, nh_ml, 1, d_ml),
                                 state_m[0].reshape(bd, nh_ml, 1, 1), gain, bd, td, nh_ml, d_ml, att_w, nh_ml, F32)

    w_route = jnp.pad(jnp.concatenate([w_exp[0], w_grp[0]], axis=1), ((0, 0), (0, LANES - n_experts - n_groups)))
    b_route = jnp.pad(jnp.concatenate([b_exp[0], b_grp[0]]), (0, LANES - n_experts - n_groups)).reshape(1, LANES)
    x1, h2g, rinfo, counts = _outproj_stage(att_p, att_s, hm_p, hm_s, x_p, x_s, gate1_p, gate1_s, shift2_p,
                                            shift2_s, scale2_p, scale2_s, w_out[0].astype(BF16),
                                            g_ffn[0].reshape(1, d), w_route, b_route, n_groups, n_experts, tp)

    moe_rows, dest = _moe_stage(h2g, rinfo, counts, w_gate[0].astype(BF16), w_up[0].astype(BF16),
                                w_down[0].astype(BF16), n_groups)
    y_p, y_s = _final_stage(x1, moe_rows, dest, gate2_p, gate2_s, g_final.reshape(1, d), ntp, tp)

    kv_shape_p = (1, bp, tp, n_heads, hd)
    kv_shape_s = (1, bd, td, n_heads, hd)
    return (y_p.reshape(bp, tp, d), y_s.reshape(bd, td, d),
            k_p.reshape(kv_shape_p), v_p.reshape(kv_shape_p), k_s.reshape(kv_shape_s), v_s.reshape(kv_shape_s),
            c_p[None], n_p.reshape(1, bp, nh_ml, d_ml), m_p.reshape(1, bp, nh_ml),
            c_s[None], n_s.reshape(1, bd, nh_ml, d_ml), m_s.reshape(1, bd, nh_ml))
```

```python
import functools
import math

import jax
import jax.numpy as jnp
from jax import lax
from jax.experimental import pallas as pl
from jax.experimental.pallas import tpu as pltpu

F32 = jnp.float32
BF16 = jnp.bfloat16

MOBA_BLOCK = 256
MOBA_TOPK = 3
EXPERT_TOPK = 2
N_MOD = 6
EPS = 1e-6
NEG = -1e30

LANES = 128
V7X_VMEM_LIMIT = 56 * 1024 * 1024
TOKEN_TILE = 256


def _cparams(sem):
    return pltpu.CompilerParams(dimension_semantics=sem, vmem_limit_bytes=V7X_VMEM_LIMIT)


def _pick_tile(n, candidates):
    for c in candidates:
        if n % c == 0:
            return c
    raise ValueError(f"no tile in {candidates} divides {n}")


def _dot(a, b):
    return jnp.dot(a, b, preferred_element_type=F32)


def _dot_nt(a, b):
    return lax.dot_general(a, b, (((1,), (1,)), ((), ())), preferred_element_type=F32)


def _dot_tn(a, b):
    return lax.dot_general(a, b, (((0,), (0,)), ((), ())), preferred_element_type=F32)


def _split(x):
    hi = x.astype(BF16)
    lo = (x - hi.astype(F32)).astype(BF16)
    return hi, lo


def _dot3(a, b, dot=_dot):
    a_hi, a_lo = _split(a)
    b_hi, b_lo = _split(b)
    return dot(a_hi, b_hi) + dot(a_lo, b_hi) + dot(a_hi, b_lo)


def _sigmoid(x):
    return 1.0 / (1.0 + jnp.exp(-x))


def _log_sigmoid(x):
    return jnp.minimum(x, 0.0) - jnp.log(1.0 + jnp.exp(-jnp.abs(x)))


def _rms(x):
    return x * lax.rsqrt(jnp.mean(x * x, axis=-1, keepdims=True) + EPS)


def _mod_kernel(c_ref, w_ref, b_ref, o_ref):
    o_ref[...] = _dot3(c_ref[...], w_ref[...]) + b_ref[...]


def _modulation(c_all, w_mod, b_mod):
    rows, d = c_all.shape
    n = w_mod.shape[1]
    tn = _pick_tile(n, (1024, 512, 256, 128))
    return pl.pallas_call(
        _mod_kernel,
        grid=(n // tn,),
        in_specs=[
            pl.BlockSpec((rows, d), lambda j: (0, 0)),
            pl.BlockSpec((d, tn), lambda j: (0, j)),
            pl.BlockSpec((1, tn), lambda j: (0, j)),
        ],
        out_specs=pl.BlockSpec((rows, tn), lambda j: (0, j)),
        out_shape=jax.ShapeDtypeStruct((rows, n), F32),
        compiler_params=_cparams(("arbitrary",)),
        name="modulation",
    )(c_all, w_mod, b_mod.reshape(1, n))


def _p_rows(n_p):
    return lambda i: (jnp.minimum(i, n_p - 1), 0)


def _p_batch(n_p, tiles_per_batch):
    return lambda i: (jnp.minimum(i, n_p - 1) // tiles_per_batch, 0, 0)


def _s_rows(n_p):
    return lambda i: (jnp.maximum(i - n_p, 0), 0)


def _dual(n_p, body, p_refs, s_refs):
    i = pl.program_id(0)

    @pl.when(i < n_p)
    def _():
        body(*p_refs)

    @pl.when(i >= n_p)
    def _():
        body(*s_refs)


def _norm_kernel(n_p, n_heads, xp, xs, shp, shs, scp, scs, g_ref, wg_ref, bg_ref, h_ref, gates_ref):
    def body(x_ref, sh_ref, sc_ref):
        h = _rms(x_ref[...]) * g_ref[...] * (1.0 + sc_ref[...]) + sh_ref[...]
        h_ref[...] = h.astype(BF16)
        wg = wg_ref[...]
        wg = jnp.concatenate([wg, jnp.zeros((LANES - wg.shape[0], wg.shape[1]), F32)], axis=0)
        g = _dot3(h, wg, _dot_nt) + bg_ref[...]
        lane = lax.broadcasted_iota(jnp.int32, g.shape, 1)
        gates_ref[...] = jnp.where(lane < n_heads, g, _log_sigmoid(g))

    _dual(n_p, body, (xp, shp, scp), (xs, shs, scs))


def _norm_stage(x_p, x_s, sh_p, sh_s, sc_p, sc_s, g_mix, w_in_t, gate_row0, b_gates, n_heads, t_p):
    tm = TOKEN_TILE
    d = x_p.shape[1]
    n_p, n_s = x_p.shape[0] // tm, x_s.shape[0] // tm
    n_tok = x_p.shape[0] + x_s.shape[0]
    n_gate = 2 * n_heads
    assert gate_row0 % n_gate == 0 and w_in_t.shape[0] - gate_row0 == n_gate and n_gate % 8 == 0
    row = pl.BlockSpec((tm, d), _p_rows(n_p))
    srow = pl.BlockSpec((tm, d), _s_rows(n_p))
    pbat = pl.BlockSpec((None, 1, d), _p_batch(n_p, t_p // tm))
    const = lambda shape: pl.BlockSpec(shape, lambda i: (0,) * len(shape))
    w_gate_cols = pl.BlockSpec((n_gate, d), lambda i: (gate_row0 // n_gate, 0))
    return pl.pallas_call(
        functools.partial(_norm_kernel, n_p, n_heads),
        grid=(n_p + n_s,),
        in_specs=[row, srow, pbat, srow, pbat, srow, const((1, d)), w_gate_cols, const((1, LANES))],
        out_specs=[pl.BlockSpec((tm, d), lambda i: (i, 0)), pl.BlockSpec((tm, LANES), lambda i: (i, 0))],
        out_shape=[jax.ShapeDtypeStruct((n_tok, d), BF16), jax.ShapeDtypeStruct((n_tok, LANES), F32)],
        compiler_params=_cparams(("arbitrary",)),
        name="norm_gates",
    )(x_p, x_s, sh_p, sh_s, sc_p, sc_s, g_mix, w_in_t, b_gates)


INPROJ_COL_TILE = 1024


def _inproj_kernel(h_ref, wt_ref, rest_ref, k_ref, v_ref, wbf_ref, *, k_tiles, v_tiles):
    j = pl.program_id(0)

    @pl.when(pl.program_id(1) == 0)
    def _():
        wbf_ref[...] = wt_ref[...].T.astype(BF16)

    acc = _dot(h_ref[...], wbf_ref[...])
    is_k = (j >= k_tiles[0]) & (j < k_tiles[1])
    is_v = (j >= v_tiles[0]) & (j < v_tiles[1])

    @pl.when(is_k)
    def _():
        k_ref[...] = acc

    @pl.when(is_v)
    def _():
        v_ref[...] = acc

    @pl.when(jnp.logical_not(is_k | is_v))
    def _():
        rest_ref[...] = acc.astype(rest_ref.dtype)


def _in_projection(h_all, w_in_t, row0, n_rows, width, att_w, rest_dtype, name):
    d = h_all.shape[1]
    tn = math.gcd(att_w, INPROJ_COL_TILE)
    tm = _pick_tile(n_rows, (1024, 512, 256))
    n_i = n_rows // tm
    rb = row0 // tm
    assert row0 % tm == 0
    last = n_i - 1
    per = att_w // tn
    k_tiles, v_tiles = (per, 2 * per), (2 * per, 3 * per)

    def rest_index(j, i):
        skipping = (j >= k_tiles[0]) & (j < v_tiles[1])
        return (jnp.where(skipping, last, i),
                jnp.where(j < k_tiles[0], j, jnp.where(skipping, k_tiles[0] - 1, j - 2 * per)))

    def own_index(tiles):
        lo, hi = tiles
        return lambda j, i: (jnp.where(j < lo, 0, jnp.where(j < hi, i, last)),
                             jnp.where(j < lo, 0, jnp.where(j < hi, j - lo, hi - lo - 1)))

    return pl.pallas_call(
        functools.partial(_inproj_kernel, k_tiles=k_tiles, v_tiles=v_tiles),
        grid=(width // tn, n_i),
        in_specs=[pl.BlockSpec((tm, d), lambda j, i: (rb + i, 0)), pl.BlockSpec((tn, d), lambda j, i: (j, 0))],
        out_specs=[pl.BlockSpec((tm, tn), rest_index), pl.BlockSpec((tm, tn), own_index(k_tiles)),
                   pl.BlockSpec((tm, tn), own_index(v_tiles))],
        out_shape=[jax.ShapeDtypeStruct((n_rows, width - 2 * att_w), rest_dtype),
                   jax.ShapeDtypeStruct((n_rows, att_w), F32), jax.ShapeDtypeStruct((n_rows, att_w), F32)],
        scratch_shapes=[pltpu.VMEM((d, tn), BF16)],
        compiler_params=_cparams(("arbitrary", "arbitrary")),
        name=name,
    )(h_all, w_in_t)


def _prompt_attention(slope, tiles, first, q_ref, k_ref, v_ref, o_ref, kbf_ref, vaug_ref, sel_ref, alibi_ref):
    t, hd = q_ref.shape
    blk = MOBA_BLOCK
    nb = t // blk
    scale = hd ** -0.5

    if first:
        k = k_ref[...]
        kbf_ref[...] = (k * scale).astype(BF16)
        vaug_ref[:, :hd] = v_ref[...].astype(BF16)
        vaug_ref[:, hd:] = jnp.ones((t, hd), BF16)

        kmean = jnp.mean(k.reshape(nb, blk, hd), axis=1)
        gate = _dot3(kmean, q_ref[...].astype(F32), _dot_nt)
        kb = lax.broadcasted_iota(jnp.int32, (nb, t), 0)
        qb = lax.broadcasted_iota(jnp.int32, (nb, t), 1) // blk
        valid = kb < qb
        g = jnp.where(valid, gate, -jnp.inf)
        rank = jnp.zeros((nb, t), jnp.int32)
        for n2 in range(nb):
            gn = g[n2:n2 + 1, :]
            rank += ((gn > g) | ((gn == g) & (n2 < kb))).astype(jnp.int32)
        allowed = (valid & (rank < MOBA_TOPK)).astype(F32)
        sel_ref[...] = jnp.concatenate([allowed, jnp.zeros((LANES - nb, t), F32)], axis=0).T.astype(BF16)
        alibi_ref[...] = slope * (lax.broadcasted_iota(jnp.int32, (blk, t), 0)
                                  - lax.broadcasted_iota(jnp.int32, (blk, t), 1)).astype(F32)

    causal = (lax.broadcasted_iota(jnp.int32, (blk, blk), 0) >= lax.broadcasted_iota(jnp.int32, (blk, blk), 1))
    for i in tiles:
        w = (i + 1) * blk
        rows = pl.ds(i * blk, blk)
        s = _dot_nt(q_ref[rows, :].astype(BF16), kbf_ref[0:w, :]) - alibi_ref[:, 0:w]
        ok = causal
        if i > 0:
            expand = (lax.broadcasted_iota(jnp.int32, (LANES, i * blk), 0)
                      == lax.broadcasted_iota(jnp.int32, (LANES, i * blk), 1) // blk).astype(BF16)
            past_ok = _dot(sel_ref[rows, :], expand) > 0.5
            ok = jnp.concatenate([past_ok, causal], axis=1)
        s = jnp.where(ok, s, NEG)
        p = jnp.exp(s - jnp.max(s, axis=1, keepdims=True)).astype(BF16)
        out = _dot(p, vaug_ref[0:w, :])
        o_ref[rows, :] = (out[:, :hd] / out[:, hd:]).astype(o_ref.dtype)


DECODE_BLOCKS_PER_STEP = 8
PACKED_ROWS = 16


def _decode_attention(step, n_steps, q_ref, k_refs, v_refs, kn_ref, vn_ref, o_ref,
                      qflat_ref, mb_ref, ksum_ref, m_ref, l_ref, acc_ref, *, n_heads, hd, past_len, bps, slopes):
    tq = q_ref.shape[0]
    blk = MOBA_BLOCK
    scale = hd ** -0.5
    n_past = acc_ref.shape[0]
    rowpad = jnp.zeros((PACKED_ROWS - tq, hd), F32)
    rows = n_heads * tq
    bw = blk * n_heads
    row_head = lax.broadcasted_iota(jnp.int32, (rows, 1), 0) // tq
    slope_row = jnp.zeros((rows, 1), F32)
    for h in range(n_heads):
        slope_row = jnp.where(row_head == h, slopes[h], slope_row)

    @pl.when(step == 0)
    def _():
        qflat_ref[...] = jnp.concatenate([q_ref[:, h * hd:(h + 1) * hd] for h in range(n_heads)], axis=0) * scale
        ksum_ref[...] = jnp.zeros_like(ksum_ref)
        r = lax.broadcasted_iota(jnp.int32, (rows, bw), 0)
        c = lax.broadcasted_iota(jnp.int32, (rows, bw), 1)
        in_block_dist = (r % tq - c // n_heads).astype(F32)
        mb_ref[...] = jnp.where(c % n_heads == r // tq, -slope_row * in_block_dist, NEG)

    qb = qflat_ref[...].astype(BF16)
    for j in range(bps):
        n = step * bps + j
        k2 = jnp.concatenate([k_refs[2 * j][...], k_refs[2 * j + 1][...]], axis=0)
        v2 = jnp.concatenate([v_refs[2 * j][...], v_refs[2 * j + 1][...]], axis=0)
        ksum_ref[pl.ds(pl.multiple_of(n * n_heads, n_heads), n_heads), :] = jnp.sum(
            k2.reshape(blk, n_heads, hd), axis=0)
        s = _dot_nt(qb, k2.astype(BF16)) + mb_ref[...]
        m = jnp.max(s, axis=1, keepdims=True)
        p = jnp.exp(s - m)
        l = jnp.sum(p, axis=1, keepdims=True)
        acc_ref[n] = _dot(p.astype(BF16), v2.astype(BF16))
        block_dist = (past_len - n * blk).astype(F32)
        m_ref[n] = jnp.broadcast_to(m - slope_row * block_dist, (rows, hd))
        l_ref[n] = jnp.broadcast_to(l, (rows, hd))

    @pl.when(step == n_steps - 1)
    def _():
        lane = lax.broadcasted_iota(jnp.int32, (tq, LANES), 1)
        valid = lane < n_past
        orow = lax.broadcasted_iota(jnp.int32, (PACKED_ROWS, PACKED_ROWS), 0)
        ocol = lax.broadcasted_iota(jnp.int32, (PACKED_ROWS, PACKED_ROWS), 1)
        d_own = (orow - ocol).astype(F32)
        for h in range(n_heads):
            hs = slice(h * hd, (h + 1) * hd)
            rs = slice(h * tq, (h + 1) * tq)
            qf = jnp.concatenate([qflat_ref[rs, :], rowpad], axis=0)
            kmean = ksum_ref[pl.ds(h, LANES, stride=n_heads), :] * (1.0 / blk)
            gate = _dot3(qf[:tq], kmean, _dot_nt)
            g = jnp.where(valid, gate, -jnp.inf)
            rank = jnp.zeros((tq, LANES), jnp.int32)
            for n2 in range(n_past):
                gn = g[:, n2:n2 + 1]
                rank += ((gn > g) | ((gn == g) & (n2 < lane))).astype(jnp.int32)
            sel = (valid & (rank < MOBA_TOPK)).astype(F32)

            kn = jnp.concatenate([kn_ref[:, hs], rowpad], axis=0).astype(BF16)
            vn = jnp.concatenate([vn_ref[:, hs], rowpad], axis=0).astype(BF16)
            s_own = jnp.where(d_own >= 0.0, _dot_nt(qf.astype(BF16), kn) - slopes[h] * d_own, NEG)
            m_own = jnp.max(s_own, axis=1, keepdims=True)
            p_own = jnp.exp(s_own - m_own)
            l_own = jnp.sum(p_own, axis=1, keepdims=True)[:tq]
            acc_own = _dot(p_own.astype(BF16), vn)[:tq]
            m_own = m_own[:tq]

            picked = [sel[:, n2:n2 + 1] > 0.0 for n2 in range(n_past)]
            m_fin = jnp.broadcast_to(m_own, (tq, hd))
            for n2 in range(n_past):
                m_fin = jnp.maximum(m_fin, jnp.where(picked[n2], m_ref[n2, rs, :], NEG))
            w_own = jnp.exp(m_own - m_fin)
            l_fin = w_own * l_own
            out = w_own * acc_own
            for n2 in range(n_past):
                w = jnp.where(picked[n2], jnp.exp(m_ref[n2, rs, :] - m_fin), 0.0)
                l_fin += w * l_ref[n2, rs, :]
                out += w * acc_ref[n2, rs, :]
            o_ref[:, hs] = out / l_fin


def _attention_kernel(pt_ref, slopes_ref, qn_ref, *refs, n_pages_step, decode_cfg, n_heads, tiles_per_step,
                      steps_per_head, n_work):
    del pt_ref
    k_refs, v_refs = refs[:n_pages_step], refs[n_pages_step:2 * n_pages_step]
    (kn_ref, vn_ref, qp_ref, kp_ref, vp_ref, od_ref, op_ref,
     qflat_ref, mb_ref, ksum_ref, m_ref, l_ref, acc_ref, kbf_ref, vaug_ref, sel_ref, alibi_ref) = refs[2 * n_pages_step:]
    step, n_steps = pl.program_id(1), pl.num_programs(1)
    _decode_attention(step, n_steps, qn_ref, k_refs, v_refs, kn_ref, vn_ref, od_ref,
                      qflat_ref, mb_ref, ksum_ref, m_ref, l_ref, acc_ref, **decode_cfg)

    g = pl.program_id(0) * n_steps + step
    phase = g % steps_per_head
    slope = slopes_ref[(g // steps_per_head) % n_heads]
    for ph in range(steps_per_head):
        @pl.when((g < n_work) & (phase == ph))
        def _(ph=ph):
            tiles = range(ph * tiles_per_step, (ph + 1) * tiles_per_step)
            _prompt_attention(slope, tiles, ph == 0, qp_ref, kp_ref, vp_ref, op_ref,
                              kbf_ref, vaug_ref, sel_ref, alibi_ref)


def _attention(q_p, k_p, v_p, slopes, batch_p, t, q_new, k_new, v_new, cache_k, cache_v, page_table,
               batch_d, tq, n_heads, hd):
    _, n_phys, page, _, _ = cache_k.shape
    width = n_heads * hd
    assert 2 * page == MOBA_BLOCK, "a MoBA block must span exactly two cache pages"
    assert tq <= PACKED_ROWS
    n_pages = page_table.shape[1]
    n_past = n_pages // 2
    bps = math.gcd(n_past, DECODE_BLOCKS_PER_STEP)
    n_steps = n_past // bps
    assert n_past <= LANES
    slopes_py = tuple(2.0 ** (-8.0 * (h + 1) / n_heads) for h in range(n_heads))

    nb = t // MOBA_BLOCK
    heads_p = batch_p * n_heads
    total_steps = batch_d * n_steps
    tiles_per_step = next(c for c in range(1, nb + 1) if nb % c == 0 and heads_p * (nb // c) <= total_steps)
    steps_per_head = nb // tiles_per_step
    n_work = heads_p * steps_per_head

    ck = cache_k.reshape(n_phys, page * n_heads, hd)
    cv = cache_v.reshape(n_phys, page * n_heads, hd)

    def page_spec(p):
        return pl.BlockSpec((None, page * n_heads, hd),
                            lambda b, s, pt, sl: (pt[b * n_pages + 2 * bps * s + p], 0, 0))

    def head_index(b, s, pt, sl):
        bh = jnp.minimum((b * n_steps + s) // steps_per_head, heads_p - 1)
        return (bh // n_heads, bh % n_heads)

    new_spec = pl.BlockSpec((tq, width), lambda b, s, pt, sl: (b, 0))
    head_spec = pl.BlockSpec((t, hd), head_index)
    pages = [page_spec(p) for p in range(2 * bps)]
    rows = n_heads * tq
    per_block = (n_past, rows, hd)
    grid_spec = pltpu.PrefetchScalarGridSpec(
        num_scalar_prefetch=2,
        grid=(batch_d, n_steps),
        in_specs=[new_spec] + pages + pages + [new_spec, new_spec, head_spec, head_spec, head_spec],
        out_specs=[new_spec, head_spec],
        scratch_shapes=[pltpu.VMEM((rows, hd), F32), pltpu.VMEM((rows, MOBA_BLOCK * n_heads), F32),
                        pltpu.VMEM((LANES * n_heads, hd), F32), pltpu.VMEM(per_block, F32),
                        pltpu.VMEM(per_block, F32), pltpu.VMEM(per_block, F32),
                        pltpu.VMEM((t, hd), BF16), pltpu.VMEM((t, 2 * hd), BF16), pltpu.VMEM((t, LANES), BF16),
                        pltpu.VMEM((MOBA_BLOCK, t), F32)],
    )
    decode_cfg = dict(n_heads=n_heads, hd=hd, past_len=n_pages * page, bps=bps, slopes=slopes_py)
    att_d, att_p = pl.pallas_call(
        functools.partial(_attention_kernel, n_pages_step=2 * bps, decode_cfg=decode_cfg, n_heads=n_heads,
                          tiles_per_step=tiles_per_step, steps_per_head=steps_per_head, n_work=n_work),
        grid_spec=grid_spec,
        out_shape=[jax.ShapeDtypeStruct((batch_d * tq, width), F32),
                   jax.ShapeDtypeStruct((batch_p * t, width), BF16)],
        compiler_params=_cparams(("arbitrary", "arbitrary")),
        name="attention",
    )(page_table.reshape(-1), slopes, q_new, *([ck] * (2 * bps)), *([cv] * (2 * bps)), k_new, v_new, q_p, k_p, v_p)
    return att_p, att_d


def _mlstm_kernel(q_ref, k_ref, v_ref, og_ref, gcol_ref, grow_ref, c0_ref, n0_ref, m0_ref, gain_ref,
                  hm_ref, c_ref, n_ref, m_ref, *, chunk, hps):
    t = q_ref.shape[0]
    d = q_ref.shape[1] // hps
    lc = chunk
    c_ref[...] = c0_ref[...]
    n_ref[...] = n0_ref[...]
    m_ref[...] = m0_ref[...]
    r = lax.broadcasted_iota(jnp.int32, (lc, lc), 0)
    c = lax.broadcasted_iota(jnp.int32, (lc, lc), 1)
    lower = r >= c

    def one_chunk(hh, rows):
        cols = slice(hh * d, (hh + 1) * d)
        q = q_ref[rows, cols].astype(F32)
        k = k_ref[rows, cols].astype(F32) * (d ** -0.5)
        v = v_ref[rows, cols].astype(F32)
        gc = gcol_ref[hh, rows, :]
        ig_c, lf_c = gc[:, 0:1], gc[:, 1:2]
        gr = grow_ref[hh, :, rows]
        ig_r, lf_r = gr[0:1, :], gr[1:2, :]
        cmat, nvec, m0 = c_ref[hh], n_ref[hh], m_ref[hh]

        b_c = jnp.sum(jnp.where(lower, lf_r, 0.0), axis=1, keepdims=True)
        b_r = jnp.sum(jnp.where(r <= c, lf_c, 0.0), axis=0, keepdims=True)
        dmat = jnp.where(lower, b_c - b_r + ig_r, NEG)
        inter = b_c + m0
        mt = jnp.maximum(inter, jnp.max(dmat, axis=1, keepdims=True))
        w = jnp.exp(dmat - mt)
        a = jnp.exp(inter - mt)
        qb, kb, vb = q.astype(BF16), k.astype(BF16), v.astype(BF16)
        sc = _dot_nt(qb, kb) * w
        num = a * _dot_nt(qb, cmat.astype(BF16)) + _dot(sc.astype(BF16), vb)
        den = a * jnp.sum(q * nvec, axis=1, keepdims=True) + jnp.sum(sc, axis=1, keepdims=True)
        h = num / jnp.maximum(jnp.abs(den), jnp.exp(-mt))
        hn = _rms(h) * gain_ref[hh]
        hm_ref[rows, cols] = (hn * _sigmoid(og_ref[rows, cols].astype(F32))).astype(hm_ref.dtype)

        b_last = b_c[lc - 1:lc, :]
        g = b_last - b_c + ig_c
        m_new = jnp.maximum(b_last + m0, jnp.max(g, axis=0, keepdims=True))
        ws = jnp.exp(g - m_new)
        a_last = jnp.exp(b_last + m0 - m_new)
        c_ref[hh] = a_last * cmat + _dot_tn((v * ws).astype(BF16), kb)
        n_ref[hh] = a_last * nvec + jnp.sum(ws * k, axis=0, keepdims=True)
        m_ref[hh] = m_new

    for hh in range(hps):
        if t == lc:
            one_chunk(hh, pl.ds(0, lc))
        else:
            def body(ci, carry, hh=hh):
                one_chunk(hh, pl.ds(pl.multiple_of(ci * lc, lc), lc))
                return carry

            lax.fori_loop(0, t // lc, body, 0)


def _mlstm(proj, gcol, grow, c0, n0, m0, gain, batch, t, n_heads, d, col0, hps, out_dtype):
    lc = math.gcd(t, MOBA_BLOCK)
    wb = hps * d
    assert col0 % wb == 0 and n_heads % hps == 0

    def col(off):
        return pl.BlockSpec((t, wb), lambda b, h: (b, (col0 + off * n_heads * d) // wb + h))

    per_head = lambda shape: pl.BlockSpec((None, hps) + shape, lambda b, h: (b, h, 0, 0))
    return pl.pallas_call(
        functools.partial(_mlstm_kernel, chunk=lc, hps=hps),
        grid=(batch, n_heads // hps),
        in_specs=[col(0), col(1), col(2), col(3), per_head((t, 2)), per_head((2, t)),
                  per_head((d, d)), per_head((1, d)), per_head((1, 1)),
                  pl.BlockSpec((hps, 1, d), lambda b, h: (h, 0, 0))],
        out_specs=[pl.BlockSpec((t, wb), lambda b, h: (b, h)), per_head((d, d)), per_head((1, d)),
                   per_head((1, 1))],
        out_shape=[jax.ShapeDtypeStruct((batch * t, n_heads * d), out_dtype),
                   jax.ShapeDtypeStruct((batch, n_heads, d, d), F32),
                   jax.ShapeDtypeStruct((batch, n_heads, 1, d), F32),
                   jax.ShapeDtypeStruct((batch, n_heads, 1, 1), F32)],
        compiler_params=_cparams(("arbitrary", "arbitrary")),
        name=f"mlstm_t{t}",
    )(proj, proj, proj, proj, gcol, grow, c0, n0, m0, gain)


def _route(logits, n_groups, n_experts):
    epg = n_experts // n_groups
    lane = lax.broadcasted_iota(jnp.int32, logits.shape, 1).astype(F32)
    big = float(LANES)

    def rmax(x):
        return jnp.max(x, axis=1, keepdims=True)

    def first_lane(mask):
        return jnp.min(jnp.where(mask, lane, big), axis=1, keepdims=True)

    is_g = (lane >= n_experts) & (lane < n_experts + n_groups)
    gmax = rmax(jnp.where(is_g, logits, -jnp.inf))
    g_w = 1.0 / jnp.sum(jnp.where(is_g, jnp.exp(logits - gmax), 0.0), axis=1, keepdims=True)
    group = first_lane(is_g & (logits == gmax)) - n_experts
    lo = group * epg
    in_grp = (lane >= lo) & (lane < lo + epg)
    emax = rmax(jnp.where(in_grp, logits, -jnp.inf))
    pe = jnp.where(in_grp, jnp.exp(logits - emax), 0.0)
    prob = pe / jnp.sum(pe, axis=1, keepdims=True)
    p1 = rmax(jnp.where(in_grp, prob, -1.0))
    i1 = first_lane(in_grp & (prob == p1))
    rest = in_grp & (lane != i1)
    p2 = rmax(jnp.where(rest, prob, -1.0))
    i2 = first_lane(rest & (prob == p2))
    den = p1 + p2
    gate = g_w * jnp.where(lane == i1, p1 / den, jnp.where(lane == i2, p2 / den, 0.0))
    return gate, group


def _outproj_kernel(n_p, att_w, n_groups, n_experts,
                    attp, atts, hmp, hms, xp, xs, g1p, g1s, sh2p, sh2s, sc2p, sc2s,
                    wo_ref, gffn_ref, wr_ref, br_ref, x1_ref, h2g_ref, rinfo_ref, counts_ref, carry_ref):
    d = x1_ref.shape[1]

    @pl.when(pl.program_id(0) == 0)
    def _():
        carry_ref[...] = jnp.zeros_like(carry_ref)

    def body(att_ref, hm_ref, x_ref, g1_ref, sh2_ref, sc2_ref):
        mixed = (_dot(att_ref[...].astype(BF16), wo_ref[0:att_w, :])
                 + _dot(hm_ref[...].astype(BF16), wo_ref[att_w:, :]))
        x1 = x_ref[...] + g1_ref[...] * mixed
        x1_ref[...] = x1
        h2 = _rms(x1) * gffn_ref[...] * (1.0 + sc2_ref[...]) + sh2_ref[...]
        gate, group = _route(_dot3(h2, wr_ref[...]) + br_ref[...], n_groups, n_experts)
        h2g_ref[:, :d] = h2
        h2g_ref[:, d:] = gate
        tm = h2.shape[0]
        lane = lax.broadcasted_iota(jnp.int32, (tm, LANES), 1).astype(F32)
        onehot = (lane == group).astype(F32)
        r = lax.broadcasted_iota(jnp.int32, (tm, tm), 0)
        c = lax.broadcasted_iota(jnp.int32, (tm, tm), 1)
        before = _dot((r > c).astype(BF16), onehot.astype(BF16)) + carry_ref[...]
        rank = jnp.sum(onehot * before, axis=1, keepdims=True)
        rinfo_ref[...] = jnp.where(lane == 0.0, group, jnp.where(lane == 1.0, rank, 0.0))
        carry_ref[...] += jnp.sum(onehot, axis=0, keepdims=True)
        counts_ref[...] = carry_ref[...]

    _dual(n_p, body, (attp, hmp, xp, g1p, sh2p, sc2p), (atts, hms, xs, g1s, sh2s, sc2s))


def _outproj_stage(att_p, att_s, hm_p, hm_s, x_p, x_s, g1_p, g1_s, sh2_p, sh2_s, sc2_p, sc2_s,
                   w_out_bf, g_ffn, w_route, b_route, n_groups, n_experts, t_p):
    tm = TOKEN_TILE
    d = x_p.shape[1]
    att_w = att_p.shape[1]
    ml_w = hm_p.shape[1]
    n_p, n_s = x_p.shape[0] // tm, x_s.shape[0] // tm
    n_tok = x_p.shape[0] + x_s.shape[0]
    prow = lambda w: pl.BlockSpec((tm, w), _p_rows(n_p))
    srow = lambda w: pl.BlockSpec((tm, w), _s_rows(n_p))
    pbat = pl.BlockSpec((None, 1, d), _p_batch(n_p, t_p // tm))
    const = lambda shape: pl.BlockSpec(shape, lambda i: (0,) * len(shape))
    out_row = lambda w: pl.BlockSpec((tm, w), lambda i: (i, 0))
    return pl.pallas_call(
        functools.partial(_outproj_kernel, n_p, att_w, n_groups, n_experts),
        grid=(n_p + n_s,),
        in_specs=[prow(att_w), srow(att_w), prow(ml_w), srow(ml_w), prow(d), srow(d),
                  pbat, srow(d), pbat, srow(d), pbat, srow(d),
                  const((att_w + ml_w, d)), const((1, d)), const((d, LANES)), const((1, LANES))],
        out_specs=[out_row(d), out_row(d + LANES), out_row(LANES), const((1, LANES))],
        out_shape=[jax.ShapeDtypeStruct((n_tok, d), F32), jax.ShapeDtypeStruct((n_tok, d + LANES), F32),
                   jax.ShapeDtypeStruct((n_tok, LANES), F32), jax.ShapeDtypeStruct((1, LANES), F32)],
        scratch_shapes=[pltpu.VMEM((1, LANES), F32)],
        compiler_params=_cparams(("arbitrary",)),
        name="out_projection_router",
    )(att_p, att_s, hm_p, hm_s, x_p, x_s, g1_p, g1_s, sh2_p, sh2_s, sc2_p, sc2_s,
      w_out_bf, g_ffn, w_route, b_route)


MOE_TILE = 512
ROW_DMA_UNROLL = 8


class _RowGather:
    def __init__(self, idx_ref, src_hbm, buf_ref, sem):
        self.idx_ref, self.src_hbm, self.buf_ref, self.sem = idx_ref, src_hbm, buf_ref, sem
        self.tm = buf_ref.shape[1]

    def _copy(self, row_index, r, slot):
        return pltpu.make_async_copy(self.src_hbm.at[pl.ds(row_index, 1)],
                                     self.buf_ref.at[slot, pl.ds(r, 1)], self.sem.at[slot])

    def start(self, tile, slot):
        def body(r, carry):
            self._copy(self.idx_ref[tile * self.tm + r], r, slot).start()
            return carry

        lax.fori_loop(0, self.tm, body, 0, unroll=ROW_DMA_UNROLL)

    def start_inline(self, tile, slot, first_row, count):
        for u in range(count):
            r = first_row + u
            self._copy(self.idx_ref[tile * self.tm + r], r, slot).start()

    def wait(self, slot):
        def body(r, carry):
            self._copy(0, r, slot).wait()
            return carry

        lax.fori_loop(0, self.tm, body, 0, unroll=ROW_DMA_UNROLL)


def _moe_kernel(tg_ref, nu_ref, src_ref, h2g_hbm, wg_ref, wu_ref, wd_ref, o_ref, hbuf_ref, sem, *, epg):
    t = pl.program_id(0)
    n_tiles = pl.num_programs(0)
    tm, d = o_ref.shape
    n_used = nu_ref[0]
    used = t < n_used
    slot = t % 2
    rows = _RowGather(src_ref, h2g_hbm, hbuf_ref, sem)

    @pl.when(t == 0)
    def _():
        rows.start(0, 0)

    @pl.when(t <= n_used)
    def _():
        rows.wait(slot)

    @pl.when(jnp.logical_not(used))
    def _():
        o_ref[...] = jnp.zeros_like(o_ref)

    @pl.when(used)
    def _():
        rows.start_inline(jnp.minimum(t + 1, n_tiles - 1), 1 - slot, 0, tm)
        h2 = hbuf_ref[slot, :, :d].astype(BF16)
        gate = hbuf_ref[slot, :, d:]
        lane = lax.broadcasted_iota(jnp.int32, gate.shape, 1)
        for j in range(epg):
            hg = _dot(h2, wg_ref[j])
            hu = _dot(h2, wu_ref[j])
            ge = jnp.sum(jnp.where(lane == tg_ref[t] * epg + j, gate, 0.0), axis=1, keepdims=True)
            y = _dot((hg * _sigmoid(hg) * hu * ge).astype(BF16), wd_ref[j])
            if j == 0:
                o_ref[...] = y
            else:
                o_ref[...] += y

    @pl.when(used & (t == n_tiles - 1))
    def _():
        rows.wait(1 - slot)


def _moe_stage(h2g, rinfo, counts, w_gate_bf, w_up_bf, w_down_bf, n_groups):
    n_tok, dw = h2g.shape
    d = dw - LANES
    n_experts, _, ff = w_gate_bf.shape
    epg = n_experts // n_groups
    tm = MOE_TILE
    n_tiles = (n_tok + n_groups * (tm - 1)) // tm

    group = rinfo[:, 0].astype(jnp.int32)
    rank = rinfo[:, 1].astype(jnp.int32)
    cnt = counts[0, :n_groups].astype(jnp.int32)
    tiles_g = (cnt + tm - 1) // tm
    tile_end = jnp.cumsum(tiles_g)
    row_start = (tile_end - tiles_g) * tm
    dest = row_start[group] + rank
    tile_group = jnp.minimum(jnp.searchsorted(tile_end, jnp.arange(n_tiles, dtype=jnp.int32), side="right"),
                             n_groups - 1).astype(jnp.int32)
    n_used = tile_end[-1:].astype(jnp.int32)
    src = jnp.zeros((n_tiles * tm,), jnp.int32).at[dest].set(jnp.arange(n_tok, dtype=jnp.int32))

    def w_spec(rows_, cols_):
        return pl.BlockSpec((None, epg, rows_, cols_),
                            lambda t, tg, nu, sr: (jnp.where(t < nu[0], tg[t], n_groups - 1), 0, 0, 0),
                            pipeline_mode=pl.Buffered(1))

    by_group = lambda w: w.reshape((n_groups, epg) + w.shape[1:])
    grid_spec = pltpu.PrefetchScalarGridSpec(
        num_scalar_prefetch=3,
        grid=(n_tiles,),
        in_specs=[pl.BlockSpec(memory_space=pl.ANY), w_spec(d, ff), w_spec(d, ff), w_spec(ff, d)],
        out_specs=pl.BlockSpec((tm, d), lambda t, tg, nu, sr: (t, 0)),
        scratch_shapes=[pltpu.VMEM((2, tm, dw), F32), pltpu.SemaphoreType.DMA((2,))],
    )
    ys = pl.pallas_call(
        functools.partial(_moe_kernel, epg=epg),
        grid_spec=grid_spec,
        out_shape=jax.ShapeDtypeStruct((n_tiles * tm, d), F32),
        compiler_params=_cparams(("arbitrary",)),
        name="moe_experts",
    )(tile_group, n_used, src, h2g, by_group(w_gate_bf), by_group(w_up_bf), by_group(w_down_bf))
    return ys, dest


def _final_kernel(n_p, dest_ref, x1_ref, moe_hbm, g2p, g2s, gfin_ref, yp_ref, ys_ref, mbuf_ref, sem):
    i = pl.program_id(0)
    slot = i % 2
    rows = _RowGather(dest_ref, moe_hbm, mbuf_ref, sem)

    @pl.when(i == 0)
    def _():
        rows.start(0, 0)

    rows.wait(slot)
    n = pl.num_programs(0)

    def body(g2_ref, y_ref):
        rows.start_inline(jnp.minimum(i + 1, n - 1), 1 - slot, 0, rows.tm)
        x2 = x1_ref[...] + g2_ref[...] * mbuf_ref[slot]
        y_ref[...] = _rms(x2) * gfin_ref[...]

    _dual(n_p, body, (g2p, yp_ref), (g2s, ys_ref))

    @pl.when(i == n - 1)
    def _():
        rows.wait(1 - slot)


def _final_stage(x1, moe_rows, dest, g2_p, g2_s, g_final, n_tok_p, t_p):
    tm = TOKEN_TILE
    n_tok, d = x1.shape
    n_p = n_tok_p // tm
    n_s = (n_tok - n_tok_p) // tm
    grid_spec = pltpu.PrefetchScalarGridSpec(
        num_scalar_prefetch=1,
        grid=(n_p + n_s,),
        in_specs=[pl.BlockSpec((tm, d), lambda i, ds: (i, 0)), pl.BlockSpec(memory_space=pl.ANY),
                  pl.BlockSpec((None, 1, d), lambda i, ds: _p_batch(n_p, t_p // tm)(i)),
                  pl.BlockSpec((tm, d), lambda i, ds: _s_rows(n_p)(i)), pl.BlockSpec((1, d), lambda i, ds: (0, 0))],
        out_specs=[pl.BlockSpec((tm, d), lambda i, ds: _p_rows(n_p)(i)),
                   pl.BlockSpec((tm, d), lambda i, ds: _s_rows(n_p)(i))],
        scratch_shapes=[pltpu.VMEM((2, tm, d), F32), pltpu.SemaphoreType.DMA((2,))],
    )
    return pl.pallas_call(
        functools.partial(_final_kernel, n_p),
        grid_spec=grid_spec,
        out_shape=[jax.ShapeDtypeStruct((n_tok_p, d), F32), jax.ShapeDtypeStruct((n_tok - n_tok_p, d), F32)],
        compiler_params=_cparams(("arbitrary",)),
        name="final_norm",
    )(dest, x1, moe_rows, g2_p, g2_s, g_final)


def kernel(x_prompt, x_sample, cache_k, cache_v, page_table, state_C, state_n, state_m, c_prompt, c_sample,
           w_in, b_ig, b_fg, ml_gain, w_out, g_mix, g_ffn, w_mod, b_mod, w_grp, b_grp, w_exp, b_exp,
           w_gate, w_up, w_down, g_final):
    bp, tp, d = x_prompt.shape
    bd, td, _ = x_sample.shape
    depth = w_in.shape[0]
    assert depth == 1, "one trunk layer"
    n_heads, hd = cache_k.shape[-2:]
    att_w = n_heads * hd
    nh_ml, d_ml = state_n.shape[-2:]
    ml_w = nh_ml * d_ml
    assert att_w == ml_w and att_w + ml_w == d
    n_groups, n_experts = w_grp.shape[-1], w_exp.shape[-1]
    assert n_experts + n_groups <= LANES and 2 * nh_ml <= LANES
    main_w = 3 * att_w + 4 * ml_w
    ntp, nts = bp * tp, bd * td

    c_rows = bp + bd
    c_pad = -c_rows % 16
    c_all = jnp.concatenate([c_prompt, c_sample, jnp.zeros((c_pad, d), F32)], axis=0)
    mod = _modulation(c_all, w_mod[0], b_mod[0])
    mod_p = mod[:bp].reshape(bp, N_MOD, 1, d)
    mod_s = jnp.repeat(mod[bp:c_rows].reshape(bd, N_MOD, d), td, axis=0)
    shift1_p, scale1_p, gate1_p, shift2_p, scale2_p, gate2_p = (mod_p[:, i] for i in range(N_MOD))
    shift1_s, scale1_s, gate1_s, shift2_s, scale2_s, gate2_s = (mod_s[:, i] for i in range(N_MOD))

    x_p = x_prompt.reshape(ntp, d)
    x_s = x_sample.reshape(nts, d)

    b_gates = jnp.pad(jnp.concatenate([b_ig[0], b_fg[0]]), (0, LANES - 2 * nh_ml)).reshape(1, LANES)
    w_in_t = w_in[0].T
    h_all, gates = _norm_stage(x_p, x_s, shift1_p, shift1_s, scale1_p, scale1_s, g_mix[0].reshape(1, d),
                               w_in_t, main_w, b_gates, nh_ml, tp)

    rest_p, k_p, v_p = _in_projection(h_all, w_in_t, 0, ntp, main_w, att_w, BF16, "in_projection_prompt")
    rest_s, k_s, v_s = _in_projection(h_all, w_in_t, ntp, nts, main_w, att_w, F32, "in_projection_decode")

    slopes = 2.0 ** (-8.0 * jnp.arange(1, n_heads + 1, dtype=F32) / n_heads)
    att_p, att_s = _attention(rest_p, k_p, v_p, slopes, bp, tp, rest_s, k_s, v_s, cache_k, cache_v, page_table,
                              bd, td, n_heads, hd)

    def gate_layouts(g, b, t):
        g = g[:, :2 * nh_ml].reshape(b, t, 2, nh_ml)
        return g.transpose(0, 3, 1, 2), g.transpose(0, 3, 2, 1)

    gcol_p, grow_p = gate_layouts(gates[:ntp], bp, tp)
    gcol_s, grow_s = gate_layouts(gates[ntp:], bd, td)
    gain = ml_gain[0].reshape(nh_ml, 1, d_ml)
    zeros = lambda *s: jnp.zeros(s, F32)
    hm_p, c_p, n_p, m_p = _mlstm(rest_p, gcol_p, grow_p, zeros(bp, nh_ml, d_ml, d_ml), zeros(bp, nh_ml, 1, d_ml),
                                 zeros(bp, nh_ml, 1, 1), gain, bp, tp, nh_ml, d_ml, att_w, 1, BF16)
    hm_s, c_s, n_s, m_s = _mlstm(rest_s, gcol_s, grow_s, state_C[0], state_n[0].reshape(bd, nh_ml, 1, d_ml),
                                 state_m[0].reshape(bd, nh_ml, 1, 1), gain, bd, td, nh_ml, d_ml, att_w, nh_ml, F32)

    w_route = jnp.pad(jnp.concatenate([w_exp[0], w_grp[0]], axis=1), ((0, 0), (0, LANES - n_experts - n_groups)))
    b_route = jnp.pad(jnp.concatenate([b_exp[0], b_grp[0]]), (0, LANES - n_experts - n_groups)).reshape(1, LANES)
    x1, h2g, rinfo, counts = _outproj_stage(att_p, att_s, hm_p, hm_s, x_p, x_s, gate1_p, gate1_s, shift2_p,
                                            shift2_s, scale2_p, scale2_s, w_out[0].astype(BF16),
                                            g_ffn[0].reshape(1, d), w_route, b_route, n_groups, n_experts, tp)

    moe_rows, dest = _moe_stage(h2g, rinfo, counts, w_gate[0].astype(BF16), w_up[0].astype(BF16),
                                w_down[0].astype(BF16), n_groups)
    y_p, y_s = _final_stage(x1, moe_rows, dest, gate2_p, gate2_s, g_final.reshape(1, d), ntp, tp)

    kv_shape_p = (1, bp, tp, n_heads, hd)
    kv_shape_s = (1, bd, td, n_heads, hd)
    return (y_p.reshape(bp, tp, d), y_s.reshape(bd, td, d),
            k_p.reshape(kv_shape_p), v_p.reshape(kv_shape_p), k_s.reshape(kv_shape_s), v_s.reshape(kv_shape_s),
            c_p[None], n_p.reshape(1, bp, nh_ml, d_ml), m_p.reshape(1, bp, nh_ml),
            c_s[None], n_s.reshape(1, bd, nh_ml, d_ml), m_s.reshape(1, bd, nh_ml))
```

```python
import functools
import math

import jax
import jax.numpy as jnp
from jax import lax
from jax.experimental import pallas as pl
from jax.experimental.pallas import tpu as pltpu

F32 = jnp.float32
BF16 = jnp.bfloat16

MOBA_BLOCK = 256
MOBA_TOPK = 3
EXPERT_TOPK = 2
N_MOD = 6
EPS = 1e-6
NEG = -1e30

LANES = 128
V7X_VMEM_LIMIT = 56 * 1024 * 1024
TOKEN_TILE = 256


def _cparams(sem):
    return pltpu.CompilerParams(dimension_semantics=sem, vmem_limit_bytes=V7X_VMEM_LIMIT)


def _pick_tile(n, candidates):
    for c in candidates:
        if n % c == 0:
            return c
    raise ValueError(f"no tile in {candidates} divides {n}")


def _dot(a, b):
    return jnp.dot(a, b, preferred_element_type=F32)


def _dot_nt(a, b):
    return lax.dot_general(a, b, (((1,), (1,)), ((), ())), preferred_element_type=F32)


def _dot_tn(a, b):
    return lax.dot_general(a, b, (((0,), (0,)), ((), ())), preferred_element_type=F32)


def _split(x):
    hi = x.astype(BF16)
    lo = (x - hi.astype(F32)).astype(BF16)
    return hi, lo


def _dot3(a, b, dot=_dot):
    a_hi, a_lo = _split(a)
    b_hi, b_lo = _split(b)
    return dot(a_hi, b_hi) + dot(a_lo, b_hi) + dot(a_hi, b_lo)


def _sigmoid(x):
    return 1.0 / (1.0 + jnp.exp(-x))


def _log_sigmoid(x):
    return jnp.minimum(x, 0.0) - jnp.log(1.0 + jnp.exp(-jnp.abs(x)))


def _rms(x):
    return x * lax.rsqrt(jnp.mean(x * x, axis=-1, keepdims=True) + EPS)


def _mod_kernel(c_ref, w_ref, b_ref, o_ref):
    o_ref[...] = _dot3(c_ref[...], w_ref[...]) + b_ref[...]


def _modulation(c_all, w_mod, b_mod):
    rows, d = c_all.shape
    n = w_mod.shape[1]
    tn = _pick_tile(n, (1024, 512, 256, 128))
    return pl.pallas_call(
        _mod_kernel,
        grid=(n // tn,),
        in_specs=[
            pl.BlockSpec((rows, d), lambda j: (0, 0)),
            pl.BlockSpec((d, tn), lambda j: (0, j)),
            pl.BlockSpec((1, tn), lambda j: (0, j)),
        ],
        out_specs=pl.BlockSpec((rows, tn), lambda j: (0, j)),
        out_shape=jax.ShapeDtypeStruct((rows, n), F32),
        compiler_params=_cparams(("arbitrary",)),
        name="modulation",
    )(c_all, w_mod, b_mod.reshape(1, n))


def _p_rows(n_p):
    return lambda i: (jnp.minimum(i, n_p - 1), 0)


def _p_batch(n_p, tiles_per_batch):
    return lambda i: (jnp.minimum(i, n_p - 1) // tiles_per_batch, 0, 0)


def _s_rows(n_p):
    return lambda i: (jnp.maximum(i - n_p, 0), 0)


def _dual(n_p, body, p_refs, s_refs):
    i = pl.program_id(0)

    @pl.when(i < n_p)
    def _():
        body(*p_refs)

    @pl.when(i >= n_p)
    def _():
        body(*s_refs)


def _norm_kernel(n_p, n_heads, xp, xs, shp, shs, scp, scs, g_ref, wg_ref, bg_ref, h_ref, gates_ref):
    def body(x_ref, sh_ref, sc_ref):
        h = _rms(x_ref[...]) * g_ref[...] * (1.0 + sc_ref[...]) + sh_ref[...]
        h_ref[...] = h.astype(BF16)
        wg = wg_ref[...]
        wg = jnp.concatenate([wg, jnp.zeros((LANES - wg.shape[0], wg.shape[1]), F32)], axis=0)
        g = _dot3(h, wg, _dot_nt) + bg_ref[...]
        lane = lax.broadcasted_iota(jnp.int32, g.shape, 1)
        gates_ref[...] = jnp.where(lane < n_heads, g, _log_sigmoid(g))

    _dual(n_p, body, (xp, shp, scp), (xs, shs, scs))


def _norm_stage(x_p, x_s, sh_p, sh_s, sc_p, sc_s, g_mix, w_in_t, gate_row0, b_gates, n_heads, t_p):
    tm = TOKEN_TILE
    d = x_p.shape[1]
    n_p, n_s = x_p.shape[0] // tm, x_s.shape[0] // tm
    n_tok = x_p.shape[0] + x_s.shape[0]
    n_gate = 2 * n_heads
    assert gate_row0 % n_gate == 0 and w_in_t.shape[0] - gate_row0 == n_gate and n_gate % 8 == 0
    row = pl.BlockSpec((tm, d), _p_rows(n_p))
    srow = pl.BlockSpec((tm, d), _s_rows(n_p))
    pbat = pl.BlockSpec((None, 1, d), _p_batch(n_p, t_p // tm))
    const = lambda shape: pl.BlockSpec(shape, lambda i: (0,) * len(shape))
    w_gate_cols = pl.BlockSpec((n_gate, d), lambda i: (gate_row0 // n_gate, 0))
    return pl.pallas_call(
        functools.partial(_norm_kernel, n_p, n_heads),
        grid=(n_p + n_s,),
        in_specs=[row, srow, pbat, srow, pbat, srow, const((1, d)), w_gate_cols, const((1, LANES))],
        out_specs=[pl.BlockSpec((tm, d), lambda i: (i, 0)), pl.BlockSpec((tm, LANES), lambda i: (i, 0))],
        out_shape=[jax.ShapeDtypeStruct((n_tok, d), BF16), jax.ShapeDtypeStruct((n_tok, LANES), F32)],
        compiler_params=_cparams(("arbitrary",)),
        name="norm_gates",
    )(x_p, x_s, sh_p, sh_s, sc_p, sc_s, g_mix, w_in_t, b_gates)


INPROJ_COL_TILE = 1024


def _inproj_kernel(h_ref, wt_ref, rest_ref, k_ref, v_ref, wbf_ref, *, k_tiles, v_tiles):
    j = pl.program_id(0)

    @pl.when(pl.program_id(1) == 0)
    def _():
        wbf_ref[...] = wt_ref[...].T.astype(BF16)

    acc = _dot(h_ref[...], wbf_ref[...])
    is_k = (j >= k_tiles[0]) & (j < k_tiles[1])
    is_v = (j >= v_tiles[0]) & (j < v_tiles[1])

    @pl.when(is_k)
    def _():
        k_ref[...] = acc

    @pl.when(is_v)
    def _():
        v_ref[...] = acc

    @pl.when(jnp.logical_not(is_k | is_v))
    def _():
        rest_ref[...] = acc.astype(rest_ref.dtype)


def _in_projection(h_all, w_in_t, row0, n_rows, width, att_w, rest_dtype, name):
    d = h_all.shape[1]
    tn = math.gcd(att_w, INPROJ_COL_TILE)
    tm = _pick_tile(n_rows, (1024, 512, 256))
    n_i = n_rows // tm
    rb = row0 // tm
    assert row0 % tm == 0
    last = n_i - 1
    per = att_w // tn
    k_tiles, v_tiles = (per, 2 * per), (2 * per, 3 * per)

    def rest_index(j, i):
        skipping = (j >= k_tiles[0]) & (j < v_tiles[1])
        return (jnp.where(skipping, last, i),
                jnp.where(j < k_tiles[0], j, jnp.where(skipping, k_tiles[0] - 1, j - 2 * per)))

    def own_index(tiles):
        lo, hi = tiles
        return lambda j, i: (jnp.where(j < lo, 0, jnp.where(j < hi, i, last)),
                             jnp.where(j < lo, 0, jnp.where(j < hi, j - lo, hi - lo - 1)))

    return pl.pallas_call(
        functools.partial(_inproj_kernel, k_tiles=k_tiles, v_tiles=v_tiles),
        grid=(width // tn, n_i),
        in_specs=[pl.BlockSpec((tm, d), lambda j, i: (rb + i, 0)), pl.BlockSpec((tn, d), lambda j, i: (j, 0))],
        out_specs=[pl.BlockSpec((tm, tn), rest_index), pl.BlockSpec((tm, tn), own_index(k_tiles)),
                   pl.BlockSpec((tm, tn), own_index(v_tiles))],
        out_shape=[jax.ShapeDtypeStruct((n_rows, width - 2 * att_w), rest_dtype),
                   jax.ShapeDtypeStruct((n_rows, att_w), F32), jax.ShapeDtypeStruct((n_rows, att_w), F32)],
        scratch_shapes=[pltpu.VMEM((d, tn), BF16)],
        compiler_params=_cparams(("arbitrary", "arbitrary")),
        name=name,
    )(h_all, w_in_t)


def _prompt_attention(slope, tiles, first, q_ref, k_ref, v_ref, o_ref, kbf_ref, vaug_ref, sel_ref, alibi_ref):
    t, hd = q_ref.shape
    blk = MOBA_BLOCK
    nb = t // blk
    scale = hd ** -0.5

    if first:
        k = k_ref[...]
        kbf_ref[...] = (k * scale).astype(BF16)
        vaug_ref[:, :hd] = v_ref[...].astype(BF16)
        vaug_ref[:, hd:] = jnp.ones((t, hd), BF16)

        kmean = jnp.mean(k.reshape(nb, blk, hd), axis=1)
        gate = _dot3(kmean, q_ref[...].astype(F32), _dot_nt)
        kb = lax.broadcasted_iota(jnp.int32, (nb, t), 0)
        qb = lax.broadcasted_iota(jnp.int32, (nb, t), 1) // blk
        valid = kb < qb
        g = jnp.where(valid, gate, -jnp.inf)
        rank = jnp.zeros((nb, t), jnp.int32)
        for n2 in range(nb):
            gn = g[n2:n2 + 1, :]
            rank += ((gn > g) | ((gn == g) & (n2 < kb))).astype(jnp.int32)
        allowed = (valid & (rank < MOBA_TOPK)).astype(F32)
        sel_ref[...] = jnp.concatenate([allowed, jnp.zeros((LANES - nb, t), F32)], axis=0).T.astype(BF16)
        alibi_ref[...] = slope * (lax.broadcasted_iota(jnp.int32, (blk, t), 0)
                                  - lax.broadcasted_iota(jnp.int32, (blk, t), 1)).astype(F32)

    causal = (lax.broadcasted_iota(jnp.int32, (blk, blk), 0) >= lax.broadcasted_iota(jnp.int32, (blk, blk), 1))
    for i in tiles:
        w = (i + 1) * blk
        rows = pl.ds(i * blk, blk)
        s = _dot_nt(q_ref[rows, :].astype(BF16), kbf_ref[0:w, :]) - alibi_ref[:, 0:w]
        ok = causal
        if i > 0:
            expand = (lax.broadcasted_iota(jnp.int32, (LANES, i * blk), 0)
                      == lax.broadcasted_iota(jnp.int32, (LANES, i * blk), 1) // blk).astype(BF16)
            past_ok = _dot(sel_ref[rows, :], expand) > 0.5
            ok = jnp.concatenate([past_ok, causal], axis=1)
        s = jnp.where(ok, s, NEG)
        p = jnp.exp(s - jnp.max(s, axis=1, keepdims=True)).astype(BF16)
        out = _dot(p, vaug_ref[0:w, :])
        o_ref[rows, :] = (out[:, :hd] / out[:, hd:]).astype(o_ref.dtype)


DECODE_BLOCKS_PER_STEP = 8
PACKED_ROWS = 16


def _decode_attention(step, n_steps, q_ref, k_refs, v_refs, kn_ref, vn_ref, o_ref,
                      qflat_ref, mb_ref, ksum_ref, m_ref, l_ref, acc_ref, *, n_heads, hd, past_len, bps, slopes):
    tq = q_ref.shape[0]
    blk = MOBA_BLOCK
    scale = hd ** -0.5
    n_past = acc_ref.shape[0]
    rowpad = jnp.zeros((PACKED_ROWS - tq, hd), F32)
    page = blk // 2
    n_pairs = n_heads // 2
    pr = 2 * tq
    pk = 2 * blk
    pair_heads = [(g, g + n_pairs) for g in range(n_pairs)]

    def slope_col(g):
        second = lax.broadcasted_iota(jnp.int32, (pr, 1), 0) >= tq
        return jnp.where(second, slopes[pair_heads[g][1]], slopes[pair_heads[g][0]])

    @pl.when(step == 0)
    def _():
        qflat_ref[...] = jnp.concatenate(
            [q_ref[:, h * hd:(h + 1) * hd] for pair in pair_heads for h in pair], axis=0) * scale
        ksum_ref[...] = jnp.zeros_like(ksum_ref)
        r = lax.broadcasted_iota(jnp.int32, (pr, pk), 0)
        c = lax.broadcasted_iota(jnp.int32, (pr, pk), 1)
        in_block_dist = (r % tq - c // 2).astype(F32)
        for g in range(n_pairs):
            mb_ref[g] = jnp.where(c % 2 == r // tq, -slope_col(g) * in_block_dist, NEG)

    for j in range(bps):
        n = step * bps + j
        k0, k1 = k_refs[2 * j], k_refs[2 * j + 1]
        ksum_ref[pl.ds(pl.multiple_of(n * n_heads, n_heads), n_heads), :] = (
            jnp.sum(k0[...].reshape(page, n_heads, hd), axis=0) + jnp.sum(k1[...].reshape(page, n_heads, hd), axis=0))

    for g in range(n_pairs):
        pair_rows = pl.ds(g, 2 * page, stride=n_pairs)
        rs = slice(g * pr, (g + 1) * pr)
        kp = jnp.concatenate([kr[pair_rows, :] for kr in k_refs], axis=0).astype(BF16)
        s_all = _dot_nt(qflat_ref[rs, :].astype(BF16), kp)
        for j in range(bps):
            n = step * bps + j
            v0, v1 = v_refs[2 * j], v_refs[2 * j + 1]
            vp = jnp.concatenate([v0[pair_rows, :], v1[pair_rows, :]], axis=0).astype(BF16)
            s = s_all[:, j * pk:(j + 1) * pk] + mb_ref[g]
            m = jnp.max(s, axis=1, keepdims=True)
            p = jnp.exp(s - m)
            l = jnp.sum(p, axis=1, keepdims=True)
            acc_ref[n, rs, :] = _dot(p.astype(BF16), vp)
            block_dist = (past_len - n * blk).astype(F32)
            m_ref[n, rs, :] = jnp.broadcast_to(m - slope_col(g) * block_dist, (pr, hd))
            l_ref[n, rs, :] = jnp.broadcast_to(l, (pr, hd))

    @pl.when(step == n_steps - 1)
    def _():
        lane = lax.broadcasted_iota(jnp.int32, (tq, LANES), 1)
        valid = lane < n_past
        orow = lax.broadcasted_iota(jnp.int32, (PACKED_ROWS, PACKED_ROWS), 0)
        ocol = lax.broadcasted_iota(jnp.int32, (PACKED_ROWS, PACKED_ROWS), 1)
        d_own = (orow - ocol).astype(F32)
        for h in range(n_heads):
            hs = slice(h * hd, (h + 1) * hd)
            row0 = (h % n_pairs) * pr + (h // n_pairs) * tq
            rs = slice(row0, row0 + tq)
            qf = jnp.concatenate([qflat_ref[rs, :], rowpad], axis=0)
            kmean = ksum_ref[pl.ds(h, LANES, stride=n_heads), :] * (1.0 / blk)
            gate = _dot3(qf[:tq], kmean, _dot_nt)
            g = jnp.where(valid, gate, -jnp.inf)
            rank = jnp.zeros((tq, LANES), jnp.int32)
            for n2 in range(n_past):
                gn = g[:, n2:n2 + 1]
                rank += ((gn > g) | ((gn == g) & (n2 < lane))).astype(jnp.int32)
            sel = (valid & (rank < MOBA_TOPK)).astype(F32)

            kn = jnp.concatenate([kn_ref[:, hs], rowpad], axis=0).astype(BF16)
            vn = jnp.concatenate([vn_ref[:, hs], rowpad], axis=0).astype(BF16)
            s_own = jnp.where(d_own >= 0.0, _dot_nt(qf.astype(BF16), kn) - slopes[h] * d_own, NEG)
            m_own = jnp.max(s_own, axis=1, keepdims=True)
            p_own = jnp.exp(s_own - m_own)
            l_own = jnp.sum(p_own, axis=1, keepdims=True)[:tq]
            acc_own = _dot(p_own.astype(BF16), vn)[:tq]
            m_own = m_own[:tq]

            picked = [sel[:, n2:n2 + 1] > 0.0 for n2 in range(n_past)]
            m_fin = jnp.broadcast_to(m_own, (tq, hd))
            for n2 in range(n_past):
                m_fin = jnp.maximum(m_fin, jnp.where(picked[n2], m_ref[n2, rs, :], NEG))
            w_own = jnp.exp(m_own - m_fin)
            l_fin = w_own * l_own
            out = w_own * acc_own
            for n2 in range(n_past):
                w = jnp.where(picked[n2], jnp.exp(m_ref[n2, rs, :] - m_fin), 0.0)
                l_fin += w * l_ref[n2, rs, :]
                out += w * acc_ref[n2, rs, :]
            o_ref[:, hs] = out / l_fin


def _attention_kernel(pt_ref, slopes_ref, qn_ref, *refs, n_pages_step, decode_cfg, n_heads, tiles_per_step,
                      steps_per_head, n_work):
    del pt_ref
    k_refs, v_refs = refs[:n_pages_step], refs[n_pages_step:2 * n_pages_step]
    (kn_ref, vn_ref, qp_ref, kp_ref, vp_ref, od_ref, op_ref,
     qflat_ref, mb_ref, ksum_ref, m_ref, l_ref, acc_ref, kbf_ref, vaug_ref, sel_ref, alibi_ref) = refs[2 * n_pages_step:]
    step, n_steps = pl.program_id(1), pl.num_programs(1)
    _decode_attention(step, n_steps, qn_ref, k_refs, v_refs, kn_ref, vn_ref, od_ref,
                      qflat_ref, mb_ref, ksum_ref, m_ref, l_ref, acc_ref, **decode_cfg)

    g = pl.program_id(0) * n_steps + step
    phase = g % steps_per_head
    slope = slopes_ref[(g // steps_per_head) % n_heads]
    for ph in range(steps_per_head):
        @pl.when((g < n_work) & (phase == ph))
        def _(ph=ph):
            tiles = range(ph * tiles_per_step, (ph + 1) * tiles_per_step)
            _prompt_attention(slope, tiles, ph == 0, qp_ref, kp_ref, vp_ref, op_ref,
                              kbf_ref, vaug_ref, sel_ref, alibi_ref)


def _attention(q_p, k_p, v_p, slopes, batch_p, t, q_new, k_new, v_new, cache_k, cache_v, page_table,
               batch_d, tq, n_heads, hd):
    _, n_phys, page, _, _ = cache_k.shape
    width = n_heads * hd
    assert 2 * page == MOBA_BLOCK, "a MoBA block must span exactly two cache pages"
    assert tq <= PACKED_ROWS
    n_pages = page_table.shape[1]
    n_past = n_pages // 2
    bps = math.gcd(n_past, DECODE_BLOCKS_PER_STEP)
    n_steps = n_past // bps
    assert n_past <= LANES
    slopes_py = tuple(2.0 ** (-8.0 * (h + 1) / n_heads) for h in range(n_heads))

    nb = t // MOBA_BLOCK
    heads_p = batch_p * n_heads
    total_steps = batch_d * n_steps
    tiles_per_step = next(c for c in range(1, nb + 1) if nb % c == 0 and heads_p * (nb // c) <= total_steps)
    steps_per_head = nb // tiles_per_step
    n_work = heads_p * steps_per_head

    ck = cache_k.reshape(n_phys, page * n_heads, hd)
    cv = cache_v.reshape(n_phys, page * n_heads, hd)

    def page_spec(p):
        return pl.BlockSpec((None, page * n_heads, hd),
                            lambda b, s, pt, sl: (pt[b * n_pages + 2 * bps * s + p], 0, 0))

    def head_index(b, s, pt, sl):
        bh = jnp.minimum((b * n_steps + s) // steps_per_head, heads_p - 1)
        return (bh // n_heads, bh % n_heads)

    new_spec = pl.BlockSpec((tq, width), lambda b, s, pt, sl: (b, 0))
    head_spec = pl.BlockSpec((t, hd), head_index)
    pages = [page_spec(p) for p in range(2 * bps)]
    rows = n_heads * tq
    per_block = (n_past, rows, hd)
    grid_spec = pltpu.PrefetchScalarGridSpec(
        num_scalar_prefetch=2,
        grid=(batch_d, n_steps),
        in_specs=[new_spec] + pages + pages + [new_spec, new_spec, head_spec, head_spec, head_spec],
        out_specs=[new_spec, head_spec],
        scratch_shapes=[pltpu.VMEM((rows, hd), F32), pltpu.VMEM((n_heads // 2, 2 * tq, 2 * MOBA_BLOCK), F32),
                        pltpu.VMEM((LANES * n_heads, hd), F32), pltpu.VMEM(per_block, F32),
                        pltpu.VMEM(per_block, F32), pltpu.VMEM(per_block, F32),
                        pltpu.VMEM((t, hd), BF16), pltpu.VMEM((t, 2 * hd), BF16), pltpu.VMEM((t, LANES), BF16),
                        pltpu.VMEM((MOBA_BLOCK, t), F32)],
    )
    decode_cfg = dict(n_heads=n_heads, hd=hd, past_len=n_pages * page, bps=bps, slopes=slopes_py)
    att_d, att_p = pl.pallas_call(
        functools.partial(_attention_kernel, n_pages_step=2 * bps, decode_cfg=decode_cfg, n_heads=n_heads,
                          tiles_per_step=tiles_per_step, steps_per_head=steps_per_head, n_work=n_work),
        grid_spec=grid_spec,
        out_shape=[jax.ShapeDtypeStruct((batch_d * tq, width), F32),
                   jax.ShapeDtypeStruct((batch_p * t, width), BF16)],
        compiler_params=_cparams(("arbitrary", "arbitrary")),
        name="attention",
    )(page_table.reshape(-1), slopes, q_new, *([ck] * (2 * bps)), *([cv] * (2 * bps)), k_new, v_new, q_p, k_p, v_p)
    return att_p, att_d


def _mlstm_kernel(q_ref, k_ref, v_ref, og_ref, gcol_ref, grow_ref, c0_ref, n0_ref, m0_ref, gain_ref,
                  hm_ref, c_ref, n_ref, m_ref, *, chunk, hps):
    t = q_ref.shape[0]
    d = q_ref.shape[1] // hps
    lc = chunk
    c_ref[...] = c0_ref[...]
    n_ref[...] = n0_ref[...]
    m_ref[...] = m0_ref[...]
    r = lax.broadcasted_iota(jnp.int32, (lc, lc), 0)
    c = lax.broadcasted_iota(jnp.int32, (lc, lc), 1)
    lower = r >= c

    def one_chunk(hh, rows):
        cols = slice(hh * d, (hh + 1) * d)
        q = q_ref[rows, cols].astype(F32)
        k = k_ref[rows, cols].astype(F32) * (d ** -0.5)
        v = v_ref[rows, cols].astype(F32)
        gc = gcol_ref[hh, rows, :]
        ig_c, lf_c = gc[:, 0:1], gc[:, 1:2]
        gr = grow_ref[hh, :, rows]
        ig_r, lf_r = gr[0:1, :], gr[1:2, :]
        cmat, nvec, m0 = c_ref[hh], n_ref[hh], m_ref[hh]

        b_c = jnp.sum(jnp.where(lower, lf_r, 0.0), axis=1, keepdims=True)
        b_r = jnp.sum(jnp.where(r <= c, lf_c, 0.0), axis=0, keepdims=True)
        dmat = jnp.where(lower, b_c - b_r + ig_r, NEG)
        inter = b_c + m0
        mt = jnp.maximum(inter, jnp.max(dmat, axis=1, keepdims=True))
        w = jnp.exp(dmat - mt)
        a = jnp.exp(inter - mt)
        qb, kb, vb = q.astype(BF16), k.astype(BF16), v.astype(BF16)
        sc = _dot_nt(qb, kb) * w
        num = a * _dot_nt(qb, cmat.astype(BF16)) + _dot(sc.astype(BF16), vb)
        den = a * jnp.sum(q * nvec, axis=1, keepdims=True) + jnp.sum(sc, axis=1, keepdims=True)
        h = num / jnp.maximum(jnp.abs(den), jnp.exp(-mt))
        hn = _rms(h) * gain_ref[hh]
        hm_ref[rows, cols] = (hn * _sigmoid(og_ref[rows, cols].astype(F32))).astype(hm_ref.dtype)

        b_last = b_c[lc - 1:lc, :]
        g = b_last - b_c + ig_c
        m_new = jnp.maximum(b_last + m0, jnp.max(g, axis=0, keepdims=True))
        ws = jnp.exp(g - m_new)
        a_last = jnp.exp(b_last + m0 - m_new)
        c_ref[hh] = a_last * cmat + _dot_tn((v * ws).astype(BF16), kb)
        n_ref[hh] = a_last * nvec + jnp.sum(ws * k, axis=0, keepdims=True)
        m_ref[hh] = m_new

    for hh in range(hps):
        if t == lc:
            one_chunk(hh, pl.ds(0, lc))
        else:
            def body(ci, carry, hh=hh):
                one_chunk(hh, pl.ds(pl.multiple_of(ci * lc, lc), lc))
                return carry

            lax.fori_loop(0, t // lc, body, 0)


def _mlstm(proj, gcol, grow, c0, n0, m0, gain, batch, t, n_heads, d, col0, hps, out_dtype):
    lc = math.gcd(t, MOBA_BLOCK)
    wb = hps * d
    assert col0 % wb == 0 and n_heads % hps == 0

    def col(off):
        return pl.BlockSpec((t, wb), lambda b, h: (b, (col0 + off * n_heads * d) // wb + h))

    per_head = lambda shape: pl.BlockSpec((None, hps) + shape, lambda b, h: (b, h, 0, 0))
    return pl.pallas_call(
        functools.partial(_mlstm_kernel, chunk=lc, hps=hps),
        grid=(batch, n_heads // hps),
        in_specs=[col(0), col(1), col(2), col(3), per_head((t, 2)), per_head((2, t)),
                  per_head((d, d)), per_head((1, d)), per_head((1, 1)),
                  pl.BlockSpec((hps, 1, d), lambda b, h: (h, 0, 0))],
        out_specs=[pl.BlockSpec((t, wb), lambda b, h: (b, h)), per_head((d, d)), per_head((1, d)),
                   per_head((1, 1))],
        out_shape=[jax.ShapeDtypeStruct((batch * t, n_heads * d), out_dtype),
                   jax.ShapeDtypeStruct((batch, n_heads, d, d), F32),
                   jax.ShapeDtypeStruct((batch, n_heads, 1, d), F32),
                   jax.ShapeDtypeStruct((batch, n_heads, 1, 1), F32)],
        compiler_params=_cparams(("arbitrary", "arbitrary")),
        name=f"mlstm_t{t}",
    )(proj, proj, proj, proj, gcol, grow, c0, n0, m0, gain)


def _route(logits, n_groups, n_experts):
    epg = n_experts // n_groups
    lane = lax.broadcasted_iota(jnp.int32, logits.shape, 1).astype(F32)
    big = float(LANES)

    def rmax(x):
        return jnp.max(x, axis=1, keepdims=True)

    def first_lane(mask):
        return jnp.min(jnp.where(mask, lane, big), axis=1, keepdims=True)

    is_g = (lane >= n_experts) & (lane < n_experts + n_groups)
    gmax = rmax(jnp.where(is_g, logits, -jnp.inf))
    g_w = 1.0 / jnp.sum(jnp.where(is_g, jnp.exp(logits - gmax), 0.0), axis=1, keepdims=True)
    group = first_lane(is_g & (logits == gmax)) - n_experts
    lo = group * epg
    in_grp = (lane >= lo) & (lane < lo + epg)
    emax = rmax(jnp.where(in_grp, logits, -jnp.inf))
    pe = jnp.where(in_grp, jnp.exp(logits - emax), 0.0)
    prob = pe / jnp.sum(pe, axis=1, keepdims=True)
    p1 = rmax(jnp.where(in_grp, prob, -1.0))
    i1 = first_lane(in_grp & (prob == p1))
    rest = in_grp & (lane != i1)
    p2 = rmax(jnp.where(rest, prob, -1.0))
    i2 = first_lane(rest & (prob == p2))
    den = p1 + p2
    gate = g_w * jnp.where(lane == i1, p1 / den, jnp.where(lane == i2, p2 / den, 0.0))
    return gate, group


def _outproj_kernel(n_p, att_w, n_groups, n_experts,
                    attp, atts, hmp, hms, xp, xs, g1p, g1s, sh2p, sh2s, sc2p, sc2s,
                    wo_ref, gffn_ref, wr_ref, br_ref, x1_ref, h2g_ref, rinfo_ref, counts_ref, carry_ref):
    d = x1_ref.shape[1]

    @pl.when(pl.program_id(0) == 0)
    def _():
        carry_ref[...] = jnp.zeros_like(carry_ref)

    def body(att_ref, hm_ref, x_ref, g1_ref, sh2_ref, sc2_ref):
        mixed = (_dot(att_ref[...].astype(BF16), wo_ref[0:att_w, :])
                 + _dot(hm_ref[...].astype(BF16), wo_ref[att_w:, :]))
        x1 = x_ref[...] + g1_ref[...] * mixed
        x1_ref[...] = x1
        h2 = _rms(x1) * gffn_ref[...] * (1.0 + sc2_ref[...]) + sh2_ref[...]
        gate, group = _route(_dot3(h2, wr_ref[...]) + br_ref[...], n_groups, n_experts)
        h2g_ref[:, :d] = h2
        h2g_ref[:, d:] = gate
        tm = h2.shape[0]
        lane = lax.broadcasted_iota(jnp.int32, (tm, LANES), 1).astype(F32)
        onehot = (lane == group).astype(F32)
        r = lax.broadcasted_iota(jnp.int32, (tm, tm), 0)
        c = lax.broadcasted_iota(jnp.int32, (tm, tm), 1)
        before = _dot((r > c).astype(BF16), onehot.astype(BF16)) + carry_ref[...]
        rank = jnp.sum(onehot * before, axis=1, keepdims=True)
        rinfo_ref[...] = jnp.where(lane == 0.0, group, jnp.where(lane == 1.0, rank, 0.0))
        carry_ref[...] += jnp.sum(onehot, axis=0, keepdims=True)
        counts_ref[...] = carry_ref[...]

    _dual(n_p, body, (attp, hmp, xp, g1p, sh2p, sc2p), (atts, hms, xs, g1s, sh2s, sc2s))


def _outproj_stage(att_p, att_s, hm_p, hm_s, x_p, x_s, g1_p, g1_s, sh2_p, sh2_s, sc2_p, sc2_s,
                   w_out_bf, g_ffn, w_route, b_route, n_groups, n_experts, t_p):
    tm = TOKEN_TILE
    d = x_p.shape[1]
    att_w = att_p.shape[1]
    ml_w = hm_p.shape[1]
    n_p, n_s = x_p.shape[0] // tm, x_s.shape[0] // tm
    n_tok = x_p.shape[0] + x_s.shape[0]
    prow = lambda w: pl.BlockSpec((tm, w), _p_rows(n_p))
    srow = lambda w: pl.BlockSpec((tm, w), _s_rows(n_p))
    pbat = pl.BlockSpec((None, 1, d), _p_batch(n_p, t_p // tm))
    const = lambda shape: pl.BlockSpec(shape, lambda i: (0,) * len(shape))
    out_row = lambda w: pl.BlockSpec((tm, w), lambda i: (i, 0))
    return pl.pallas_call(
        functools.partial(_outproj_kernel, n_p, att_w, n_groups, n_experts),
        grid=(n_p + n_s,),
        in_specs=[prow(att_w), srow(att_w), prow(ml_w), srow(ml_w), prow(d), srow(d),
                  pbat, srow(d), pbat, srow(d), pbat, srow(d),
                  const((att_w + ml_w, d)), const((1, d)), const((d, LANES)), const((1, LANES))],
        out_specs=[out_row(d), out_row(d + LANES), out_row(LANES), const((1, LANES))],
        out_shape=[jax.ShapeDtypeStruct((n_tok, d), F32), jax.ShapeDtypeStruct((n_tok, d + LANES), F32),
                   jax.ShapeDtypeStruct((n_tok, LANES), F32), jax.ShapeDtypeStruct((1, LANES), F32)],
        scratch_shapes=[pltpu.VMEM((1, LANES), F32)],
        compiler_params=_cparams(("arbitrary",)),
        name="out_projection_router",
    )(att_p, att_s, hm_p, hm_s, x_p, x_s, g1_p, g1_s, sh2_p, sh2_s, sc2_p, sc2_s,
      w_out_bf, g_ffn, w_route, b_route)


MOE_TILE = 512
ROW_DMA_UNROLL = 8


class _RowGather:
    def __init__(self, idx_ref, src_hbm, buf_ref, sem):
        self.idx_ref, self.src_hbm, self.buf_ref, self.sem = idx_ref, src_hbm, buf_ref, sem
        self.tm = buf_ref.shape[1]

    def _copy(self, row_index, r, slot):
        return pltpu.make_async_copy(self.src_hbm.at[pl.ds(row_index, 1)],
                                     self.buf_ref.at[slot, pl.ds(r, 1)], self.sem.at[slot])

    def start(self, tile, slot):
        def body(r, carry):
            self._copy(self.idx_ref[tile * self.tm + r], r, slot).start()
            return carry

        lax.fori_loop(0, self.tm, body, 0, unroll=ROW_DMA_UNROLL)

    def start_inline(self, tile, slot, first_row, count):
        for u in range(count):
            r = first_row + u
            self._copy(self.idx_ref[tile * self.tm + r], r, slot).start()

    def wait(self, slot):
        def body(r, carry):
            self._copy(0, r, slot).wait()
            return carry

        lax.fori_loop(0, self.tm, body, 0, unroll=ROW_DMA_UNROLL)


def _moe_kernel(tg_ref, nu_ref, src_ref, h2g_hbm, wg_ref, wu_ref, wd_ref, o_ref, hbuf_ref, sem, *, epg):
    t = pl.program_id(0)
    n_tiles = pl.num_programs(0)
    tm, d = o_ref.shape
    n_used = nu_ref[0]
    used = t < n_used
    slot = t % 2
    rows = _RowGather(src_ref, h2g_hbm, hbuf_ref, sem)

    @pl.when(t == 0)
    def _():
        rows.start(0, 0)

    @pl.when(t <= n_used)
    def _():
        rows.wait(slot)

    @pl.when(jnp.logical_not(used))
    def _():
        o_ref[...] = jnp.zeros_like(o_ref)

    @pl.when(used)
    def _():
        rows.start_inline(jnp.minimum(t + 1, n_tiles - 1), 1 - slot, 0, tm)
        h2 = hbuf_ref[slot, :, :d].astype(BF16)
        gate = hbuf_ref[slot, :, d:]
        lane = lax.broadcasted_iota(jnp.int32, gate.shape, 1)
        for j in range(epg):
            hg = _dot(h2, wg_ref[j])
            hu = _dot(h2, wu_ref[j])
            ge = jnp.sum(jnp.where(lane == tg_ref[t] * epg + j, gate, 0.0), axis=1, keepdims=True)
            y = _dot((hg * _sigmoid(hg) * hu * ge).astype(BF16), wd_ref[j])
            if j == 0:
                o_ref[...] = y
            else:
                o_ref[...] += y

    @pl.when(used & (t == n_tiles - 1))
    def _():
        rows.wait(1 - slot)


def _moe_stage(h2g, rinfo, counts, w_gate_bf, w_up_bf, w_down_bf, n_groups):
    n_tok, dw = h2g.shape
    d = dw - LANES
    n_experts, _, ff = w_gate_bf.shape
    epg = n_experts // n_groups
    tm = MOE_TILE
    n_tiles = (n_tok + n_groups * (tm - 1)) // tm

    group = rinfo[:, 0].astype(jnp.int32)
    rank = rinfo[:, 1].astype(jnp.int32)
    cnt = counts[0, :n_groups].astype(jnp.int32)
    tiles_g = (cnt + tm - 1) // tm
    tile_end = jnp.cumsum(tiles_g)
    row_start = (tile_end - tiles_g) * tm
    dest = row_start[group] + rank
    tile_group = jnp.minimum(jnp.searchsorted(tile_end, jnp.arange(n_tiles, dtype=jnp.int32), side="right"),
                             n_groups - 1).astype(jnp.int32)
    n_used = tile_end[-1:].astype(jnp.int32)
    src = jnp.zeros((n_tiles * tm,), jnp.int32).at[dest].set(jnp.arange(n_tok, dtype=jnp.int32))

    def w_spec(rows_, cols_):
        return pl.BlockSpec((None, epg, rows_, cols_),
                            lambda t, tg, nu, sr: (jnp.where(t < nu[0], tg[t], n_groups - 1), 0, 0, 0),
                            pipeline_mode=pl.Buffered(1))

    by_group = lambda w: w.reshape((n_groups, epg) + w.shape[1:])
    grid_spec = pltpu.PrefetchScalarGridSpec(
        num_scalar_prefetch=3,
        grid=(n_tiles,),
        in_specs=[pl.BlockSpec(memory_space=pl.ANY), w_spec(d, ff), w_spec(d, ff), w_spec(ff, d)],
        out_specs=pl.BlockSpec((tm, d), lambda t, tg, nu, sr: (t, 0)),
        scratch_shapes=[pltpu.VMEM((2, tm, dw), F32), pltpu.SemaphoreType.DMA((2,))],
    )
    ys = pl.pallas_call(
        functools.partial(_moe_kernel, epg=epg),
        grid_spec=grid_spec,
        out_shape=jax.ShapeDtypeStruct((n_tiles * tm, d), F32),
        compiler_params=_cparams(("arbitrary",)),
        name="moe_experts",
    )(tile_group, n_used, src, h2g, by_group(w_gate_bf), by_group(w_up_bf), by_group(w_down_bf))
    return ys, dest


def _final_kernel(n_p, dest_ref, x1_ref, moe_hbm, g2p, g2s, gfin_ref, yp_ref, ys_ref, mbuf_ref, sem):
    i = pl.program_id(0)
    slot = i % 2
    rows = _RowGather(dest_ref, moe_hbm, mbuf_ref, sem)

    @pl.when(i == 0)
    def _():
        rows.start(0, 0)

    rows.wait(slot)
    n = pl.num_programs(0)

    def body(g2_ref, y_ref):
        rows.start_inline(jnp.minimum(i + 1, n - 1), 1 - slot, 0, rows.tm)
        x2 = x1_ref[...] + g2_ref[...] * mbuf_ref[slot]
        y_ref[...] = _rms(x2) * gfin_ref[...]

    _dual(n_p, body, (g2p, yp_ref), (g2s, ys_ref))

    @pl.when(i == n - 1)
    def _():
        rows.wait(1 - slot)


def _final_stage(x1, moe_rows, dest, g2_p, g2_s, g_final, n_tok_p, t_p):
    tm = TOKEN_TILE
    n_tok, d = x1.shape
    n_p = n_tok_p // tm
    n_s = (n_tok - n_tok_p) // tm
    grid_spec = pltpu.PrefetchScalarGridSpec(
        num_scalar_prefetch=1,
        grid=(n_p + n_s,),
        in_specs=[pl.BlockSpec((tm, d), lambda i, ds: (i, 0)), pl.BlockSpec(memory_space=pl.ANY),
                  pl.BlockSpec((None, 1, d), lambda i, ds: _p_batch(n_p, t_p // tm)(i)),
                  pl.BlockSpec((tm, d), lambda i, ds: _s_rows(n_p)(i)), pl.BlockSpec((1, d), lambda i, ds: (0, 0))],
        out_specs=[pl.BlockSpec((tm, d), lambda i, ds: _p_rows(n_p)(i)),
                   pl.BlockSpec((tm, d), lambda i, ds: _s_rows(n_p)(i))],
        scratch_shapes=[pltpu.VMEM((2, tm, d), F32), pltpu.SemaphoreType.DMA((2,))],
    )
    return pl.pallas_call(
        functools.partial(_final_kernel, n_p),
        grid_spec=grid_spec,
        out_shape=[jax.ShapeDtypeStruct((n_tok_p, d), F32), jax.ShapeDtypeStruct((n_tok - n_tok_p, d), F32)],
        compiler_params=_cparams(("arbitrary",)),
        name="final_norm",
    )(dest, x1, moe_rows, g2_p, g2_s, g_final)


def kernel(x_prompt, x_sample, cache_k, cache_v, page_table, state_C, state_n, state_m, c_prompt, c_sample,
           w_in, b_ig, b_fg, ml_gain, w_out, g_mix, g_ffn, w_mod, b_mod, w_grp, b_grp, w_exp, b_exp,
           w_gate, w_up, w_down, g_final):
    bp, tp, d = x_prompt.shape
    bd, td, _ = x_sample.shape
    depth = w_in.shape[0]
    assert depth == 1, "one trunk layer"
    n_heads, hd = cache_k.shape[-2:]
    att_w = n_heads * hd
    nh_ml, d_ml = state_n.shape[-2:]
    ml_w = nh_ml * d_ml
    assert att_w == ml_w and att_w + ml_w == d
    n_groups, n_experts = w_grp.shape[-1], w_exp.shape[-1]
    assert n_experts + n_groups <= LANES and 2 * nh_ml <= LANES
    main_w = 3 * att_w + 4 * ml_w
    ntp, nts = bp * tp, bd * td

    c_rows = bp + bd
    c_pad = -c_rows % 16
    c_all = jnp.concatenate([c_prompt, c_sample, jnp.zeros((c_pad, d), F32)], axis=0)
    mod = _modulation(c_all, w_mod[0], b_mod[0])
    mod_p = mod[:bp].reshape(bp, N_MOD, 1, d)
    mod_s = jnp.repeat(mod[bp:c_rows].reshape(bd, N_MOD, d), td, axis=0)
    shift1_p, scale1_p, gate1_p, shift2_p, scale2_p, gate2_p = (mod_p[:, i] for i in range(N_MOD))
    shift1_s, scale1_s, gate1_s, shift2_s, scale2_s, gate2_s = (mod_s[:, i] for i in range(N_MOD))

    x_p = x_prompt.reshape(ntp, d)
    x_s = x_sample.reshape(nts, d)

    b_gates = jnp.pad(jnp.concatenate([b_ig[0], b_fg[0]]), (0, LANES - 2 * nh_ml)).reshape(1, LANES)
    w_in_t = w_in[0].T
    h_all, gates = _norm_stage(x_p, x_s, shift1_p, shift1_s, scale1_p, scale1_s, g_mix[0].reshape(1, d),
                               w_in_t, main_w, b_gates, nh_ml, tp)

    rest_p, k_p, v_p = _in_projection(h_all, w_in_t, 0, ntp, main_w, att_w, BF16, "in_projection_prompt")
    rest_s, k_s, v_s = _in_projection(h_all, w_in_t, ntp, nts, main_w, att_w, F32, "in_projection_decode")

    slopes = 2.0 ** (-8.0 * jnp.arange(1, n_heads + 1, dtype=F32) / n_heads)
    att_p, att_s = _attention(rest_p, k_p, v_p, slopes, bp, tp, rest_s, k_s, v_s, cache_k, cache_v, page_table,
                              bd, td, n_heads, hd)

    def gate_layouts(g, b, t):
        g = g[:, :2 * nh_ml].reshape(b, t, 2, nh_ml)
        return g.transpose(0, 3, 1, 2), g.transpose(0, 3, 2, 1)

    gcol_p, grow_p = gate_layouts(gates[:ntp], bp, tp)
    gcol_s, grow_s = gate_layouts(gates[ntp:], bd, td)
    gain = ml_gain[0].reshape(nh_ml, 1, d_ml)
    zeros = lambda *s: jnp.zeros(s, F32)
    hm_p, c_p, n_p, m_p = _mlstm(rest_p, gcol_p, grow_p, zeros(bp, nh_ml, d_ml, d_ml), zeros(bp, nh_ml, 1, d_ml),
                                 zeros(bp, nh_ml, 1, 1), gain, bp, tp, nh_ml, d_ml, att_w, 1, BF16)
    hm_s, c_s, n_s, m_s = _mlstm(rest_s, gcol_s, grow_s, state_C[0], state_n[0].reshape(bd, nh_ml, 1, d_ml),
                                 state_m[0].reshape(bd, nh_ml, 1, 1), gain, bd, td, nh_ml, d_ml, att_w, nh_ml, F32)

    w_route = jnp.pad(jnp.concatenate([w_exp[0], w_grp[0]], axis=1), ((0, 0), (0, LANES - n_experts - n_groups)))
    b_route = jnp.pad(jnp.concatenate([b_exp[0], b_grp[0]]), (0, LANES - n_experts - n_groups)).reshape(1, LANES)
    x1, h2g, rinfo, counts = _outproj_stage(att_p, att_s, hm_p, hm_s, x_p, x_s, gate1_p, gate1_s, shift2_p,
                                            shift2_s, scale2_p, scale2_s, w_out[0].astype(BF16),
                                            g_ffn[0].reshape(1, d), w_route, b_route, n_groups, n_experts, tp)

    moe_rows, dest = _moe_stage(h2g, rinfo, counts, w_gate[0].astype(BF16), w_up[0].astype(BF16),
                                w_down[0].astype(BF16), n_groups)
    y_p, y_s = _final_stage(x1, moe_rows, dest, gate2_p, gate2_s, g_final.reshape(1, d), ntp, tp)

    kv_shape_p = (1, bp, tp, n_heads, hd)
    kv_shape_s = (1, bd, td, n_heads, hd)
    return (y_p.reshape(bp, tp, d), y_s.reshape(bd, td, d),
            k_p.reshape(kv_shape_p), v_p.reshape(kv_shape_p), k_s.reshape(kv_shape_s), v_s.reshape(kv_shape_s),
            c_p[None], n_p.reshape(1, bp, nh_ml, d_ml), m_p.reshape(1, bp, nh_ml),
            c_s[None], n_s.reshape(1, bd, nh_ml, d_ml), m_s.reshape(1, bd, nh_ml))
```

```python
import functools
import math

import jax
import jax.numpy as jnp
from jax import lax
from jax.experimental import pallas as pl
from jax.experimental.pallas import tpu as pltpu

F32 = jnp.float32
BF16 = jnp.bfloat16

MOBA_BLOCK = 256
MOBA_TOPK = 3
EXPERT_TOPK = 2
N_MOD = 6
EPS = 1e-6
NEG = -1e30

LANES = 128
V7X_VMEM_LIMIT = 56 * 1024 * 1024
TOKEN_TILE = 256


def _cparams(sem):
    return pltpu.CompilerParams(dimension_semantics=sem, vmem_limit_bytes=V7X_VMEM_LIMIT)


def _pick_tile(n, candidates):
    for c in candidates:
        if n % c == 0:
            return c
    raise ValueError(f"no tile in {candidates} divides {n}")


def _dot(a, b):
    return jnp.dot(a, b, preferred_element_type=F32)


def _dot_nt(a, b):
    return lax.dot_general(a, b, (((1,), (1,)), ((), ())), preferred_element_type=F32)


def _dot_tn(a, b):
    return lax.dot_general(a, b, (((0,), (0,)), ((), ())), preferred_element_type=F32)


def _split(x):
    hi = x.astype(BF16)
    lo = (x - hi.astype(F32)).astype(BF16)
    return hi, lo


def _dot3(a, b, dot=_dot):
    a_hi, a_lo = _split(a)
    b_hi, b_lo = _split(b)
    return dot(a_hi, b_hi) + dot(a_lo, b_hi) + dot(a_hi, b_lo)


def _sigmoid(x):
    return 1.0 / (1.0 + jnp.exp(-x))


def _log_sigmoid(x):
    return jnp.minimum(x, 0.0) - jnp.log(1.0 + jnp.exp(-jnp.abs(x)))


def _rms(x):
    return x * lax.rsqrt(jnp.mean(x * x, axis=-1, keepdims=True) + EPS)


def _mod_kernel(c_ref, w_ref, b_ref, o_ref):
    o_ref[...] = _dot3(c_ref[...], w_ref[...]) + b_ref[...]


def _modulation(c_all, w_mod, b_mod):
    rows, d = c_all.shape
    n = w_mod.shape[1]
    tn = _pick_tile(n, (1024, 512, 256, 128))
    return pl.pallas_call(
        _mod_kernel,
        grid=(n // tn,),
        in_specs=[
            pl.BlockSpec((rows, d), lambda j: (0, 0)),
            pl.BlockSpec((d, tn), lambda j: (0, j)),
            pl.BlockSpec((1, tn), lambda j: (0, j)),
        ],
        out_specs=pl.BlockSpec((rows, tn), lambda j: (0, j)),
        out_shape=jax.ShapeDtypeStruct((rows, n), F32),
        compiler_params=_cparams(("arbitrary",)),
        name="modulation",
    )(c_all, w_mod, b_mod.reshape(1, n))


def _p_rows(n_p):
    return lambda i: (jnp.minimum(i, n_p - 1), 0)


def _p_batch(n_p, tiles_per_batch):
    return lambda i: (jnp.minimum(i, n_p - 1) // tiles_per_batch, 0, 0)


def _s_rows(n_p):
    return lambda i: (jnp.maximum(i - n_p, 0), 0)


def _dual(n_p, body, p_refs, s_refs):
    i = pl.program_id(0)

    @pl.when(i < n_p)
    def _():
        body(*p_refs)

    @pl.when(i >= n_p)
    def _():
        body(*s_refs)


def _norm_kernel(n_p, n_heads, xp, xs, shp, shs, scp, scs, g_ref, wg_ref, bg_ref, h_ref, gates_ref):
    def body(x_ref, sh_ref, sc_ref):
        h = _rms(x_ref[...]) * g_ref[...] * (1.0 + sc_ref[...]) + sh_ref[...]
        h_ref[...] = h.astype(BF16)
        wg = wg_ref[...]
        wg = jnp.concatenate([wg, jnp.zeros((LANES - wg.shape[0], wg.shape[1]), F32)], axis=0)
        g = _dot3(h, wg, _dot_nt) + bg_ref[...]
        lane = lax.broadcasted_iota(jnp.int32, g.shape, 1)
        gates_ref[...] = jnp.where(lane < n_heads, g, _log_sigmoid(g))

    _dual(n_p, body, (xp, shp, scp), (xs, shs, scs))


def _norm_stage(x_p, x_s, sh_p, sh_s, sc_p, sc_s, g_mix, w_in_t, gate_row0, b_gates, n_heads, t_p):
    tm = TOKEN_TILE
    d = x_p.shape[1]
    n_p, n_s = x_p.shape[0] // tm, x_s.shape[0] // tm
    n_tok = x_p.shape[0] + x_s.shape[0]
    n_gate = 2 * n_heads
    assert gate_row0 % n_gate == 0 and w_in_t.shape[0] - gate_row0 == n_gate and n_gate % 8 == 0
    row = pl.BlockSpec((tm, d), _p_rows(n_p))
    srow = pl.BlockSpec((tm, d), _s_rows(n_p))
    pbat = pl.BlockSpec((None, 1, d), _p_batch(n_p, t_p // tm))
    const = lambda shape: pl.BlockSpec(shape, lambda i: (0,) * len(shape))
    w_gate_cols = pl.BlockSpec((n_gate, d), lambda i: (gate_row0 // n_gate, 0))
    return pl.pallas_call(
        functools.partial(_norm_kernel, n_p, n_heads),
        grid=(n_p + n_s,),
        in_specs=[row, srow, pbat, srow, pbat, srow, const((1, d)), w_gate_cols, const((1, LANES))],
        out_specs=[pl.BlockSpec((tm, d), lambda i: (i, 0)), pl.BlockSpec((tm, LANES), lambda i: (i, 0))],
        out_shape=[jax.ShapeDtypeStruct((n_tok, d), BF16), jax.ShapeDtypeStruct((n_tok, LANES), F32)],
        compiler_params=_cparams(("arbitrary",)),
        name="norm_gates",
    )(x_p, x_s, sh_p, sh_s, sc_p, sc_s, g_mix, w_in_t, b_gates)


INPROJ_COL_TILE = 1024


def _inproj_kernel(h_ref, wt_ref, rest_ref, k_ref, v_ref, wbf_ref, *, k_tiles, v_tiles):
    j = pl.program_id(0)

    @pl.when(pl.program_id(1) == 0)
    def _():
        wbf_ref[...] = wt_ref[...].T.astype(BF16)

    acc = _dot(h_ref[...], wbf_ref[...])
    is_k = (j >= k_tiles[0]) & (j < k_tiles[1])
    is_v = (j >= v_tiles[0]) & (j < v_tiles[1])

    @pl.when(is_k)
    def _():
        k_ref[...] = acc

    @pl.when(is_v)
    def _():
        v_ref[...] = acc

    @pl.when(jnp.logical_not(is_k | is_v))
    def _():
        rest_ref[...] = acc.astype(rest_ref.dtype)


def _in_projection(h_all, w_in_t, row0, n_rows, width, att_w, rest_dtype, name):
    d = h_all.shape[1]
    tn = math.gcd(att_w, INPROJ_COL_TILE)
    tm = _pick_tile(n_rows, (1024, 512, 256))
    n_i = n_rows // tm
    rb = row0 // tm
    assert row0 % tm == 0
    last = n_i - 1
    per = att_w // tn
    k_tiles, v_tiles = (per, 2 * per), (2 * per, 3 * per)

    def rest_index(j, i):
        skipping = (j >= k_tiles[0]) & (j < v_tiles[1])
        return (jnp.where(skipping, last, i),
                jnp.where(j < k_tiles[0], j, jnp.where(skipping, k_tiles[0] - 1, j - 2 * per)))

    def own_index(tiles):
        lo, hi = tiles
        return lambda j, i: (jnp.where(j < lo, 0, jnp.where(j < hi, i, last)),
                             jnp.where(j < lo, 0, jnp.where(j < hi, j - lo, hi - lo - 1)))

    return pl.pallas_call(
        functools.partial(_inproj_kernel, k_tiles=k_tiles, v_tiles=v_tiles),
        grid=(width // tn, n_i),
        in_specs=[pl.BlockSpec((tm, d), lambda j, i: (rb + i, 0)), pl.BlockSpec((tn, d), lambda j, i: (j, 0))],
        out_specs=[pl.BlockSpec((tm, tn), rest_index), pl.BlockSpec((tm, tn), own_index(k_tiles)),
                   pl.BlockSpec((tm, tn), own_index(v_tiles))],
        out_shape=[jax.ShapeDtypeStruct((n_rows, width - 2 * att_w), rest_dtype),
                   jax.ShapeDtypeStruct((n_rows, att_w), F32), jax.ShapeDtypeStruct((n_rows, att_w), F32)],
        scratch_shapes=[pltpu.VMEM((d, tn), BF16)],
        compiler_params=_cparams(("arbitrary", "arbitrary")),
        name=name,
    )(h_all, w_in_t)


def _prompt_attention(slope, tiles, first, q_ref, k_ref, v_ref, o_ref, kbf_ref, vaug_ref, sel_ref, alibi_ref):
    t, hd = q_ref.shape
    blk = MOBA_BLOCK
    nb = t // blk
    scale = hd ** -0.5

    if first:
        k = k_ref[...]
        kbf_ref[...] = (k * scale).astype(BF16)
        vaug_ref[:, :hd] = v_ref[...].astype(BF16)
        vaug_ref[:, hd:] = jnp.ones((t, hd), BF16)

        kmean = jnp.mean(k.reshape(nb, blk, hd), axis=1)
        gate = _dot3(kmean, q_ref[...].astype(F32), _dot_nt)
        kb = lax.broadcasted_iota(jnp.int32, (nb, t), 0)
        qb = lax.broadcasted_iota(jnp.int32, (nb, t), 1) // blk
        valid = kb < qb
        g = jnp.where(valid, gate, -jnp.inf)
        rank = jnp.zeros((nb, t), jnp.int32)
        for n2 in range(nb):
            gn = g[n2:n2 + 1, :]
            rank += ((gn > g) | ((gn == g) & (n2 < kb))).astype(jnp.int32)
        allowed = (valid & (rank < MOBA_TOPK)).astype(F32)
        sel_ref[...] = jnp.concatenate([allowed, jnp.zeros((LANES - nb, t), F32)], axis=0).T.astype(BF16)
        alibi_ref[...] = slope * (lax.broadcasted_iota(jnp.int32, (blk, t), 0)
                                  - lax.broadcasted_iota(jnp.int32, (blk, t), 1)).astype(F32)

    causal = (lax.broadcasted_iota(jnp.int32, (blk, blk), 0) >= lax.broadcasted_iota(jnp.int32, (blk, blk), 1))
    for i in tiles:
        w = (i + 1) * blk
        rows = pl.ds(i * blk, blk)
        s = _dot_nt(q_ref[rows, :].astype(BF16), kbf_ref[0:w, :]) - alibi_ref[:, 0:w]
        ok = causal
        if i > 0:
            expand = (lax.broadcasted_iota(jnp.int32, (LANES, i * blk), 0)
                      == lax.broadcasted_iota(jnp.int32, (LANES, i * blk), 1) // blk).astype(BF16)
            past_ok = _dot(sel_ref[rows, :], expand) > 0.5
            ok = jnp.concatenate([past_ok, causal], axis=1)
        s = jnp.where(ok, s, NEG)
        p = jnp.exp(s - jnp.max(s, axis=1, keepdims=True)).astype(BF16)
        out = _dot(p, vaug_ref[0:w, :])
        o_ref[rows, :] = (out[:, :hd] / out[:, hd:]).astype(o_ref.dtype)


DECODE_BLOCKS_PER_STEP = 8
PACKED_ROWS = 16


def _decode_attention(step, n_steps, q_ref, k_refs, v_refs, kn_ref, vn_ref, o_ref,
                      qflat_ref, mb_ref, ksum_ref, m_ref, l_ref, acc_ref, *, n_heads, hd, past_len, bps, slopes):
    tq = q_ref.shape[0]
    blk = MOBA_BLOCK
    scale = hd ** -0.5
    n_past = acc_ref.shape[0]
    rowpad = jnp.zeros((PACKED_ROWS - tq, hd), F32)
    page = blk // 2
    n_pairs = n_heads // 2
    pr = 2 * tq
    pk = 2 * blk
    pair_heads = [(g, g + n_pairs) for g in range(n_pairs)]

    def slope_col(g):
        second = lax.broadcasted_iota(jnp.int32, (pr, 1), 0) >= tq
        return jnp.where(second, slopes[pair_heads[g][1]], slopes[pair_heads[g][0]])

    @pl.when(step == 0)
    def _():
        qflat_ref[...] = jnp.concatenate(
            [q_ref[:, h * hd:(h + 1) * hd] for pair in pair_heads for h in pair], axis=0) * scale
        ksum_ref[...] = jnp.zeros_like(ksum_ref)
        r = lax.broadcasted_iota(jnp.int32, (pr, pk), 0)
        c = lax.broadcasted_iota(jnp.int32, (pr, pk), 1)
        in_block_dist = (r % tq - c // 2).astype(F32)
        for g in range(n_pairs):
            mb_ref[g] = jnp.where(c % 2 == r // tq, -slope_col(g) * in_block_dist, NEG)

    for j in range(bps):
        n = step * bps + j
        k0, k1 = k_refs[2 * j], k_refs[2 * j + 1]
        ksum_ref[pl.ds(pl.multiple_of(n * n_heads, n_heads), n_heads), :] = (
            jnp.sum(k0[...].reshape(page, n_heads, hd), axis=0) + jnp.sum(k1[...].reshape(page, n_heads, hd), axis=0))

    for g in range(n_pairs):
        pair_rows = pl.ds(g, 2 * page, stride=n_pairs)
        rs = slice(g * pr, (g + 1) * pr)
        kp = jnp.concatenate([kr[pair_rows, :] for kr in k_refs], axis=0).astype(BF16)
        s_all = _dot_nt(qflat_ref[rs, :].astype(BF16), kp)
        for j in range(bps):
            n = step * bps + j
            v0, v1 = v_refs[2 * j], v_refs[2 * j + 1]
            vp = jnp.concatenate([v0[pair_rows, :], v1[pair_rows, :]], axis=0).astype(BF16)
            s = s_all[:, j * pk:(j + 1) * pk] + mb_ref[g]
            m = jnp.max(s, axis=1, keepdims=True)
            p = jnp.exp(s - m)
            l = jnp.sum(p, axis=1, keepdims=True)
            acc_ref[n, rs, :] = _dot(p.astype(BF16), vp)
            block_dist = (past_len - n * blk).astype(F32)
            m_ref[n, rs, :] = jnp.broadcast_to(m - slope_col(g) * block_dist, (pr, hd))
            l_ref[n, rs, :] = jnp.broadcast_to(l, (pr, hd))

    @pl.when(step == n_steps - 1)
    def _():
        lane = lax.broadcasted_iota(jnp.int32, (tq, LANES), 1)
        valid = lane < n_past
        orow = lax.broadcasted_iota(jnp.int32, (PACKED_ROWS, PACKED_ROWS), 0)
        ocol = lax.broadcasted_iota(jnp.int32, (PACKED_ROWS, PACKED_ROWS), 1)
        d_own = (orow - ocol).astype(F32)
        for h in range(n_heads):
            hs = slice(h * hd, (h + 1) * hd)
            row0 = (h % n_pairs) * pr + (h // n_pairs) * tq
            rs = slice(row0, row0 + tq)
            qf = jnp.concatenate([qflat_ref[rs, :], rowpad], axis=0)
            kmean = ksum_ref[pl.ds(h, LANES, stride=n_heads), :] * (1.0 / blk)
            gate = _dot3(qf[:tq], kmean, _dot_nt)
            g = jnp.where(valid, gate, -jnp.inf)
            rank = jnp.zeros((tq, LANES), jnp.int32)
            for n2 in range(n_past):
                gn = g[:, n2:n2 + 1]
                rank += ((gn > g) | ((gn == g) & (n2 < lane))).astype(jnp.int32)
            sel = (valid & (rank < MOBA_TOPK)).astype(F32)

            kn = jnp.concatenate([kn_ref[:, hs], rowpad], axis=0).astype(BF16)
            vn = jnp.concatenate([vn_ref[:, hs], rowpad], axis=0).astype(BF16)
            s_own = jnp.where(d_own >= 0.0, _dot_nt(qf.astype(BF16), kn) - slopes[h] * d_own, NEG)
            m_own = jnp.max(s_own, axis=1, keepdims=True)
            p_own = jnp.exp(s_own - m_own)
            l_own = jnp.sum(p_own, axis=1, keepdims=True)[:tq]
            acc_own = _dot(p_own.astype(BF16), vn)[:tq]
            m_own = m_own[:tq]

            picked = [sel[:, n2:n2 + 1] > 0.0 for n2 in range(n_past)]
            m_fin = jnp.broadcast_to(m_own, (tq, hd))
            for n2 in range(n_past):
                m_fin = jnp.maximum(m_fin, jnp.where(picked[n2], m_ref[n2, rs, :], NEG))
            w_own = jnp.exp(m_own - m_fin)
            l_fin = w_own * l_own
            out = w_own * acc_own
            for n2 in range(n_past):
                w = jnp.where(picked[n2], jnp.exp(m_ref[n2, rs, :] - m_fin), 0.0)
                l_fin += w * l_ref[n2, rs, :]
                out += w * acc_ref[n2, rs, :]
            o_ref[:, hs] = out / l_fin


def _attention_kernel(pt_ref, slopes_ref, qn_ref, *refs, n_pages_step, decode_cfg, n_heads, tiles_per_step,
                      steps_per_head, n_work):
    del pt_ref
    k_refs, v_refs = refs[:n_pages_step], refs[n_pages_step:2 * n_pages_step]
    (kn_ref, vn_ref, qp_ref, kp_ref, vp_ref, od_ref, op_ref,
     qflat_ref, mb_ref, ksum_ref, m_ref, l_ref, acc_ref, kbf_ref, vaug_ref, sel_ref, alibi_ref) = refs[2 * n_pages_step:]
    step, n_steps = pl.program_id(1), pl.num_programs(1)
    _decode_attention(step, n_steps, qn_ref, k_refs, v_refs, kn_ref, vn_ref, od_ref,
                      qflat_ref, mb_ref, ksum_ref, m_ref, l_ref, acc_ref, **decode_cfg)

    g = pl.program_id(0) * n_steps + step
    phase = g % steps_per_head
    slope = slopes_ref[(g // steps_per_head) % n_heads]
    for ph in range(steps_per_head):
        @pl.when((g < n_work) & (phase == ph))
        def _(ph=ph):
            tiles = range(ph * tiles_per_step, (ph + 1) * tiles_per_step)
            _prompt_attention(slope, tiles, ph == 0, qp_ref, kp_ref, vp_ref, op_ref,
                              kbf_ref, vaug_ref, sel_ref, alibi_ref)


def _attention(q_p, k_p, v_p, slopes, batch_p, t, q_new, k_new, v_new, cache_k, cache_v, page_table,
               batch_d, tq, n_heads, hd):
    _, n_phys, page, _, _ = cache_k.shape
    width = n_heads * hd
    assert 2 * page == MOBA_BLOCK, "a MoBA block must span exactly two cache pages"
    assert tq <= PACKED_ROWS
    n_pages = page_table.shape[1]
    n_past = n_pages // 2
    bps = math.gcd(n_past, DECODE_BLOCKS_PER_STEP)
    n_steps = n_past // bps
    assert n_past <= LANES
    slopes_py = tuple(2.0 ** (-8.0 * (h + 1) / n_heads) for h in range(n_heads))

    nb = t // MOBA_BLOCK
    heads_p = batch_p * n_heads
    total_steps = batch_d * n_steps
    tiles_per_step = next(c for c in range(1, nb + 1) if nb % c == 0 and heads_p * (nb // c) <= total_steps)
    steps_per_head = nb // tiles_per_step
    n_work = heads_p * steps_per_head

    ck = cache_k.reshape(n_phys, page * n_heads, hd)
    cv = cache_v.reshape(n_phys, page * n_heads, hd)

    def page_spec(p):
        return pl.BlockSpec((None, page * n_heads, hd),
                            lambda b, s, pt, sl: (pt[b * n_pages + 2 * bps * s + p], 0, 0))

    def head_index(b, s, pt, sl):
        bh = jnp.minimum((b * n_steps + s) // steps_per_head, heads_p - 1)
        return (bh // n_heads, bh % n_heads)

    new_spec = pl.BlockSpec((tq, width), lambda b, s, pt, sl: (b, 0))
    head_spec = pl.BlockSpec((t, hd), head_index)
    pages = [page_spec(p) for p in range(2 * bps)]
    rows = n_heads * tq
    per_block = (n_past, rows, hd)
    grid_spec = pltpu.PrefetchScalarGridSpec(
        num_scalar_prefetch=2,
        grid=(batch_d, n_steps),
        in_specs=[new_spec] + pages + pages + [new_spec, new_spec, head_spec, head_spec, head_spec],
        out_specs=[new_spec, head_spec],
        scratch_shapes=[pltpu.VMEM((rows, hd), F32), pltpu.VMEM((n_heads // 2, 2 * tq, 2 * MOBA_BLOCK), F32),
                        pltpu.VMEM((LANES * n_heads, hd), F32), pltpu.VMEM(per_block, F32),
                        pltpu.VMEM(per_block, F32), pltpu.VMEM(per_block, F32),
                        pltpu.VMEM((t, hd), BF16), pltpu.VMEM((t, 2 * hd), BF16), pltpu.VMEM((t, LANES), BF16),
                        pltpu.VMEM((MOBA_BLOCK, t), F32)],
    )
    decode_cfg = dict(n_heads=n_heads, hd=hd, past_len=n_pages * page, bps=bps, slopes=slopes_py)
    att_d, att_p = pl.pallas_call(
        functools.partial(_attention_kernel, n_pages_step=2 * bps, decode_cfg=decode_cfg, n_heads=n_heads,
                          tiles_per_step=tiles_per_step, steps_per_head=steps_per_head, n_work=n_work),
        grid_spec=grid_spec,
        out_shape=[jax.ShapeDtypeStruct((batch_d * tq, width), F32),
                   jax.ShapeDtypeStruct((batch_p * t, width), BF16)],
        compiler_params=_cparams(("arbitrary", "arbitrary")),
        name="attention",
    )(page_table.reshape(-1), slopes, q_new, *([ck] * (2 * bps)), *([cv] * (2 * bps)), k_new, v_new, q_p, k_p, v_p)
    return att_p, att_d


def _mlstm_kernel(q_ref, k_ref, v_ref, og_ref, gcol_ref, grow_ref, c0_ref, n0_ref, m0_ref, gain_ref,
                  hm_ref, c_ref, n_ref, m_ref, *, chunk, hps):
    t = q_ref.shape[0]
    d = q_ref.shape[1] // hps
    lc = chunk
    c_ref[...] = c0_ref[...]
    n_ref[...] = n0_ref[...]
    m_ref[...] = m0_ref[...]
    r = lax.broadcasted_iota(jnp.int32, (lc, lc), 0)
    c = lax.broadcasted_iota(jnp.int32, (lc, lc), 1)
    lower = r >= c

    def one_chunk(hh, rows):
        cols = slice(hh * d, (hh + 1) * d)
        q = q_ref[rows, cols].astype(F32)
        k = k_ref[rows, cols].astype(F32) * (d ** -0.5)
        v = v_ref[rows, cols].astype(F32)
        gc = gcol_ref[hh, rows, :]
        ig_c, lf_c = gc[:, 0:1], gc[:, 1:2]
        gr = grow_ref[hh, :, rows]
        ig_r, lf_r = gr[0:1, :], gr[1:2, :]
        cmat, nvec, m0 = c_ref[hh], n_ref[hh], m_ref[hh]

        b_c = jnp.sum(jnp.where(lower, lf_r, 0.0), axis=1, keepdims=True)
        b_r = jnp.sum(jnp.where(r <= c, lf_c, 0.0), axis=0, keepdims=True)
        dmat = jnp.where(lower, b_c - b_r + ig_r, NEG)
        inter = b_c + m0
        mt = jnp.maximum(inter, jnp.max(dmat, axis=1, keepdims=True))
        w = jnp.exp(dmat - mt)
        a = jnp.exp(inter - mt)
        qb, kb, vb = q.astype(BF16), k.astype(BF16), v.astype(BF16)
        sc = _dot_nt(qb, kb) * w
        num = a * _dot_nt(qb, cmat.astype(BF16)) + _dot(sc.astype(BF16), vb)
        den = a * jnp.sum(q * nvec, axis=1, keepdims=True) + jnp.sum(sc, axis=1, keepdims=True)
        h = num / jnp.maximum(jnp.abs(den), jnp.exp(-mt))
        hn = _rms(h) * gain_ref[hh]
        hm_ref[rows, cols] = (hn * _sigmoid(og_ref[rows, cols].astype(F32))).astype(hm_ref.dtype)

        b_last = b_c[lc - 1:lc, :]
        g = b_last - b_c + ig_c
        m_new = jnp.maximum(b_last + m0, jnp.max(g, axis=0, keepdims=True))
        ws = jnp.exp(g - m_new)
        a_last = jnp.exp(b_last + m0 - m_new)
        c_ref[hh] = a_last * cmat + _dot_tn((v * ws).astype(BF16), kb)
        n_ref[hh] = a_last * nvec + jnp.sum(ws * k, axis=0, keepdims=True)
        m_ref[hh] = m_new

    for hh in range(hps):
        if t == lc:
            one_chunk(hh, pl.ds(0, lc))
        else:
            def body(ci, carry, hh=hh):
                one_chunk(hh, pl.ds(pl.multiple_of(ci * lc, lc), lc))
                return carry

            lax.fori_loop(0, t // lc, body, 0)


def _mlstm(proj, gcol, grow, c0, n0, m0, gain, batch, t, n_heads, d, col0, hps, out_dtype):
    lc = math.gcd(t, MOBA_BLOCK)
    wb = hps * d
    assert col0 % wb == 0 and n_heads % hps == 0

    def col(off):
        return pl.BlockSpec((t, wb), lambda b, h: (b, (col0 + off * n_heads * d) // wb + h))

    per_head = lambda shape: pl.BlockSpec((None, hps) + shape, lambda b, h: (b, h, 0, 0))
    return pl.pallas_call(
        functools.partial(_mlstm_kernel, chunk=lc, hps=hps),
        grid=(batch, n_heads // hps),
        in_specs=[col(0), col(1), col(2), col(3), per_head((t, 2)), per_head((2, t)),
                  per_head((d, d)), per_head((1, d)), per_head((1, 1)),
                  pl.BlockSpec((hps, 1, d), lambda b, h: (h, 0, 0))],
        out_specs=[pl.BlockSpec((t, wb), lambda b, h: (b, h)), per_head((d, d)), per_head((1, d)),
                   per_head((1, 1))],
        out_shape=[jax.ShapeDtypeStruct((batch * t, n_heads * d), out_dtype),
                   jax.ShapeDtypeStruct((batch, n_heads, d, d), F32),
                   jax.ShapeDtypeStruct((batch, n_heads, 1, d), F32),
                   jax.ShapeDtypeStruct((batch, n_heads, 1, 1), F32)],
        compiler_params=_cparams(("arbitrary", "arbitrary")),
        name=f"mlstm_t{t}",
    )(proj, proj, proj, proj, gcol, grow, c0, n0, m0, gain)


def _route(logits, n_groups, n_experts):
    epg = n_experts // n_groups
    lane = lax.broadcasted_iota(jnp.int32, logits.shape, 1).astype(F32)
    big = float(LANES)

    def rmax(x):
        return jnp.max(x, axis=1, keepdims=True)

    def first_lane(mask):
        return jnp.min(jnp.where(mask, lane, big), axis=1, keepdims=True)

    is_g = (lane >= n_experts) & (lane < n_experts + n_groups)
    gmax = rmax(jnp.where(is_g, logits, -jnp.inf))
    g_w = 1.0 / jnp.sum(jnp.where(is_g, jnp.exp(logits - gmax), 0.0), axis=1, keepdims=True)
    group = first_lane(is_g & (logits == gmax)) - n_experts
    lo = group * epg
    in_grp = (lane >= lo) & (lane < lo + epg)
    emax = rmax(jnp.where(in_grp, logits, -jnp.inf))
    pe = jnp.where(in_grp, jnp.exp(logits - emax), 0.0)
    prob = pe / jnp.sum(pe, axis=1, keepdims=True)
    p1 = rmax(jnp.where(in_grp, prob, -1.0))
    i1 = first_lane(in_grp & (prob == p1))
    rest = in_grp & (lane != i1)
    p2 = rmax(jnp.where(rest, prob, -1.0))
    i2 = first_lane(rest & (prob == p2))
    den = p1 + p2
    gate = g_w * jnp.where(lane == i1, p1 / den, jnp.where(lane == i2, p2 / den, 0.0))
    return gate, i1, i2


def _outproj_kernel(n_p, att_w, n_groups, n_experts,
                    attp, atts, hmp, hms, xp, xs, g1p, g1s, sh2p, sh2s, sc2p, sc2s,
                    wo_ref, gffn_ref, wr_ref, br_ref, x1_ref, h2g_ref, rinfo_ref, counts_ref, carry_ref):
    d = x1_ref.shape[1]

    @pl.when(pl.program_id(0) == 0)
    def _():
        carry_ref[...] = jnp.zeros_like(carry_ref)

    def body(att_ref, hm_ref, x_ref, g1_ref, sh2_ref, sc2_ref):
        mixed = (_dot(att_ref[...].astype(BF16), wo_ref[0:att_w, :])
                 + _dot(hm_ref[...].astype(BF16), wo_ref[att_w:, :]))
        x1 = x_ref[...] + g1_ref[...] * mixed
        x1_ref[...] = x1
        h2 = _rms(x1) * gffn_ref[...] * (1.0 + sc2_ref[...]) + sh2_ref[...]
        gate, e1, e2 = _route(_dot3(h2, wr_ref[...]) + br_ref[...], n_groups, n_experts)
        h2g_ref[:, :d] = h2
        h2g_ref[:, d:] = gate
        tm = h2.shape[0]
        lane = lax.broadcasted_iota(jnp.int32, (tm, LANES), 1).astype(F32)
        member = ((lane == e1) | (lane == e2)).astype(F32)
        r = lax.broadcasted_iota(jnp.int32, (tm, tm), 0)
        c = lax.broadcasted_iota(jnp.int32, (tm, tm), 1)
        before = _dot((r > c).astype(BF16), member.astype(BF16)) + carry_ref[...]
        rank1 = jnp.sum(jnp.where(lane == e1, before, 0.0), axis=1, keepdims=True)
        rank2 = jnp.sum(jnp.where(lane == e2, before, 0.0), axis=1, keepdims=True)
        rinfo_ref[...] = jnp.where(lane == 0.0, e1, jnp.where(lane == 1.0, rank1,
                                   jnp.where(lane == 2.0, e2, jnp.where(lane == 3.0, rank2, 0.0))))
        carry_ref[...] += jnp.sum(member, axis=0, keepdims=True)
        counts_ref[...] = carry_ref[...]

    _dual(n_p, body, (attp, hmp, xp, g1p, sh2p, sc2p), (atts, hms, xs, g1s, sh2s, sc2s))


def _outproj_stage(att_p, att_s, hm_p, hm_s, x_p, x_s, g1_p, g1_s, sh2_p, sh2_s, sc2_p, sc2_s,
                   w_out_bf, g_ffn, w_route, b_route, n_groups, n_experts, t_p):
    tm = TOKEN_TILE
    d = x_p.shape[1]
    att_w = att_p.shape[1]
    ml_w = hm_p.shape[1]
    n_p, n_s = x_p.shape[0] // tm, x_s.shape[0] // tm
    n_tok = x_p.shape[0] + x_s.shape[0]
    prow = lambda w: pl.BlockSpec((tm, w), _p_rows(n_p))
    srow = lambda w: pl.BlockSpec((tm, w), _s_rows(n_p))
    pbat = pl.BlockSpec((None, 1, d), _p_batch(n_p, t_p // tm))
    const = lambda shape: pl.BlockSpec(shape, lambda i: (0,) * len(shape))
    out_row = lambda w: pl.BlockSpec((tm, w), lambda i: (i, 0))
    return pl.pallas_call(
        functools.partial(_outproj_kernel, n_p, att_w, n_groups, n_experts),
        grid=(n_p + n_s,),
        in_specs=[prow(att_w), srow(att_w), prow(ml_w), srow(ml_w), prow(d), srow(d),
                  pbat, srow(d), pbat, srow(d), pbat, srow(d),
                  const((att_w + ml_w, d)), const((1, d)), const((d, LANES)), const((1, LANES))],
        out_specs=[out_row(d), out_row(d + LANES), out_row(LANES), const((1, LANES))],
        out_shape=[jax.ShapeDtypeStruct((n_tok, d), F32), jax.ShapeDtypeStruct((n_tok, d + LANES), F32),
                   jax.ShapeDtypeStruct((n_tok, LANES), F32), jax.ShapeDtypeStruct((1, LANES), F32)],
        scratch_shapes=[pltpu.VMEM((1, LANES), F32)],
        compiler_params=_cparams(("arbitrary",)),
        name="out_projection_router",
    )(att_p, att_s, hm_p, hm_s, x_p, x_s, g1_p, g1_s, sh2_p, sh2_s, sc2_p, sc2_s,
      w_out_bf, g_ffn, w_route, b_route)


MOE_TILE = 384
ROW_DMA_UNROLL = 8


class _RowGather:
    def __init__(self, idx_ref, src_hbm, buf_ref, sem):
        self.idx_ref, self.src_hbm, self.buf_ref, self.sem = idx_ref, src_hbm, buf_ref, sem
        self.tm = buf_ref.shape[1]

    def _copy(self, row_index, r, slot):
        return pltpu.make_async_copy(self.src_hbm.at[pl.ds(row_index, 1)],
                                     self.buf_ref.at[slot, pl.ds(r, 1)], self.sem.at[slot])

    def start(self, tile, slot):
        def body(r, carry):
            self._copy(self.idx_ref[tile * self.tm + r], r, slot).start()
            return carry

        lax.fori_loop(0, self.tm, body, 0, unroll=ROW_DMA_UNROLL)

    def start_inline(self, tile, slot, first_row, count):
        for u in range(count):
            r = first_row + u
            self._copy(self.idx_ref[tile * self.tm + r], r, slot).start()

    def wait(self, slot):
        def body(r, carry):
            self._copy(0, r, slot).wait()
            return carry

        lax.fori_loop(0, self.tm, body, 0, unroll=ROW_DMA_UNROLL)


def _moe_kernel(te_ref, nu_ref, src_ref, h2g_hbm, wg_ref, wu_ref, wd_ref, o_ref,
                hbuf_ref, wgb_ref, wub_ref, wdb_ref, sem):
    t = pl.program_id(0)
    n_tiles = pl.num_programs(0)
    tm, d = o_ref.shape
    n_used = nu_ref[0]
    used = t < n_used
    slot = t % 2
    rows = _RowGather(src_ref, h2g_hbm, hbuf_ref, sem)

    @pl.when(t == 0)
    def _():
        rows.start(0, 0)

    @pl.when(t <= n_used)
    def _():
        rows.wait(slot)

    @pl.when(jnp.logical_not(used))
    def _():
        o_ref[...] = jnp.zeros_like(o_ref)

    @pl.when(used)
    def _():
        rows.start_inline(jnp.minimum(t + 1, n_tiles - 1), 1 - slot, 0, tm)
        expert = te_ref[t]

        @pl.when((t == 0) | (expert != te_ref[jnp.maximum(t - 1, 0)]))
        def _():
            wgb_ref[...] = wg_ref[...].astype(BF16)
            wub_ref[...] = wu_ref[...].astype(BF16)
            wdb_ref[...] = wd_ref[...].astype(BF16)

        h2 = hbuf_ref[slot, :, :d].astype(BF16)
        gate = hbuf_ref[slot, :, d:]
        lane = lax.broadcasted_iota(jnp.int32, gate.shape, 1)
        ge = jnp.sum(jnp.where(lane == expert, gate, 0.0), axis=1, keepdims=True)
        hg = _dot(h2, wgb_ref[...])
        hu = _dot(h2, wub_ref[...])
        o_ref[...] = _dot((hg * _sigmoid(hg) * hu * ge).astype(BF16), wdb_ref[...])

    @pl.when(used & (t == n_tiles - 1))
    def _():
        rows.wait(1 - slot)


def _moe_stage(h2g, rinfo, counts, w_gate, w_up, w_down, token_tile):
    n_tok, dw = h2g.shape
    d = dw - LANES
    n_experts, _, ff = w_gate.shape
    tm = MOE_TILE
    n_tiles = (EXPERT_TOPK * n_tok + n_experts * (tm - 1)) // tm

    e1, r1, e2, r2 = (rinfo[:, i].astype(jnp.int32) for i in range(4))
    cnt = counts[0, :n_experts].astype(jnp.int32)
    tiles_e = (cnt + tm - 1) // tm
    tile_end = jnp.cumsum(tiles_e)
    row_start = (tile_end - tiles_e) * tm
    dest1, dest2 = row_start[e1] + r1, row_start[e2] + r2
    tile_expert = jnp.minimum(jnp.searchsorted(tile_end, jnp.arange(n_tiles, dtype=jnp.int32), side="right"),
                              n_experts - 1).astype(jnp.int32)
    n_used = tile_end[-1:].astype(jnp.int32)
    tok = jnp.arange(n_tok, dtype=jnp.int32)
    src = jnp.zeros((n_tiles * tm,), jnp.int32).at[jnp.concatenate([dest1, dest2])].set(jnp.concatenate([tok, tok]))
    dest = jnp.stack([dest1.reshape(-1, token_tile), dest2.reshape(-1, token_tile)], axis=1).reshape(-1)

    def w_spec(rows_, cols_):
        return pl.BlockSpec((None, rows_, cols_),
                            lambda t, te, nu, sr: (jnp.where(t < nu[0], te[t], n_experts - 1), 0, 0))

    grid_spec = pltpu.PrefetchScalarGridSpec(
        num_scalar_prefetch=3,
        grid=(n_tiles,),
        in_specs=[pl.BlockSpec(memory_space=pl.ANY), w_spec(d, ff), w_spec(d, ff), w_spec(ff, d)],
        out_specs=pl.BlockSpec((tm, d), lambda t, te, nu, sr: (t, 0)),
        scratch_shapes=[pltpu.VMEM((2, tm, dw), F32), pltpu.VMEM((d, ff), BF16), pltpu.VMEM((d, ff), BF16),
                        pltpu.VMEM((ff, d), BF16), pltpu.SemaphoreType.DMA((2,))],
    )
    ys = pl.pallas_call(
        _moe_kernel,
        grid_spec=grid_spec,
        out_shape=jax.ShapeDtypeStruct((n_tiles * tm, d), F32),
        compiler_params=_cparams(("arbitrary",)),
        name="moe_experts",
    )(tile_expert, n_used, src, h2g, w_gate, w_up, w_down)
    return ys, dest


def _final_kernel(n_p, dest_ref, x1_ref, moe_hbm, g2p, g2s, gfin_ref, yp_ref, ys_ref, mbuf_ref, sem):
    i = pl.program_id(0)
    slot = i % 2
    rows = _RowGather(dest_ref, moe_hbm, mbuf_ref, sem)

    @pl.when(i == 0)
    def _():
        rows.start(0, 0)

    rows.wait(slot)
    n = pl.num_programs(0)

    tm = x1_ref.shape[0]

    def body(g2_ref, y_ref):
        rows.start_inline(jnp.minimum(i + 1, n - 1), 1 - slot, 0, rows.tm)
        moe = mbuf_ref[slot, :tm, :] + mbuf_ref[slot, tm:, :]
        x2 = x1_ref[...] + g2_ref[...] * moe
        y_ref[...] = _rms(x2) * gfin_ref[...]

    _dual(n_p, body, (g2p, yp_ref), (g2s, ys_ref))

    @pl.when(i == n - 1)
    def _():
        rows.wait(1 - slot)


def _final_stage(x1, moe_rows, dest, g2_p, g2_s, g_final, n_tok_p, t_p):
    tm = TOKEN_TILE
    n_tok, d = x1.shape
    n_p = n_tok_p // tm
    n_s = (n_tok - n_tok_p) // tm
    grid_spec = pltpu.PrefetchScalarGridSpec(
        num_scalar_prefetch=1,
        grid=(n_p + n_s,),
        in_specs=[pl.BlockSpec((tm, d), lambda i, ds: (i, 0)), pl.BlockSpec(memory_space=pl.ANY),
                  pl.BlockSpec((None, 1, d), lambda i, ds: _p_batch(n_p, t_p // tm)(i)),
                  pl.BlockSpec((tm, d), lambda i, ds: _s_rows(n_p)(i)), pl.BlockSpec((1, d), lambda i, ds: (0, 0))],
        out_specs=[pl.BlockSpec((tm, d), lambda i, ds: _p_rows(n_p)(i)),
                   pl.BlockSpec((tm, d), lambda i, ds: _s_rows(n_p)(i))],
        scratch_shapes=[pltpu.VMEM((2, EXPERT_TOPK * tm, d), F32), pltpu.SemaphoreType.DMA((2,))],
    )
    return pl.pallas_call(
        functools.partial(_final_kernel, n_p),
        grid_spec=grid_spec,
        out_shape=[jax.ShapeDtypeStruct((n_tok_p, d), F32), jax.ShapeDtypeStruct((n_tok - n_tok_p, d), F32)],
        compiler_params=_cparams(("arbitrary",)),
        name="final_norm",
    )(dest, x1, moe_rows, g2_p, g2_s, g_final)


def kernel(x_prompt, x_sample, cache_k, cache_v, page_table, state_C, state_n, state_m, c_prompt, c_sample,
           w_in, b_ig, b_fg, ml_gain, w_out, g_mix, g_ffn, w_mod, b_mod, w_grp, b_grp, w_exp, b_exp,
           w_gate, w_up, w_down, g_final):
    bp, tp, d = x_prompt.shape
    bd, td, _ = x_sample.shape
    depth = w_in.shape[0]
    assert depth == 1, "one trunk layer"
    n_heads, hd = cache_k.shape[-2:]
    att_w = n_heads * hd
    nh_ml, d_ml = state_n.shape[-2:]
    ml_w = nh_ml * d_ml
    assert att_w == ml_w and att_w + ml_w == d
    n_groups, n_experts = w_grp.shape[-1], w_exp.shape[-1]
    assert n_experts + n_groups <= LANES and 2 * nh_ml <= LANES
    main_w = 3 * att_w + 4 * ml_w
    ntp, nts = bp * tp, bd * td

    c_rows = bp + bd
    c_pad = -c_rows % 16
    c_all = jnp.concatenate([c_prompt, c_sample, jnp.zeros((c_pad, d), F32)], axis=0)
    mod = _modulation(c_all, w_mod[0], b_mod[0])
    mod_p = mod[:bp].reshape(bp, N_MOD, 1, d)
    mod_s = jnp.repeat(mod[bp:c_rows].reshape(bd, N_MOD, d), td, axis=0)
    shift1_p, scale1_p, gate1_p, shift2_p, scale2_p, gate2_p = (mod_p[:, i] for i in range(N_MOD))
    shift1_s, scale1_s, gate1_s, shift2_s, scale2_s, gate2_s = (mod_s[:, i] for i in range(N_MOD))

    x_p = x_prompt.reshape(ntp, d)
    x_s = x_sample.reshape(nts, d)

    b_gates = jnp.pad(jnp.concatenate([b_ig[0], b_fg[0]]), (0, LANES - 2 * nh_ml)).reshape(1, LANES)
    w_in_t = w_in[0].T
    h_all, gates = _norm_stage(x_p, x_s, shift1_p, shift1_s, scale1_p, scale1_s, g_mix[0].reshape(1, d),
                               w_in_t, main_w, b_gates, nh_ml, tp)

    rest_p, k_p, v_p = _in_projection(h_all, w_in_t, 0, ntp, main_w, att_w, BF16, "in_projection_prompt")
    rest_s, k_s, v_s = _in_projection(h_all, w_in_t, ntp, nts, main_w, att_w, F32, "in_projection_decode")

    slopes = 2.0 ** (-8.0 * jnp.arange(1, n_heads + 1, dtype=F32) / n_heads)
    att_p, att_s = _attention(rest_p, k_p, v_p, slopes, bp, tp, rest_s, k_s, v_s, cache_k, cache_v, page_table,
                              bd, td, n_heads, hd)

    def gate_layouts(g, b, t):
        g = g[:, :2 * nh_ml].reshape(b, t, 2, nh_ml)
        return g.transpose(0, 3, 1, 2), g.transpose(0, 3, 2, 1)

    gcol_p, grow_p = gate_layouts(gates[:ntp], bp, tp)
    gcol_s, grow_s = gate_layouts(gates[ntp:], bd, td)
    gain = ml_gain[0].reshape(nh_ml, 1, d_ml)
    zeros = lambda *s: jnp.zeros(s, F32)
    hm_p, c_p, n_p, m_p = _mlstm(rest_p, gcol_p, grow_p, zeros(bp, nh_ml, d_ml, d_ml), zeros(bp, nh_ml, 1, d_ml),
                                 zeros(bp, nh_ml, 1, 1), gain, bp, tp, nh_ml, d_ml, att_w, 1, BF16)
    hm_s, c_s, n_s, m_s = _mlstm(rest_s, gcol_s, grow_s, state_C[0], state_n[0].reshape(bd, nh_ml, 1, d_ml),
                                 state_m[0].reshape(bd, nh_ml, 1, 1), gain, bd, td, nh_ml, d_ml, att_w, nh_ml, F32)

    w_route = jnp.pad(jnp.concatenate([w_exp[0], w_grp[0]], axis=1), ((0, 0), (0, LANES - n_experts - n_groups)))
    b_route = jnp.pad(jnp.concatenate([b_exp[0], b_grp[0]]), (0, LANES - n_experts - n_groups)).reshape(1, LANES)
    x1, h2g, rinfo, counts = _outproj_stage(att_p, att_s, hm_p, hm_s, x_p, x_s, gate1_p, gate1_s, shift2_p,
                                            shift2_s, scale2_p, scale2_s, w_out[0].astype(BF16),
                                            g_ffn[0].reshape(1, d), w_route, b_route, n_groups, n_experts, tp)

    moe_rows, dest = _moe_stage(h2g, rinfo, counts, w_gate[0], w_up[0], w_down[0], TOKEN_TILE)
    y_p, y_s = _final_stage(x1, moe_rows, dest, gate2_p, gate2_s, g_final.reshape(1, d), ntp, tp)

    kv_shape_p = (1, bp, tp, n_heads, hd)
    kv_shape_s = (1, bd, td, n_heads, hd)
    return (y_p.reshape(bp, tp, d), y_s.reshape(bd, td, d),
            k_p.reshape(kv_shape_p), v_p.reshape(kv_shape_p), k_s.reshape(kv_shape_s), v_s.reshape(kv_shape_s),
            c_p[None], n_p.reshape(1, bp, nh_ml, d_ml), m_p.reshape(1, bp, nh_ml),
            c_s[None], n_s.reshape(1, bd, nh_ml, d_ml), m_s.reshape(1, bd, nh_ml))
```

```python
import functools
import math

import jax
import jax.numpy as jnp
from jax import lax
from jax.experimental import pallas as pl
from jax.experimental.pallas import tpu as pltpu

F32 = jnp.float32
BF16 = jnp.bfloat16

MOBA_BLOCK = 256
MOBA_TOPK = 3
EXPERT_TOPK = 2
N_MOD = 6
EPS = 1e-6
NEG = -1e30

LANES = 128
V7X_VMEM_LIMIT = 56 * 1024 * 1024
TOKEN_TILE = 256


def _cparams(sem):
    return pltpu.CompilerParams(dimension_semantics=sem, vmem_limit_bytes=V7X_VMEM_LIMIT)


def _pick_tile(n, candidates):
    for c in candidates:
        if n % c == 0:
            return c
    raise ValueError(f"no tile in {candidates} divides {n}")


def _dot(a, b):
    return jnp.dot(a, b, preferred_element_type=F32)


def _dot_nt(a, b):
    return lax.dot_general(a, b, (((1,), (1,)), ((), ())), preferred_element_type=F32)


def _dot_tn(a, b):
    return lax.dot_general(a, b, (((0,), (0,)), ((), ())), preferred_element_type=F32)


def _split(x):
    hi = x.astype(BF16)
    lo = (x - hi.astype(F32)).astype(BF16)
    return hi, lo


def _dot3(a, b, dot=_dot):
    a_hi, a_lo = _split(a)
    b_hi, b_lo = _split(b)
    return dot(a_hi, b_hi) + dot(a_lo, b_hi) + dot(a_hi, b_lo)


def _sigmoid(x):
    return 1.0 / (1.0 + jnp.exp(-x))


def _log_sigmoid(x):
    return jnp.minimum(x, 0.0) - jnp.log(1.0 + jnp.exp(-jnp.abs(x)))


def _rms(x):
    return x * lax.rsqrt(jnp.mean(x * x, axis=-1, keepdims=True) + EPS)


def _mod_kernel(c_ref, w_ref, b_ref, o_ref):
    o_ref[...] = _dot3(c_ref[...], w_ref[...]) + b_ref[...]


def _modulation(c_all, w_mod, b_mod):
    rows, d = c_all.shape
    n = w_mod.shape[1]
    tn = _pick_tile(n, (1024, 512, 256, 128))
    return pl.pallas_call(
        _mod_kernel,
        grid=(n // tn,),
        in_specs=[
            pl.BlockSpec((rows, d), lambda j: (0, 0)),
            pl.BlockSpec((d, tn), lambda j: (0, j)),
            pl.BlockSpec((1, tn), lambda j: (0, j)),
        ],
        out_specs=pl.BlockSpec((rows, tn), lambda j: (0, j)),
        out_shape=jax.ShapeDtypeStruct((rows, n), F32),
        compiler_params=_cparams(("arbitrary",)),
        name="modulation",
    )(c_all, w_mod, b_mod.reshape(1, n))


def _p_rows(n_p):
    return lambda i: (jnp.minimum(i, n_p - 1), 0)


def _p_batch(n_p, tiles_per_batch):
    return lambda i: (jnp.minimum(i, n_p - 1) // tiles_per_batch, 0, 0)


def _s_rows(n_p):
    return lambda i: (jnp.maximum(i - n_p, 0), 0)


def _dual(n_p, body, p_refs, s_refs):
    i = pl.program_id(0)

    @pl.when(i < n_p)
    def _():
        body(*p_refs)

    @pl.when(i >= n_p)
    def _():
        body(*s_refs)


def _norm_kernel(n_p, n_heads, xp, xs, shp, shs, scp, scs, g_ref, wg_ref, bg_ref, h_ref, gates_ref):
    def body(x_ref, sh_ref, sc_ref):
        h = _rms(x_ref[...]) * g_ref[...] * (1.0 + sc_ref[...]) + sh_ref[...]
        h_ref[...] = h.astype(BF16)
        wg = wg_ref[...]
        wg = jnp.concatenate([wg, jnp.zeros((LANES - wg.shape[0], wg.shape[1]), F32)], axis=0)
        g = _dot3(h, wg, _dot_nt) + bg_ref[...]
        lane = lax.broadcasted_iota(jnp.int32, g.shape, 1)
        gates_ref[...] = jnp.where(lane < n_heads, g, _log_sigmoid(g))

    _dual(n_p, body, (xp, shp, scp), (xs, shs, scs))


def _norm_stage(x_p, x_s, sh_p, sh_s, sc_p, sc_s, g_mix, w_in_t, gate_row0, b_gates, n_heads, t_p):
    tm = TOKEN_TILE
    d = x_p.shape[1]
    n_p, n_s = x_p.shape[0] // tm, x_s.shape[0] // tm
    n_tok = x_p.shape[0] + x_s.shape[0]
    n_gate = 2 * n_heads
    assert gate_row0 % n_gate == 0 and w_in_t.shape[0] - gate_row0 == n_gate and n_gate % 8 == 0
    row = pl.BlockSpec((tm, d), _p_rows(n_p))
    srow = pl.BlockSpec((tm, d), _s_rows(n_p))
    pbat = pl.BlockSpec((None, 1, d), _p_batch(n_p, t_p // tm))
    const = lambda shape: pl.BlockSpec(shape, lambda i: (0,) * len(shape))
    w_gate_cols = pl.BlockSpec((n_gate, d), lambda i: (gate_row0 // n_gate, 0))
    return pl.pallas_call(
        functools.partial(_norm_kernel, n_p, n_heads),
        grid=(n_p + n_s,),
        in_specs=[row, srow, pbat, srow, pbat, srow, const((1, d)), w_gate_cols, const((1, LANES))],
        out_specs=[pl.BlockSpec((tm, d), lambda i: (i, 0)), pl.BlockSpec((tm, LANES), lambda i: (i, 0))],
        out_shape=[jax.ShapeDtypeStruct((n_tok, d), BF16), jax.ShapeDtypeStruct((n_tok, LANES), F32)],
        compiler_params=_cparams(("arbitrary",)),
        name="norm_gates",
    )(x_p, x_s, sh_p, sh_s, sc_p, sc_s, g_mix, w_in_t, b_gates)


INPROJ_COL_TILE = 1024


def _inproj_kernel(h_ref, wt_ref, rest_ref, k_ref, v_ref, wbf_ref, *, k_tiles, v_tiles):
    j = pl.program_id(0)

    @pl.when(pl.program_id(1) == 0)
    def _():
        wbf_ref[...] = wt_ref[...].T.astype(BF16)

    acc = _dot(h_ref[...], wbf_ref[...])
    is_k = (j >= k_tiles[0]) & (j < k_tiles[1])
    is_v = (j >= v_tiles[0]) & (j < v_tiles[1])

    @pl.when(is_k)
    def _():
        k_ref[...] = acc

    @pl.when(is_v)
    def _():
        v_ref[...] = acc

    @pl.when(jnp.logical_not(is_k | is_v))
    def _():
        rest_ref[...] = acc.astype(rest_ref.dtype)


def _in_projection(h_all, w_in_t, row0, n_rows, width, att_w, rest_dtype, name):
    d = h_all.shape[1]
    tn = math.gcd(att_w, INPROJ_COL_TILE)
    tm = _pick_tile(n_rows, (1024, 512, 256))
    n_i = n_rows // tm
    rb = row0 // tm
    assert row0 % tm == 0
    last = n_i - 1
    per = att_w // tn
    k_tiles, v_tiles = (per, 2 * per), (2 * per, 3 * per)

    def rest_index(j, i):
        skipping = (j >= k_tiles[0]) & (j < v_tiles[1])
        return (jnp.where(skipping, last, i),
                jnp.where(j < k_tiles[0], j, jnp.where(skipping, k_tiles[0] - 1, j - 2 * per)))

    def own_index(tiles):
        lo, hi = tiles
        return lambda j, i: (jnp.where(j < lo, 0, jnp.where(j < hi, i, last)),
                             jnp.where(j < lo, 0, jnp.where(j < hi, j - lo, hi - lo - 1)))

    return pl.pallas_call(
        functools.partial(_inproj_kernel, k_tiles=k_tiles, v_tiles=v_tiles),
        grid=(width // tn, n_i),
        in_specs=[pl.BlockSpec((tm, d), lambda j, i: (rb + i, 0)), pl.BlockSpec((tn, d), lambda j, i: (j, 0))],
        out_specs=[pl.BlockSpec((tm, tn), rest_index), pl.BlockSpec((tm, tn), own_index(k_tiles)),
                   pl.BlockSpec((tm, tn), own_index(v_tiles))],
        out_shape=[jax.ShapeDtypeStruct((n_rows, width - 2 * att_w), rest_dtype),
                   jax.ShapeDtypeStruct((n_rows, att_w), F32), jax.ShapeDtypeStruct((n_rows, att_w), F32)],
        scratch_shapes=[pltpu.VMEM((d, tn), BF16)],
        compiler_params=_cparams(("arbitrary", "arbitrary")),
        name=name,
    )(h_all, w_in_t)


def _prompt_attention(slope, tiles, first, q_ref, k_ref, v_ref, o_ref, kbf_ref, vaug_ref, sel_ref, alibi_ref):
    t, hd = q_ref.shape
    blk = MOBA_BLOCK
    nb = t // blk
    scale = hd ** -0.5

    if first:
        k = k_ref[...]
        kbf_ref[...] = (k * scale).astype(BF16)
        vaug_ref[:, :hd] = v_ref[...].astype(BF16)
        vaug_ref[:, hd:] = jnp.ones((t, hd), BF16)

        kmean = jnp.mean(k.reshape(nb, blk, hd), axis=1)
        gate = _dot3(kmean, q_ref[...].astype(F32), _dot_nt)
        kb = lax.broadcasted_iota(jnp.int32, (nb, t), 0)
        qb = lax.broadcasted_iota(jnp.int32, (nb, t), 1) // blk
        valid = kb < qb
        g = jnp.where(valid, gate, -jnp.inf)
        rank = jnp.zeros((nb, t), jnp.int32)
        for n2 in range(nb):
            gn = g[n2:n2 + 1, :]
            rank += ((gn > g) | ((gn == g) & (n2 < kb))).astype(jnp.int32)
        allowed = (valid & (rank < MOBA_TOPK)).astype(F32)
        sel_ref[...] = jnp.concatenate([allowed, jnp.zeros((LANES - nb, t), F32)], axis=0).T.astype(BF16)
        alibi_ref[...] = slope * (lax.broadcasted_iota(jnp.int32, (blk, t), 0)
                                  - lax.broadcasted_iota(jnp.int32, (blk, t), 1)).astype(F32)

    causal = (lax.broadcasted_iota(jnp.int32, (blk, blk), 0) >= lax.broadcasted_iota(jnp.int32, (blk, blk), 1))
    for i in tiles:
        w = (i + 1) * blk
        rows = pl.ds(i * blk, blk)
        s = _dot_nt(q_ref[rows, :].astype(BF16), kbf_ref[0:w, :]) - alibi_ref[:, 0:w]
        ok = causal
        if i > 0:
            expand = (lax.broadcasted_iota(jnp.int32, (LANES, i * blk), 0)
                      == lax.broadcasted_iota(jnp.int32, (LANES, i * blk), 1) // blk).astype(BF16)
            past_ok = _dot(sel_ref[rows, :], expand) > 0.5
            ok = jnp.concatenate([past_ok, causal], axis=1)
        s = jnp.where(ok, s, NEG)
        p = jnp.exp(s - jnp.max(s, axis=1, keepdims=True)).astype(BF16)
        out = _dot(p, vaug_ref[0:w, :])
        o_ref[rows, :] = (out[:, :hd] / out[:, hd:]).astype(o_ref.dtype)


DECODE_BLOCKS_PER_STEP = 8
PACKED_ROWS = 16


def _decode_attention(step, n_steps, q_ref, k_refs, v_refs, kn_ref, vn_ref, o_ref,
                      qflat_ref, mb_ref, ksum_ref, m_ref, l_ref, acc_ref, *, n_heads, hd, past_len, bps, slopes):
    tq = q_ref.shape[0]
    blk = MOBA_BLOCK
    scale = hd ** -0.5
    n_past = acc_ref.shape[0]
    rowpad = jnp.zeros((PACKED_ROWS - tq, hd), F32)
    page = blk // 2
    n_pairs = n_heads // 2
    pr = 2 * tq
    pk = 2 * blk
    pair_heads = [(g, g + n_pairs) for g in range(n_pairs)]

    def slope_col(g):
        second = lax.broadcasted_iota(jnp.int32, (pr, 1), 0) >= tq
        return jnp.where(second, slopes[pair_heads[g][1]], slopes[pair_heads[g][0]])

    @pl.when(step == 0)
    def _():
        qflat_ref[...] = jnp.concatenate(
            [q_ref[:, h * hd:(h + 1) * hd] for pair in pair_heads for h in pair], axis=0) * scale
        ksum_ref[...] = jnp.zeros_like(ksum_ref)
        r = lax.broadcasted_iota(jnp.int32, (pr, pk), 0)
        c = lax.broadcasted_iota(jnp.int32, (pr, pk), 1)
        in_block_dist = (r % tq - c // 2).astype(F32)
        for g in range(n_pairs):
            mb_ref[g] = jnp.where(c % 2 == r // tq, -slope_col(g) * in_block_dist, NEG)

    for j in range(bps):
        n = step * bps + j
        k0, k1 = k_refs[2 * j], k_refs[2 * j + 1]
        ksum_ref[pl.ds(pl.multiple_of(n * n_heads, n_heads), n_heads), :] = (
            jnp.sum(k0[...].reshape(page, n_heads, hd), axis=0) + jnp.sum(k1[...].reshape(page, n_heads, hd), axis=0))

    for g in range(n_pairs):
        pair_rows = pl.ds(g, 2 * page, stride=n_pairs)
        rs = slice(g * pr, (g + 1) * pr)
        kp = jnp.concatenate([kr[pair_rows, :] for kr in k_refs], axis=0).astype(BF16)
        s_all = _dot_nt(qflat_ref[rs, :].astype(BF16), kp)
        for j in range(bps):
            n = step * bps + j
            v0, v1 = v_refs[2 * j], v_refs[2 * j + 1]
            vp = jnp.concatenate([v0[pair_rows, :], v1[pair_rows, :]], axis=0).astype(BF16)
            s = s_all[:, j * pk:(j + 1) * pk] + mb_ref[g]
            m = jnp.max(s, axis=1, keepdims=True)
            p = jnp.exp(s - m)
            l = jnp.sum(p, axis=1, keepdims=True)
            acc_ref[n, rs, :] = _dot(p.astype(BF16), vp)
            block_dist = (past_len - n * blk).astype(F32)
            m_ref[n, rs, :] = jnp.broadcast_to(m - slope_col(g) * block_dist, (pr, hd))
            l_ref[n, rs, :] = jnp.broadcast_to(l, (pr, hd))

    @pl.when(step == n_steps - 1)
    def _():
        lane = lax.broadcasted_iota(jnp.int32, (tq, LANES), 1)
        valid = lane < n_past
        orow = lax.broadcasted_iota(jnp.int32, (PACKED_ROWS, PACKED_ROWS), 0)
        ocol = lax.broadcasted_iota(jnp.int32, (PACKED_ROWS, PACKED_ROWS), 1)
        d_own = (orow - ocol).astype(F32)
        for h in range(n_heads):
            hs = slice(h * hd, (h + 1) * hd)
            row0 = (h % n_pairs) * pr + (h // n_pairs) * tq
            rs = slice(row0, row0 + tq)
            qf = jnp.concatenate([qflat_ref[rs, :], rowpad], axis=0)
            kmean = ksum_ref[pl.ds(h, LANES, stride=n_heads), :] * (1.0 / blk)
            gate = _dot3(qf[:tq], kmean, _dot_nt)
            g = jnp.where(valid, gate, -jnp.inf)
            rank = jnp.zeros((tq, LANES), jnp.int32)
            for n2 in range(n_past):
                gn = g[:, n2:n2 + 1]
                rank += ((gn > g) | ((gn == g) & (n2 < lane))).astype(jnp.int32)
            sel = (valid & (rank < MOBA_TOPK)).astype(F32)

            kn = jnp.concatenate([kn_ref[:, hs], rowpad], axis=0).astype(BF16)
            vn = jnp.concatenate([vn_ref[:, hs], rowpad], axis=0).astype(BF16)
            s_own = jnp.where(d_own >= 0.0, _dot_nt(qf.astype(BF16), kn) - slopes[h] * d_own, NEG)
            m_own = jnp.max(s_own, axis=1, keepdims=True)
            p_own = jnp.exp(s_own - m_own)
            l_own = jnp.sum(p_own, axis=1, keepdims=True)[:tq]
            acc_own = _dot(p_own.astype(BF16), vn)[:tq]
            m_own = m_own[:tq]

            picked = [sel[:, n2:n2 + 1] > 0.0 for n2 in range(n_past)]
            m_fin = jnp.broadcast_to(m_own, (tq, hd))
            for n2 in range(n_past):
                m_fin = jnp.maximum(m_fin, jnp.where(picked[n2], m_ref[n2, rs, :], NEG))
            w_own = jnp.exp(m_own - m_fin)
            l_fin = w_own * l_own
            out = w_own * acc_own
            for n2 in range(n_past):
                w = jnp.where(picked[n2], jnp.exp(m_ref[n2, rs, :] - m_fin), 0.0)
                l_fin += w * l_ref[n2, rs, :]
                out += w * acc_ref[n2, rs, :]
            o_ref[:, hs] = out / l_fin


def _attention_kernel(pt_ref, slopes_ref, qn_ref, *refs, n_pages_step, decode_cfg, n_heads, tiles_per_step,
                      steps_per_head, n_work):
    del pt_ref
    k_refs, v_refs = refs[:n_pages_step], refs[n_pages_step:2 * n_pages_step]
    (kn_ref, vn_ref, qp_ref, kp_ref, vp_ref, od_ref, op_ref,
     qflat_ref, mb_ref, ksum_ref, m_ref, l_ref, acc_ref, kbf_ref, vaug_ref, sel_ref, alibi_ref) = refs[2 * n_pages_step:]
    step, n_steps = pl.program_id(1), pl.num_programs(1)
    _decode_attention(step, n_steps, qn_ref, k_refs, v_refs, kn_ref, vn_ref, od_ref,
                      qflat_ref, mb_ref, ksum_ref, m_ref, l_ref, acc_ref, **decode_cfg)

    g = pl.program_id(0) * n_steps + step
    phase = g % steps_per_head
    slope = slopes_ref[(g // steps_per_head) % n_heads]
    for ph in range(steps_per_head):
        @pl.when((g < n_work) & (phase == ph))
        def _(ph=ph):
            tiles = range(ph * tiles_per_step, (ph + 1) * tiles_per_step)
            _prompt_attention(slope, tiles, ph == 0, qp_ref, kp_ref, vp_ref, op_ref,
                              kbf_ref, vaug_ref, sel_ref, alibi_ref)


def _attention(q_p, k_p, v_p, slopes, batch_p, t, q_new, k_new, v_new, cache_k, cache_v, page_table,
               batch_d, tq, n_heads, hd):
    _, n_phys, page, _, _ = cache_k.shape
    width = n_heads * hd
    assert 2 * page == MOBA_BLOCK, "a MoBA block must span exactly two cache pages"
    assert tq <= PACKED_ROWS
    n_pages = page_table.shape[1]
    n_past = n_pages // 2
    bps = math.gcd(n_past, DECODE_BLOCKS_PER_STEP)
    n_steps = n_past // bps
    assert n_past <= LANES
    slopes_py = tuple(2.0 ** (-8.0 * (h + 1) / n_heads) for h in range(n_heads))

    nb = t // MOBA_BLOCK
    heads_p = batch_p * n_heads
    total_steps = batch_d * n_steps
    tiles_per_step = next(c for c in range(1, nb + 1) if nb % c == 0 and heads_p * (nb // c) <= total_steps)
    steps_per_head = nb // tiles_per_step
    n_work = heads_p * steps_per_head

    ck = cache_k.reshape(n_phys, page * n_heads, hd)
    cv = cache_v.reshape(n_phys, page * n_heads, hd)

    def page_spec(p):
        return pl.BlockSpec((None, page * n_heads, hd),
                            lambda b, s, pt, sl: (pt[b * n_pages + 2 * bps * s + p], 0, 0))

    def head_index(b, s, pt, sl):
        bh = jnp.minimum((b * n_steps + s) // steps_per_head, heads_p - 1)
        return (bh // n_heads, bh % n_heads)

    new_spec = pl.BlockSpec((tq, width), lambda b, s, pt, sl: (b, 0))
    head_spec = pl.BlockSpec((t, hd), head_index)
    pages = [page_spec(p) for p in range(2 * bps)]
    rows = n_heads * tq
    per_block = (n_past, rows, hd)
    grid_spec = pltpu.PrefetchScalarGridSpec(
        num_scalar_prefetch=2,
        grid=(batch_d, n_steps),
        in_specs=[new_spec] + pages + pages + [new_spec, new_spec, head_spec, head_spec, head_spec],
        out_specs=[new_spec, head_spec],
        scratch_shapes=[pltpu.VMEM((rows, hd), F32), pltpu.VMEM((n_heads // 2, 2 * tq, 2 * MOBA_BLOCK), F32),
                        pltpu.VMEM((LANES * n_heads, hd), F32), pltpu.VMEM(per_block, F32),
                        pltpu.VMEM(per_block, F32), pltpu.VMEM(per_block, F32),
                        pltpu.VMEM((t, hd), BF16), pltpu.VMEM((t, 2 * hd), BF16), pltpu.VMEM((t, LANES), BF16),
                        pltpu.VMEM((MOBA_BLOCK, t), F32)],
    )
    decode_cfg = dict(n_heads=n_heads, hd=hd, past_len=n_pages * page, bps=bps, slopes=slopes_py)
    att_d, att_p = pl.pallas_call(
        functools.partial(_attention_kernel, n_pages_step=2 * bps, decode_cfg=decode_cfg, n_heads=n_heads,
                          tiles_per_step=tiles_per_step, steps_per_head=steps_per_head, n_work=n_work),
        grid_spec=grid_spec,
        out_shape=[jax.ShapeDtypeStruct((batch_d * tq, width), F32),
                   jax.ShapeDtypeStruct((batch_p * t, width), BF16)],
        compiler_params=_cparams(("arbitrary", "arbitrary")),
        name="attention",
    )(page_table.reshape(-1), slopes, q_new, *([ck] * (2 * bps)), *([cv] * (2 * bps)), k_new, v_new, q_p, k_p, v_p)
    return att_p, att_d


def _mlstm_kernel(q_ref, k_ref, v_ref, og_ref, gcol_ref, grow_ref, c0_ref, n0_ref, m0_ref, gain_ref,
                  hm_ref, c_ref, n_ref, m_ref, *, chunk, hps):
    t = q_ref.shape[0]
    d = q_ref.shape[1] // hps
    lc = chunk
    c_ref[...] = c0_ref[...]
    n_ref[...] = n0_ref[...]
    m_ref[...] = m0_ref[...]
    r = lax.broadcasted_iota(jnp.int32, (lc, lc), 0)
    c = lax.broadcasted_iota(jnp.int32, (lc, lc), 1)
    lower = r >= c

    def one_chunk(hh, rows):
        cols = slice(hh * d, (hh + 1) * d)
        q = q_ref[rows, cols].astype(F32)
        k = k_ref[rows, cols].astype(F32) * (d ** -0.5)
        v = v_ref[rows, cols].astype(F32)
        gc = gcol_ref[hh, rows, :]
        ig_c, lf_c = gc[:, 0:1], gc[:, 1:2]
        gr = grow_ref[hh, :, rows]
        ig_r, lf_r = gr[0:1, :], gr[1:2, :]
        cmat, nvec, m0 = c_ref[hh], n_ref[hh], m_ref[hh]

        b_c = jnp.sum(jnp.where(lower, lf_r, 0.0), axis=1, keepdims=True)
        b_r = jnp.sum(jnp.where(r <= c, lf_c, 0.0), axis=0, keepdims=True)
        dmat = jnp.where(lower, b_c - b_r + ig_r, NEG)
        inter = b_c + m0
        mt = jnp.maximum(inter, jnp.max(dmat, axis=1, keepdims=True))
        w = jnp.exp(dmat - mt)
        a = jnp.exp(inter - mt)
        qb, kb, vb = q.astype(BF16), k.astype(BF16), v.astype(BF16)
        sc = _dot_nt(qb, kb) * w
        num = a * _dot_nt(qb, cmat.astype(BF16)) + _dot(sc.astype(BF16), vb)
        den = a * jnp.sum(q * nvec, axis=1, keepdims=True) + jnp.sum(sc, axis=1, keepdims=True)
        h = num / jnp.maximum(jnp.abs(den), jnp.exp(-mt))
        hn = _rms(h) * gain_ref[hh]
        hm_ref[rows, cols] = (hn * _sigmoid(og_ref[rows, cols].astype(F32))).astype(hm_ref.dtype)

        b_last = b_c[lc - 1:lc, :]
        g = b_last - b_c + ig_c
        m_new = jnp.maximum(b_last + m0, jnp.max(g, axis=0, keepdims=True))
        ws = jnp.exp(g - m_new)
        a_last = jnp.exp(b_last + m0 - m_new)
        c_ref[hh] = a_last * cmat + _dot_tn((v * ws).astype(BF16), kb)
        n_ref[hh] = a_last * nvec + jnp.sum(ws * k, axis=0, keepdims=True)
        m_ref[hh] = m_new

    for hh in range(hps):
        if t == lc:
            one_chunk(hh, pl.ds(0, lc))
        else:
            def body(ci, carry, hh=hh):
                one_chunk(hh, pl.ds(pl.multiple_of(ci * lc, lc), lc))
                return carry

            lax.fori_loop(0, t // lc, body, 0)


def _mlstm(proj, gcol, grow, c0, n0, m0, gain, batch, t, n_heads, d, col0, hps, out_dtype):
    lc = math.gcd(t, MOBA_BLOCK)
    wb = hps * d
    assert col0 % wb == 0 and n_heads % hps == 0

    def col(off):
        return pl.BlockSpec((t, wb), lambda b, h: (b, (col0 + off * n_heads * d) // wb + h))

    per_head = lambda shape: pl.BlockSpec((None, hps) + shape, lambda b, h: (b, h, 0, 0))
    return pl.pallas_call(
        functools.partial(_mlstm_kernel, chunk=lc, hps=hps),
        grid=(batch, n_heads // hps),
        in_specs=[col(0), col(1), col(2), col(3), per_head((t, 2)), per_head((2, t)),
                  per_head((d, d)), per_head((1, d)), per_head((1, 1)),
                  pl.BlockSpec((hps, 1, d), lambda b, h: (h, 0, 0))],
        out_specs=[pl.BlockSpec((t, wb), lambda b, h: (b, h)), per_head((d, d)), per_head((1, d)),
                   per_head((1, 1))],
        out_shape=[jax.ShapeDtypeStruct((batch * t, n_heads * d), out_dtype),
                   jax.ShapeDtypeStruct((batch, n_heads, d, d), F32),
                   jax.ShapeDtypeStruct((batch, n_heads, 1, d), F32),
                   jax.ShapeDtypeStruct((batch, n_heads, 1, 1), F32)],
        compiler_params=_cparams(("arbitrary", "arbitrary")),
        name=f"mlstm_t{t}",
    )(proj, proj, proj, proj, gcol, grow, c0, n0, m0, gain)


def _route(logits, n_groups, n_experts):
    epg = n_experts // n_groups
    lane = lax.broadcasted_iota(jnp.int32, logits.shape, 1).astype(F32)
    big = float(LANES)

    def rmax(x):
        return jnp.max(x, axis=1, keepdims=True)

    def first_lane(mask):
        return jnp.min(jnp.where(mask, lane, big), axis=1, keepdims=True)

    is_g = (lane >= n_experts) & (lane < n_experts + n_groups)
    gmax = rmax(jnp.where(is_g, logits, -jnp.inf))
    g_w = 1.0 / jnp.sum(jnp.where(is_g, jnp.exp(logits - gmax), 0.0), axis=1, keepdims=True)
    group = first_lane(is_g & (logits == gmax)) - n_experts
    lo = group * epg
    in_grp = (lane >= lo) & (lane < lo + epg)
    emax = rmax(jnp.where(in_grp, logits, -jnp.inf))
    pe = jnp.where(in_grp, jnp.exp(logits - emax), 0.0)
    prob = pe / jnp.sum(pe, axis=1, keepdims=True)
    p1 = rmax(jnp.where(in_grp, prob, -1.0))
    i1 = first_lane(in_grp & (prob == p1))
    rest = in_grp & (lane != i1)
    p2 = rmax(jnp.where(rest, prob, -1.0))
    i2 = first_lane(rest & (prob == p2))
    den = p1 + p2
    gate = g_w * jnp.where(lane == i1, p1 / den, jnp.where(lane == i2, p2 / den, 0.0))
    return gate, group


def _outproj_kernel(n_p, att_w, n_groups, n_experts,
                    attp, atts, hmp, hms, xp, xs, g1p, g1s, sh2p, sh2s, sc2p, sc2s,
                    wo_ref, gffn_ref, wr_ref, br_ref, x1_ref, h2g_ref, rinfo_ref, counts_ref, carry_ref):
    d = x1_ref.shape[1]

    @pl.when(pl.program_id(0) == 0)
    def _():
        carry_ref[...] = jnp.zeros_like(carry_ref)

    def body(att_ref, hm_ref, x_ref, g1_ref, sh2_ref, sc2_ref):
        mixed = (_dot(att_ref[...].astype(BF16), wo_ref[0:att_w, :])
                 + _dot(hm_ref[...].astype(BF16), wo_ref[att_w:, :]))
        x1 = x_ref[...] + g1_ref[...] * mixed
        x1_ref[...] = x1
        h2 = _rms(x1) * gffn_ref[...] * (1.0 + sc2_ref[...]) + sh2_ref[...]
        gate, group = _route(_dot3(h2, wr_ref[...]) + br_ref[...], n_groups, n_experts)
        h2g_ref[:, :d] = h2
        h2g_ref[:, d:] = gate
        tm = h2.shape[0]
        lane = lax.broadcasted_iota(jnp.int32, (tm, LANES), 1).astype(F32)
        onehot = (lane == group).astype(F32)
        r = lax.broadcasted_iota(jnp.int32, (tm, tm), 0)
        c = lax.broadcasted_iota(jnp.int32, (tm, tm), 1)
        before = _dot((r > c).astype(BF16), onehot.astype(BF16)) + carry_ref[...]
        rank = jnp.sum(onehot * before, axis=1, keepdims=True)
        rinfo_ref[...] = jnp.where(lane == 0.0, group, jnp.where(lane == 1.0, rank, 0.0))
        carry_ref[...] += jnp.sum(onehot, axis=0, keepdims=True)
        counts_ref[...] = carry_ref[...]

    _dual(n_p, body, (attp, hmp, xp, g1p, sh2p, sc2p), (atts, hms, xs, g1s, sh2s, sc2s))


def _outproj_stage(att_p, att_s, hm_p, hm_s, x_p, x_s, g1_p, g1_s, sh2_p, sh2_s, sc2_p, sc2_s,
                   w_out_bf, g_ffn, w_route, b_route, n_groups, n_experts, t_p):
    tm = TOKEN_TILE
    d = x_p.shape[1]
    att_w = att_p.shape[1]
    ml_w = hm_p.shape[1]
    n_p, n_s = x_p.shape[0] // tm, x_s.shape[0] // tm
    n_tok = x_p.shape[0] + x_s.shape[0]
    prow = lambda w: pl.BlockSpec((tm, w), _p_rows(n_p))
    srow = lambda w: pl.BlockSpec((tm, w), _s_rows(n_p))
    pbat = pl.BlockSpec((None, 1, d), _p_batch(n_p, t_p // tm))
    const = lambda shape: pl.BlockSpec(shape, lambda i: (0,) * len(shape))
    out_row = lambda w: pl.BlockSpec((tm, w), lambda i: (i, 0))
    return pl.pallas_call(
        functools.partial(_outproj_kernel, n_p, att_w, n_groups, n_experts),
        grid=(n_p + n_s,),
        in_specs=[prow(att_w), srow(att_w), prow(ml_w), srow(ml_w), prow(d), srow(d),
                  pbat, srow(d), pbat, srow(d), pbat, srow(d),
                  const((att_w + ml_w, d)), const((1, d)), const((d, LANES)), const((1, LANES))],
        out_specs=[out_row(d), out_row(d + LANES), out_row(LANES), const((1, LANES))],
        out_shape=[jax.ShapeDtypeStruct((n_tok, d), F32), jax.ShapeDtypeStruct((n_tok, d + LANES), F32),
                   jax.ShapeDtypeStruct((n_tok, LANES), F32), jax.ShapeDtypeStruct((1, LANES), F32)],
        scratch_shapes=[pltpu.VMEM((1, LANES), F32)],
        compiler_params=_cparams(("arbitrary",)),
        name="out_projection_router",
    )(att_p, att_s, hm_p, hm_s, x_p, x_s, g1_p, g1_s, sh2_p, sh2_s, sc2_p, sc2_s,
      w_out_bf, g_ffn, w_route, b_route)


MOE_TILE = 384
ROW_DMA_UNROLL = 8


class _RowGather:
    def __init__(self, idx_ref, src_hbm, buf_ref, sem):
        self.idx_ref, self.src_hbm, self.buf_ref, self.sem = idx_ref, src_hbm, buf_ref, sem
        self.tm = buf_ref.shape[1]

    def _copy(self, row_index, r, slot):
        return pltpu.make_async_copy(self.src_hbm.at[pl.ds(row_index, 1)],
                                     self.buf_ref.at[slot, pl.ds(r, 1)], self.sem.at[slot])

    def start(self, tile, slot):
        def body(r, carry):
            self._copy(self.idx_ref[tile * self.tm + r], r, slot).start()
            return carry

        lax.fori_loop(0, self.tm, body, 0, unroll=ROW_DMA_UNROLL)

    def start_inline(self, tile, slot, first_row, count):
        for u in range(count):
            r = first_row + u
            self._copy(self.idx_ref[tile * self.tm + r], r, slot).start(priority=u % 2)

    def wait(self, slot):
        def body(r, carry):
            self._copy(0, r, slot).wait()
            return carry

        lax.fori_loop(0, self.tm, body, 0, unroll=ROW_DMA_UNROLL)


def _moe_kernel(tg_ref, nu_ref, src_ref, h2g_hbm, wg_ref, wu_ref, wd_ref, o_ref, hbuf_ref, sem, *, epg):
    t = pl.program_id(0)
    n_tiles = pl.num_programs(0)
    tm, d = o_ref.shape
    n_used = nu_ref[0]
    used = t < n_used
    slot = t % 2
    rows = _RowGather(src_ref, h2g_hbm, hbuf_ref, sem)

    @pl.when(t == 0)
    def _():
        rows.start(0, 0)

    @pl.when(t <= n_used)
    def _():
        rows.wait(slot)

    @pl.when(jnp.logical_not(used))
    def _():
        o_ref[...] = jnp.zeros_like(o_ref)

    @pl.when(used)
    def _():
        rows.start_inline(jnp.minimum(t + 1, n_tiles - 1), 1 - slot, 0, tm)
        h2 = hbuf_ref[slot, :, :d].astype(BF16)
        gate = hbuf_ref[slot, :, d:]
        lane = lax.broadcasted_iota(jnp.int32, gate.shape, 1)
        for j in range(epg):
            hg = _dot(h2, wg_ref[j])
            hu = _dot(h2, wu_ref[j])
            ge = jnp.sum(jnp.where(lane == tg_ref[t] * epg + j, gate, 0.0), axis=1, keepdims=True)
            y = _dot((hg * _sigmoid(hg) * hu * ge).astype(BF16), wd_ref[j])
            if j == 0:
                o_ref[...] = y
            else:
                o_ref[...] += y

    @pl.when(used & (t == n_tiles - 1))
    def _():
        rows.wait(1 - slot)


def _moe_stage(h2g, rinfo, counts, w_gate_bf, w_up_bf, w_down_bf, n_groups):
    n_tok, dw = h2g.shape
    d = dw - LANES
    n_experts, _, ff = w_gate_bf.shape
    epg = n_experts // n_groups
    tm = MOE_TILE
    n_tiles = (n_tok + n_groups * (tm - 1)) // tm

    group = rinfo[:, 0].astype(jnp.int32)
    rank = rinfo[:, 1].astype(jnp.int32)
    cnt = counts[0, :n_groups].astype(jnp.int32)
    tiles_g = (cnt + tm - 1) // tm
    tile_end = jnp.cumsum(tiles_g)
    row_start = (tile_end - tiles_g) * tm
    dest = row_start[group] + rank
    tile_group = jnp.minimum(jnp.searchsorted(tile_end, jnp.arange(n_tiles, dtype=jnp.int32), side="right"),
                             n_groups - 1).astype(jnp.int32)
    n_used = tile_end[-1:].astype(jnp.int32)
    src = jnp.zeros((n_tiles * tm,), jnp.int32).at[dest].set(jnp.arange(n_tok, dtype=jnp.int32))

    def w_spec(rows_, cols_):
        return pl.BlockSpec((None, epg, rows_, cols_),
                            lambda t, tg, nu, sr: (jnp.where(t < nu[0], tg[t], n_groups - 1), 0, 0, 0),
                            pipeline_mode=pl.Buffered(1))

    by_group = lambda w: w.reshape((n_groups, epg) + w.shape[1:])
    grid_spec = pltpu.PrefetchScalarGridSpec(
        num_scalar_prefetch=3,
        grid=(n_tiles,),
        in_specs=[pl.BlockSpec(memory_space=pl.ANY), w_spec(d, ff), w_spec(d, ff), w_spec(ff, d)],
        out_specs=pl.BlockSpec((tm, d), lambda t, tg, nu, sr: (t, 0)),
        scratch_shapes=[pltpu.VMEM((2, tm, dw), F32), pltpu.SemaphoreType.DMA((2,))],
    )
    ys = pl.pallas_call(
        functools.partial(_moe_kernel, epg=epg),
        grid_spec=grid_spec,
        out_shape=jax.ShapeDtypeStruct((n_tiles * tm, d), F32),
        compiler_params=_cparams(("arbitrary",)),
        name="moe_experts",
    )(tile_group, n_used, src, h2g, by_group(w_gate_bf), by_group(w_up_bf), by_group(w_down_bf))
    return ys, dest


def _final_kernel(n_p, dest_ref, x1_ref, moe_hbm, g2p, g2s, gfin_ref, yp_ref, ys_ref, mbuf_ref, sem):
    i = pl.program_id(0)
    slot = i % 2
    rows = _RowGather(dest_ref, moe_hbm, mbuf_ref, sem)

    @pl.when(i == 0)
    def _():
        rows.start(0, 0)

    rows.wait(slot)
    n = pl.num_programs(0)

    def body(g2_ref, y_ref):
        rows.start_inline(jnp.minimum(i + 1, n - 1), 1 - slot, 0, rows.tm)
        x2 = x1_ref[...] + g2_ref[...] * mbuf_ref[slot]
        y_ref[...] = _rms(x2) * gfin_ref[...]

    _dual(n_p, body, (g2p, yp_ref), (g2s, ys_ref))

    @pl.when(i == n - 1)
    def _():
        rows.wait(1 - slot)


def _final_stage(x1, moe_rows, dest, g2_p, g2_s, g_final, n_tok_p, t_p):
    tm = TOKEN_TILE
    n_tok, d = x1.shape
    n_p = n_tok_p // tm
    n_s = (n_tok - n_tok_p) // tm
    grid_spec = pltpu.PrefetchScalarGridSpec(
        num_scalar_prefetch=1,
        grid=(n_p + n_s,),
        in_specs=[pl.BlockSpec((tm, d), lambda i, ds: (i, 0)), pl.BlockSpec(memory_space=pl.ANY),
                  pl.BlockSpec((None, 1, d), lambda i, ds: _p_batch(n_p, t_p // tm)(i)),
                  pl.BlockSpec((tm, d), lambda i, ds: _s_rows(n_p)(i)), pl.BlockSpec((1, d), lambda i, ds: (0, 0))],
        out_specs=[pl.BlockSpec((tm, d), lambda i, ds: _p_rows(n_p)(i)),
                   pl.BlockSpec((tm, d), lambda i, ds: _s_rows(n_p)(i))],
        scratch_shapes=[pltpu.VMEM((2, tm, d), F32), pltpu.SemaphoreType.DMA((2,))],
    )
    return pl.pallas_call(
        functools.partial(_final_kernel, n_p),
        grid_spec=grid_spec,
        out_shape=[jax.ShapeDtypeStruct((n_tok_p, d), F32), jax.ShapeDtypeStruct((n_tok - n_tok_p, d), F32)],
        compiler_params=_cparams(("arbitrary",)),
        name="final_norm",
    )(dest, x1, moe_rows, g2_p, g2_s, g_final)


def kernel(x_prompt, x_sample, cache_k, cache_v, page_table, state_C, state_n, state_m, c_prompt, c_sample,
           w_in, b_ig, b_fg, ml_gain, w_out, g_mix, g_ffn, w_mod, b_mod, w_grp, b_grp, w_exp, b_exp,
           w_gate, w_up, w_down, g_final):
    bp, tp, d = x_prompt.shape
    bd, td, _ = x_sample.shape
    depth = w_in.shape[0]
    assert depth == 1, "one trunk layer"
    n_heads, hd = cache_k.shape[-2:]
    att_w = n_heads * hd
    nh_ml, d_ml = state_n.shape[-2:]
    ml_w = nh_ml * d_ml
    assert att_w == ml_w and att_w + ml_w == d
    n_groups, n_experts = w_grp.shape[-1], w_exp.shape[-1]
    assert n_experts + n_groups <= LANES and 2 * nh_ml <= LANES
    main_w = 3 * att_w + 4 * ml_w
    ntp, nts = bp * tp, bd * td

    c_rows = bp + bd
    c_pad = -c_rows % 16
    c_all = jnp.concatenate([c_prompt, c_sample, jnp.zeros((c_pad, d), F32)], axis=0)
    mod = _modulation(c_all, w_mod[0], b_mod[0])
    mod_p = mod[:bp].reshape(bp, N_MOD, 1, d)
    mod_s = jnp.repeat(mod[bp:c_rows].reshape(bd, N_MOD, d), td, axis=0)
    shift1_p, scale1_p, gate1_p, shift2_p, scale2_p, gate2_p = (mod_p[:, i] for i in range(N_MOD))
    shift1_s, scale1_s, gate1_s, shift2_s, scale2_s, gate2_s = (mod_s[:, i] for i in range(N_MOD))

    x_p = x_prompt.reshape(ntp, d)
    x_s = x_sample.reshape(nts, d)

    b_gates = jnp.pad(jnp.concatenate([b_ig[0], b_fg[0]]), (0, LANES - 2 * nh_ml)).reshape(1, LANES)
    w_in_t = w_in[0].T
    h_all, gates = _norm_stage(x_p, x_s, shift1_p, shift1_s, scale1_p, scale1_s, g_mix[0].reshape(1, d),
                               w_in_t, main_w, b_gates, nh_ml, tp)

    rest_p, k_p, v_p = _in_projection(h_all, w_in_t, 0, ntp, main_w, att_w, BF16, "in_projection_prompt")
    rest_s, k_s, v_s = _in_projection(h_all, w_in_t, ntp, nts, main_w, att_w, F32, "in_projection_decode")

    slopes = 2.0 ** (-8.0 * jnp.arange(1, n_heads + 1, dtype=F32) / n_heads)
    att_p, att_s = _attention(rest_p, k_p, v_p, slopes, bp, tp, rest_s, k_s, v_s, cache_k, cache_v, page_table,
                              bd, td, n_heads, hd)

    def gate_layouts(g, b, t):
        g = g[:, :2 * nh_ml].reshape(b, t, 2, nh_ml)
        return g.transpose(0, 3, 1, 2), g.transpose(0, 3, 2, 1)

    gcol_p, grow_p = gate_layouts(gates[:ntp], bp, tp)
    gcol_s, grow_s = gate_layouts(gates[ntp:], bd, td)
    gain = ml_gain[0].reshape(nh_ml, 1, d_ml)
    zeros = lambda *s: jnp.zeros(s, F32)
    hm_p, c_p, n_p, m_p = _mlstm(rest_p, gcol_p, grow_p, zeros(bp, nh_ml, d_ml, d_ml), zeros(bp, nh_ml, 1, d_ml),
                                 zeros(bp, nh_ml, 1, 1), gain, bp, tp, nh_ml, d_ml, att_w, 1, BF16)
    hm_s, c_s, n_s, m_s = _mlstm(rest_s, gcol_s, grow_s, state_C[0], state_n[0].reshape(bd, nh_ml, 1, d_ml),
                                 state_m[0].reshape(bd, nh_ml, 1, 1), gain, bd, td, nh_ml, d_ml, att_w, nh_ml, F32)

    w_route = jnp.pad(jnp.concatenate([w_exp[0], w_grp[0]], axis=1), ((0, 0), (0, LANES - n_experts - n_groups)))
    b_route = jnp.pad(jnp.concatenate([b_exp[0], b_grp[0]]), (0, LANES - n_experts - n_groups)).reshape(1, LANES)
    x1, h2g, rinfo, counts = _outproj_stage(att_p, att_s, hm_p, hm_s, x_p, x_s, gate1_p, gate1_s, shift2_p,
                                            shift2_s, scale2_p, scale2_s, w_out[0].astype(BF16),
                                            g_ffn[0].reshape(1, d), w_route, b_route, n_groups, n_experts, tp)

    moe_rows, dest = _moe_stage(h2g, rinfo, counts, w_gate[0].astype(BF16), w_up[0].astype(BF16),
                                w_down[0].astype(BF16), n_groups)
    y_p, y_s = _final_stage(x1, moe_rows, dest, gate2_p, gate2_s, g_final.reshape(1, d), ntp, tp)

    kv_shape_p = (1, bp, tp, n_heads, hd)
    kv_shape_s = (1, bd, td, n_heads, hd)
    return (y_p.reshape(bp, tp, d), y_s.reshape(bd, td, d),
            k_p.reshape(kv_shape_p), v_p.reshape(kv_shape_p), k_s.reshape(kv_shape_s), v_s.reshape(kv_shape_s),
            c_p[None], n_p.reshape(1, bp, nh_ml, d_ml), m_p.reshape(1, bp, nh_ml),
            c_s[None], n_s.reshape(1, bd, nh_ml, d_ml), m_s.reshape(1, bd, nh_ml))
```

```python
import functools
import math

import jax
import jax.numpy as jnp
from jax import lax
from jax.experimental import pallas as pl
from jax.experimental.pallas import tpu as pltpu

F32 = jnp.float32
BF16 = jnp.bfloat16

MOBA_BLOCK = 256
MOBA_TOPK = 3
EXPERT_TOPK = 2
N_MOD = 6
EPS = 1e-6
NEG = -1e30

LANES = 128
V7X_VMEM_LIMIT = 56 * 1024 * 1024
TOKEN_TILE = 256


def _cparams(sem):
    return pltpu.CompilerParams(dimension_semantics=sem, vmem_limit_bytes=V7X_VMEM_LIMIT)


def _pick_tile(n, candidates):
    for c in candidates:
        if n % c == 0:
            return c
    raise ValueError(f"no tile in {candidates} divides {n}")


def _dot(a, b):
    return jnp.dot(a, b, preferred_element_type=F32)


def _dot_nt(a, b):
    return lax.dot_general(a, b, (((1,), (1,)), ((), ())), preferred_element_type=F32)


def _dot_tn(a, b):
    return lax.dot_general(a, b, (((0,), (0,)), ((), ())), preferred_element_type=F32)


def _split(x):
    hi = x.astype(BF16)
    lo = (x - hi.astype(F32)).astype(BF16)
    return hi, lo


def _dot3(a, b, dot=_dot):
    a_hi, a_lo = _split(a)
    b_hi, b_lo = _split(b)
    return dot(a_hi, b_hi) + dot(a_lo, b_hi) + dot(a_hi, b_lo)


def _sigmoid(x):
    return 1.0 / (1.0 + jnp.exp(-x))


def _log_sigmoid(x):
    return jnp.minimum(x, 0.0) - jnp.log(1.0 + jnp.exp(-jnp.abs(x)))


def _rms(x):
    return x * lax.rsqrt(jnp.mean(x * x, axis=-1, keepdims=True) + EPS)


def _mod_kernel(c_ref, w_ref, b_ref, o_ref):
    o_ref[...] = _dot3(c_ref[...], w_ref[...]) + b_ref[...]


def _modulation(c_all, w_mod, b_mod):
    rows, d = c_all.shape
    n = w_mod.shape[1]
    tn = _pick_tile(n, (1024, 512, 256, 128))
    return pl.pallas_call(
        _mod_kernel,
        grid=(n // tn,),
        in_specs=[
            pl.BlockSpec((rows, d), lambda j: (0, 0)),
            pl.BlockSpec((d, tn), lambda j: (0, j)),
            pl.BlockSpec((1, tn), lambda j: (0, j)),
        ],
        out_specs=pl.BlockSpec((rows, tn), lambda j: (0, j)),
        out_shape=jax.ShapeDtypeStruct((rows, n), F32),
        compiler_params=_cparams(("arbitrary",)),
        name="modulation",
    )(c_all, w_mod, b_mod.reshape(1, n))


def _p_rows(n_p):
    return lambda i: (jnp.minimum(i, n_p - 1), 0)


def _p_batch(n_p, tiles_per_batch):
    return lambda i: (jnp.minimum(i, n_p - 1) // tiles_per_batch, 0, 0)


def _s_rows(n_p):
    return lambda i: (jnp.maximum(i - n_p, 0), 0)


def _dual(n_p, body, p_refs, s_refs):
    i = pl.program_id(0)

    @pl.when(i < n_p)
    def _():
        body(*p_refs)

    @pl.when(i >= n_p)
    def _():
        body(*s_refs)


def _norm_kernel(n_p, n_heads, xp, xs, shp, shs, scp, scs, g_ref, wg_ref, bg_ref, h_ref, gates_ref):
    def body(x_ref, sh_ref, sc_ref):
        h = _rms(x_ref[...]) * g_ref[...] * (1.0 + sc_ref[...]) + sh_ref[...]
        h_ref[...] = h.astype(BF16)
        wg = wg_ref[...]
        wg = jnp.concatenate([wg, jnp.zeros((LANES - wg.shape[0], wg.shape[1]), F32)], axis=0)
        g = _dot3(h, wg, _dot_nt) + bg_ref[...]
        lane = lax.broadcasted_iota(jnp.int32, g.shape, 1)
        gates_ref[...] = jnp.where(lane < n_heads, g, _log_sigmoid(g))

    _dual(n_p, body, (xp, shp, scp), (xs, shs, scs))


def _norm_stage(x_p, x_s, sh_p, sh_s, sc_p, sc_s, g_mix, w_in_t, gate_row0, b_gates, n_heads, t_p):
    tm = TOKEN_TILE
    d = x_p.shape[1]
    n_p, n_s = x_p.shape[0] // tm, x_s.shape[0] // tm
    n_tok = x_p.shape[0] + x_s.shape[0]
    n_gate = 2 * n_heads
    assert gate_row0 % n_gate == 0 and w_in_t.shape[0] - gate_row0 == n_gate and n_gate % 8 == 0
    row = pl.BlockSpec((tm, d), _p_rows(n_p))
    srow = pl.BlockSpec((tm, d), _s_rows(n_p))
    pbat = pl.BlockSpec((None, 1, d), _p_batch(n_p, t_p // tm))
    const = lambda shape: pl.BlockSpec(shape, lambda i: (0,) * len(shape))
    w_gate_cols = pl.BlockSpec((n_gate, d), lambda i: (gate_row0 // n_gate, 0))
    return pl.pallas_call(
        functools.partial(_norm_kernel, n_p, n_heads),
        grid=(n_p + n_s,),
        in_specs=[row, srow, pbat, srow, pbat, srow, const((1, d)), w_gate_cols, const((1, LANES))],
        out_specs=[pl.BlockSpec((tm, d), lambda i: (i, 0)), pl.BlockSpec((tm, LANES), lambda i: (i, 0))],
        out_shape=[jax.ShapeDtypeStruct((n_tok, d), BF16), jax.ShapeDtypeStruct((n_tok, LANES), F32)],
        compiler_params=_cparams(("arbitrary",)),
        name="norm_gates",
    )(x_p, x_s, sh_p, sh_s, sc_p, sc_s, g_mix, w_in_t, b_gates)


INPROJ_COL_TILE = 1024


def _inproj_kernel(h_ref, wt_ref, rest_ref, k_ref, v_ref, wbf_ref, *, k_tiles, v_tiles):
    j = pl.program_id(0)

    @pl.when(pl.program_id(1) == 0)
    def _():
        wbf_ref[...] = wt_ref[...].T.astype(BF16)

    acc = _dot(h_ref[...], wbf_ref[...])
    is_k = (j >= k_tiles[0]) & (j < k_tiles[1])
    is_v = (j >= v_tiles[0]) & (j < v_tiles[1])

    @pl.when(is_k)
    def _():
        k_ref[...] = acc

    @pl.when(is_v)
    def _():
        v_ref[...] = acc

    @pl.when(jnp.logical_not(is_k | is_v))
    def _():
        rest_ref[...] = acc.astype(rest_ref.dtype)


def _in_projection(h_all, w_in_t, row0, n_rows, width, att_w, rest_dtype, name):
    d = h_all.shape[1]
    tn = math.gcd(att_w, INPROJ_COL_TILE)
    tm = _pick_tile(n_rows, (1024, 512, 256))
    n_i = n_rows // tm
    rb = row0 // tm
    assert row0 % tm == 0
    last = n_i - 1
    per = att_w // tn
    k_tiles, v_tiles = (per, 2 * per), (2 * per, 3 * per)

    def rest_index(j, i):
        skipping = (j >= k_tiles[0]) & (j < v_tiles[1])
        return (jnp.where(skipping, last, i),
                jnp.where(j < k_tiles[0], j, jnp.where(skipping, k_tiles[0] - 1, j - 2 * per)))

    def own_index(tiles):
        lo, hi = tiles
        return lambda j, i: (jnp.where(j < lo, 0, jnp.where(j < hi, i, last)),
                             jnp.where(j < lo, 0, jnp.where(j < hi, j - lo, hi - lo - 1)))

    return pl.pallas_call(
        functools.partial(_inproj_kernel, k_tiles=k_tiles, v_tiles=v_tiles),
        grid=(width // tn, n_i),
        in_specs=[pl.BlockSpec((tm, d), lambda j, i: (rb + i, 0)), pl.BlockSpec((tn, d), lambda j, i: (j, 0))],
        out_specs=[pl.BlockSpec((tm, tn), rest_index), pl.BlockSpec((tm, tn), own_index(k_tiles)),
                   pl.BlockSpec((tm, tn), own_index(v_tiles))],
        out_shape=[jax.ShapeDtypeStruct((n_rows, width - 2 * att_w), rest_dtype),
                   jax.ShapeDtypeStruct((n_rows, att_w), F32), jax.ShapeDtypeStruct((n_rows, att_w), F32)],
        scratch_shapes=[pltpu.VMEM((d, tn), BF16)],
        compiler_params=_cparams(("arbitrary", "arbitrary")),
        name=name,
    )(h_all, w_in_t)


def _prompt_attention(slope, tiles, first, q_ref, k_ref, v_ref, o_ref, kbf_ref, vaug_ref, sel_ref, alibi_ref):
    t, hd = q_ref.shape
    blk = MOBA_BLOCK
    nb = t // blk
    scale = hd ** -0.5

    if first:
        k = k_ref[...]
        kbf_ref[...] = (k * scale).astype(BF16)
        vaug_ref[:, :hd] = v_ref[...].astype(BF16)
        vaug_ref[:, hd:] = jnp.ones((t, hd), BF16)

        kmean = jnp.mean(k.reshape(nb, blk, hd), axis=1)
        gate = _dot3(kmean, q_ref[...].astype(F32), _dot_nt)
        kb = lax.broadcasted_iota(jnp.int32, (nb, t), 0)
        qb = lax.broadcasted_iota(jnp.int32, (nb, t), 1) // blk
        valid = kb < qb
        g = jnp.where(valid, gate, -jnp.inf)
        rank = jnp.zeros((nb, t), jnp.int32)
        for n2 in range(nb):
            gn = g[n2:n2 + 1, :]
            rank += ((gn > g) | ((gn == g) & (n2 < kb))).astype(jnp.int32)
        allowed = (valid & (rank < MOBA_TOPK)).astype(F32)
        sel_ref[...] = jnp.concatenate([allowed, jnp.zeros((LANES - nb, t), F32)], axis=0).T.astype(BF16)
        alibi_ref[...] = slope * (lax.broadcasted_iota(jnp.int32, (blk, t), 0)
                                  - lax.broadcasted_iota(jnp.int32, (blk, t), 1)).astype(F32)

    causal = (lax.broadcasted_iota(jnp.int32, (blk, blk), 0) >= lax.broadcasted_iota(jnp.int32, (blk, blk), 1))
    for i in tiles:
        w = (i + 1) * blk
        rows = pl.ds(i * blk, blk)
        s = _dot_nt(q_ref[rows, :].astype(BF16), kbf_ref[0:w, :]) - alibi_ref[:, 0:w]
        ok = causal
        if i > 0:
            expand = (lax.broadcasted_iota(jnp.int32, (LANES, i * blk), 0)
                      == lax.broadcasted_iota(jnp.int32, (LANES, i * blk), 1) // blk).astype(BF16)
            past_ok = _dot(sel_ref[rows, :], expand) > 0.5
            ok = jnp.concatenate([past_ok, causal], axis=1)
        s = jnp.where(ok, s, NEG)
        p = jnp.exp(s - jnp.max(s, axis=1, keepdims=True)).astype(BF16)
        out = _dot(p, vaug_ref[0:w, :])
        o_ref[rows, :] = (out[:, :hd] / out[:, hd:]).astype(o_ref.dtype)


DECODE_BLOCKS_PER_STEP = 8
PACKED_ROWS = 16


def _decode_attention(step, n_steps, q_ref, k_refs, v_refs, kn_ref, vn_ref, o_ref,
                      qflat_ref, mb_ref, ksum_ref, m_ref, l_ref, acc_ref, *, n_heads, hd, past_len, bps, slopes):
    tq = q_ref.shape[0]
    blk = MOBA_BLOCK
    scale = hd ** -0.5
    n_past = acc_ref.shape[0]
    rowpad = jnp.zeros((PACKED_ROWS - tq, hd), F32)
    page = blk // 2
    n_pairs = n_heads // 2
    pr = 2 * tq
    pk = 2 * blk
    pair_heads = [(g, g + n_pairs) for g in range(n_pairs)]

    def slope_col(g):
        second = lax.broadcasted_iota(jnp.int32, (pr, 1), 0) >= tq
        return jnp.where(second, slopes[pair_heads[g][1]], slopes[pair_heads[g][0]])

    @pl.when(step == 0)
    def _():
        qflat_ref[...] = jnp.concatenate(
            [q_ref[:, h * hd:(h + 1) * hd] for pair in pair_heads for h in pair], axis=0) * scale
        ksum_ref[...] = jnp.zeros_like(ksum_ref)
        r = lax.broadcasted_iota(jnp.int32, (pr, pk), 0)
        c = lax.broadcasted_iota(jnp.int32, (pr, pk), 1)
        in_block_dist = (r % tq - c // 2).astype(F32)
        for g in range(n_pairs):
            mb_ref[g] = jnp.where(c % 2 == r // tq, -slope_col(g) * in_block_dist, NEG)

    for j in range(bps):
        n = step * bps + j
        k0, k1 = k_refs[2 * j], k_refs[2 * j + 1]
        ksum_ref[pl.ds(pl.multiple_of(n * n_heads, n_heads), n_heads), :] = (
            jnp.sum(k0[...].reshape(page, n_heads, hd), axis=0) + jnp.sum(k1[...].reshape(page, n_heads, hd), axis=0))

    for g in range(n_pairs):
        pair_rows = pl.ds(g, 2 * page, stride=n_pairs)
        rs = slice(g * pr, (g + 1) * pr)
        kp = jnp.concatenate([kr[pair_rows, :] for kr in k_refs], axis=0).astype(BF16)
        s_all = _dot_nt(qflat_ref[rs, :].astype(BF16), kp)
        for j in range(bps):
            n = step * bps + j
            v0, v1 = v_refs[2 * j], v_refs[2 * j + 1]
            vp = jnp.concatenate([v0[pair_rows, :], v1[pair_rows, :]], axis=0).astype(BF16)
            s = s_all[:, j * pk:(j + 1) * pk] + mb_ref[g]
            m = jnp.max(s, axis=1, keepdims=True)
            p = jnp.exp(s - m)
            l = jnp.sum(p, axis=1, keepdims=True)
            acc_ref[n, rs, :] = _dot(p.astype(BF16), vp)
            block_dist = (past_len - n * blk).astype(F32)
            m_ref[n, rs, :] = jnp.broadcast_to(m - slope_col(g) * block_dist, (pr, hd))
            l_ref[n, rs, :] = jnp.broadcast_to(l, (pr, hd))

    @pl.when(step == n_steps - 1)
    def _():
        lane = lax.broadcasted_iota(jnp.int32, (tq, LANES), 1)
        valid = lane < n_past
        orow = lax.broadcasted_iota(jnp.int32, (PACKED_ROWS, PACKED_ROWS), 0)
        ocol = lax.broadcasted_iota(jnp.int32, (PACKED_ROWS, PACKED_ROWS), 1)
        d_own = (orow - ocol).astype(F32)
        for h in range(n_heads):
            hs = slice(h * hd, (h + 1) * hd)
            row0 = (h % n_pairs) * pr + (h // n_pairs) * tq
            rs = slice(row0, row0 + tq)
            qf = jnp.concatenate([qflat_ref[rs, :], rowpad], axis=0)
            kmean = ksum_ref[pl.ds(h, LANES, stride=n_heads), :] * (1.0 / blk)
            gate = _dot3(qf[:tq], kmean, _dot_nt)
            g = jnp.where(valid, gate, -jnp.inf)
            rank = jnp.zeros((tq, LANES), jnp.int32)
            for n2 in range(n_past):
                gn = g[:, n2:n2 + 1]
                rank += ((gn > g) | ((gn == g) & (n2 < lane))).astype(jnp.int32)
            sel = (valid & (rank < MOBA_TOPK)).astype(F32)

            kn = jnp.concatenate([kn_ref[:, hs], rowpad], axis=0).astype(BF16)
            vn = jnp.concatenate([vn_ref[:, hs], rowpad], axis=0).astype(BF16)
            s_own = jnp.where(d_own >= 0.0, _dot_nt(qf.astype(BF16), kn) - slopes[h] * d_own, NEG)
            m_own = jnp.max(s_own, axis=1, keepdims=True)
            p_own = jnp.exp(s_own - m_own)
            l_own = jnp.sum(p_own, axis=1, keepdims=True)[:tq]
            acc_own = _dot(p_own.astype(BF16), vn)[:tq]
            m_own = m_own[:tq]

            picked = [sel[:, n2:n2 + 1] > 0.0 for n2 in range(n_past)]
            m_fin = jnp.broadcast_to(m_own, (tq, hd))
            for n2 in range(n_past):
                m_fin = jnp.maximum(m_fin, jnp.where(picked[n2], m_ref[n2, rs, :], NEG))
            w_own = jnp.exp(m_own - m_fin)
            l_fin = w_own * l_own
            out = w_own * acc_own
            for n2 in range(n_past):
                w = jnp.where(picked[n2], jnp.exp(m_ref[n2, rs, :] - m_fin), 0.0)
                l_fin += w * l_ref[n2, rs, :]
                out += w * acc_ref[n2, rs, :]
            o_ref[:, hs] = out / l_fin


def _attention_kernel(pt_ref, slopes_ref, qn_ref, *refs, n_pages_step, decode_cfg, n_heads, tiles_per_step,
                      steps_per_head, n_work):
    del pt_ref
    k_refs, v_refs = refs[:n_pages_step], refs[n_pages_step:2 * n_pages_step]
    (kn_ref, vn_ref, qp_ref, kp_ref, vp_ref, od_ref, op_ref,
     qflat_ref, mb_ref, ksum_ref, m_ref, l_ref, acc_ref, kbf_ref, vaug_ref, sel_ref, alibi_ref) = refs[2 * n_pages_step:]
    step, n_steps = pl.program_id(1), pl.num_programs(1)
    _decode_attention(step, n_steps, qn_ref, k_refs, v_refs, kn_ref, vn_ref, od_ref,
                      qflat_ref, mb_ref, ksum_ref, m_ref, l_ref, acc_ref, **decode_cfg)

    g = pl.program_id(0) * n_steps + step
    phase = g % steps_per_head
    slope = slopes_ref[(g // steps_per_head) % n_heads]
    for ph in range(steps_per_head):
        @pl.when((g < n_work) & (phase == ph))
        def _(ph=ph):
            tiles = range(ph * tiles_per_step, (ph + 1) * tiles_per_step)
            _prompt_attention(slope, tiles, ph == 0, qp_ref, kp_ref, vp_ref, op_ref,
                              kbf_ref, vaug_ref, sel_ref, alibi_ref)


def _attention(q_p, k_p, v_p, slopes, batch_p, t, q_new, k_new, v_new, cache_k, cache_v, page_table,
               batch_d, tq, n_heads, hd):
    _, n_phys, page, _, _ = cache_k.shape
    width = n_heads * hd
    assert 2 * page == MOBA_BLOCK, "a MoBA block must span exactly two cache pages"
    assert tq <= PACKED_ROWS
    n_pages = page_table.shape[1]
    n_past = n_pages // 2
    bps = math.gcd(n_past, DECODE_BLOCKS_PER_STEP)
    n_steps = n_past // bps
    assert n_past <= LANES
    slopes_py = tuple(2.0 ** (-8.0 * (h + 1) / n_heads) for h in range(n_heads))

    nb = t // MOBA_BLOCK
    heads_p = batch_p * n_heads
    total_steps = batch_d * n_steps
    tiles_per_step = next(c for c in range(1, nb + 1) if nb % c == 0 and heads_p * (nb // c) <= total_steps)
    steps_per_head = nb // tiles_per_step
    n_work = heads_p * steps_per_head

    ck = cache_k.reshape(n_phys, page * n_heads, hd)
    cv = cache_v.reshape(n_phys, page * n_heads, hd)

    def page_spec(p):
        return pl.BlockSpec((None, page * n_heads, hd),
                            lambda b, s, pt, sl: (pt[b * n_pages + 2 * bps * s + p], 0, 0))

    def head_index(b, s, pt, sl):
        bh = jnp.minimum((b * n_steps + s) // steps_per_head, heads_p - 1)
        return (bh // n_heads, bh % n_heads)

    new_spec = pl.BlockSpec((tq, width), lambda b, s, pt, sl: (b, 0))
    head_spec = pl.BlockSpec((t, hd), head_index)
    pages = [page_spec(p) for p in range(2 * bps)]
    rows = n_heads * tq
    per_block = (n_past, rows, hd)
    grid_spec = pltpu.PrefetchScalarGridSpec(
        num_scalar_prefetch=2,
        grid=(batch_d, n_steps),
        in_specs=[new_spec] + pages + pages + [new_spec, new_spec, head_spec, head_spec, head_spec],
        out_specs=[new_spec, head_spec],
        scratch_shapes=[pltpu.VMEM((rows, hd), F32), pltpu.VMEM((n_heads // 2, 2 * tq, 2 * MOBA_BLOCK), F32),
                        pltpu.VMEM((LANES * n_heads, hd), F32), pltpu.VMEM(per_block, F32),
                        pltpu.VMEM(per_block, F32), pltpu.VMEM(per_block, F32),
                        pltpu.VMEM((t, hd), BF16), pltpu.VMEM((t, 2 * hd), BF16), pltpu.VMEM((t, LANES), BF16),
                        pltpu.VMEM((MOBA_BLOCK, t), F32)],
    )
    decode_cfg = dict(n_heads=n_heads, hd=hd, past_len=n_pages * page, bps=bps, slopes=slopes_py)
    att_d, att_p = pl.pallas_call(
        functools.partial(_attention_kernel, n_pages_step=2 * bps, decode_cfg=decode_cfg, n_heads=n_heads,
                          tiles_per_step=tiles_per_step, steps_per_head=steps_per_head, n_work=n_work),
        grid_spec=grid_spec,
        out_shape=[jax.ShapeDtypeStruct((batch_d * tq, width), F32),
                   jax.ShapeDtypeStruct((batch_p * t, width), BF16)],
        compiler_params=_cparams(("arbitrary", "arbitrary")),
        name="attention",
    )(page_table.reshape(-1), slopes, q_new, *([ck] * (2 * bps)), *([cv] * (2 * bps)), k_new, v_new, q_p, k_p, v_p)
    return att_p, att_d


def _mlstm_kernel(q_ref, k_ref, v_ref, og_ref, gcol_ref, grow_ref, c0_ref, n0_ref, m0_ref, gain_ref, *rest,
                  chunk, hps, n_cast, cast_steps):
    cast_in, (hm_ref, c_ref, n_ref, m_ref), cast_out = rest[:n_cast], rest[n_cast:n_cast + 4], rest[n_cast + 4:]
    if n_cast:
        @pl.when(pl.program_id(0) * pl.num_programs(1) + pl.program_id(1) < cast_steps)
        def _():
            for src_ref, dst_ref in zip(cast_in, cast_out):
                dst_ref[...] = src_ref[...].astype(dst_ref.dtype)

    t = q_ref.shape[0]
    d = q_ref.shape[1] // hps
    lc = chunk
    c_ref[...] = c0_ref[...]
    n_ref[...] = n0_ref[...]
    m_ref[...] = m0_ref[...]
    r = lax.broadcasted_iota(jnp.int32, (lc, lc), 0)
    c = lax.broadcasted_iota(jnp.int32, (lc, lc), 1)
    lower = r >= c

    def one_chunk(hh, rows):
        cols = slice(hh * d, (hh + 1) * d)
        q = q_ref[rows, cols].astype(F32)
        k = k_ref[rows, cols].astype(F32) * (d ** -0.5)
        v = v_ref[rows, cols].astype(F32)
        gc = gcol_ref[hh, rows, :]
        ig_c, lf_c = gc[:, 0:1], gc[:, 1:2]
        gr = grow_ref[hh, :, rows]
        ig_r, lf_r = gr[0:1, :], gr[1:2, :]
        cmat, nvec, m0 = c_ref[hh], n_ref[hh], m_ref[hh]

        b_c = jnp.sum(jnp.where(lower, lf_r, 0.0), axis=1, keepdims=True)
        b_r = jnp.sum(jnp.where(r <= c, lf_c, 0.0), axis=0, keepdims=True)
        dmat = jnp.where(lower, b_c - b_r + ig_r, NEG)
        inter = b_c + m0
        mt = jnp.maximum(inter, jnp.max(dmat, axis=1, keepdims=True))
        w = jnp.exp(dmat - mt)
        a = jnp.exp(inter - mt)
        qb, kb, vb = q.astype(BF16), k.astype(BF16), v.astype(BF16)
        sc = _dot_nt(qb, kb) * w
        num = a * _dot_nt(qb, cmat.astype(BF16)) + _dot(sc.astype(BF16), vb)
        den = a * jnp.sum(q * nvec, axis=1, keepdims=True) + jnp.sum(sc, axis=1, keepdims=True)
        h = num / jnp.maximum(jnp.abs(den), jnp.exp(-mt))
        hn = _rms(h) * gain_ref[hh]
        hm_ref[rows, cols] = (hn * _sigmoid(og_ref[rows, cols].astype(F32))).astype(hm_ref.dtype)

        b_last = b_c[lc - 1:lc, :]
        g = b_last - b_c + ig_c
        m_new = jnp.maximum(b_last + m0, jnp.max(g, axis=0, keepdims=True))
        ws = jnp.exp(g - m_new)
        a_last = jnp.exp(b_last + m0 - m_new)
        c_ref[hh] = a_last * cmat + _dot_tn((v * ws).astype(BF16), kb)
        n_ref[hh] = a_last * nvec + jnp.sum(ws * k, axis=0, keepdims=True)
        m_ref[hh] = m_new

    for hh in range(hps):
        if t == lc:
            one_chunk(hh, pl.ds(0, lc))
        else:
            def body(ci, carry, hh=hh):
                one_chunk(hh, pl.ds(pl.multiple_of(ci * lc, lc), lc))
                return carry

            lax.fori_loop(0, t // lc, body, 0)


def _mlstm(proj, gcol, grow, c0, n0, m0, gain, batch, t, n_heads, d, col0, hps, out_dtype, to_bf16=()):
    lc = math.gcd(t, MOBA_BLOCK)
    wb = hps * d
    assert col0 % wb == 0 and n_heads % hps == 0
    h_steps = n_heads // hps
    n_steps = batch * h_steps

    def col(off):
        return pl.BlockSpec((t, wb), lambda b, h: (b, (col0 + off * n_heads * d) // wb + h))

    lead = to_bf16[0].shape[0] if to_bf16 else 0
    per_step = -(-lead // n_steps) if lead else 1
    assert all(w.shape[0] == lead for w in to_bf16) and lead % per_step == 0
    cast_steps = lead // per_step

    def slab(w):
        return pl.BlockSpec((per_step,) + w.shape[1:],
                            lambda b, h: (jnp.minimum(b * h_steps + h, cast_steps - 1),) + (0,) * (w.ndim - 1))

    per_head = lambda shape: pl.BlockSpec((None, hps) + shape, lambda b, h: (b, h, 0, 0))
    outs = pl.pallas_call(
        functools.partial(_mlstm_kernel, chunk=lc, hps=hps, n_cast=len(to_bf16), cast_steps=cast_steps),
        grid=(batch, h_steps),
        in_specs=[col(0), col(1), col(2), col(3), per_head((t, 2)), per_head((2, t)),
                  per_head((d, d)), per_head((1, d)), per_head((1, 1)),
                  pl.BlockSpec((hps, 1, d), lambda b, h: (h, 0, 0))] + [slab(w) for w in to_bf16],
        out_specs=[pl.BlockSpec((t, wb), lambda b, h: (b, h)), per_head((d, d)), per_head((1, d)),
                   per_head((1, 1))] + [slab(w) for w in to_bf16],
        out_shape=[jax.ShapeDtypeStruct((batch * t, n_heads * d), out_dtype),
                   jax.ShapeDtypeStruct((batch, n_heads, d, d), F32),
                   jax.ShapeDtypeStruct((batch, n_heads, 1, d), F32),
                   jax.ShapeDtypeStruct((batch, n_heads, 1, 1), F32)]
                  + [jax.ShapeDtypeStruct(w.shape, BF16) for w in to_bf16],
        compiler_params=_cparams(("arbitrary", "arbitrary")),
        name=f"mlstm_t{t}",
    )(proj, proj, proj, proj, gcol, grow, c0, n0, m0, gain, *to_bf16)
    return outs[:4], outs[4:]


def _route(logits, n_groups, n_experts):
    epg = n_experts // n_groups
    lane = lax.broadcasted_iota(jnp.int32, logits.shape, 1).astype(F32)
    big = float(LANES)

    def rmax(x):
        return jnp.max(x, axis=1, keepdims=True)

    def first_lane(mask):
        return jnp.min(jnp.where(mask, lane, big), axis=1, keepdims=True)

    is_g = (lane >= n_experts) & (lane < n_experts + n_groups)
    gmax = rmax(jnp.where(is_g, logits, -jnp.inf))
    g_w = 1.0 / jnp.sum(jnp.where(is_g, jnp.exp(logits - gmax), 0.0), axis=1, keepdims=True)
    group = first_lane(is_g & (logits == gmax)) - n_experts
    lo = group * epg
    in_grp = (lane >= lo) & (lane < lo + epg)
    emax = rmax(jnp.where(in_grp, logits, -jnp.inf))
    pe = jnp.where(in_grp, jnp.exp(logits - emax), 0.0)
    prob = pe / jnp.sum(pe, axis=1, keepdims=True)
    p1 = rmax(jnp.where(in_grp, prob, -1.0))
    i1 = first_lane(in_grp & (prob == p1))
    rest = in_grp & (lane != i1)
    p2 = rmax(jnp.where(rest, prob, -1.0))
    i2 = first_lane(rest & (prob == p2))
    den = p1 + p2
    gate = g_w * jnp.where(lane == i1, p1 / den, jnp.where(lane == i2, p2 / den, 0.0))
    return gate, group


def _outproj_kernel(n_p, att_w, n_groups, n_experts,
                    attp, atts, hmp, hms, xp, xs, g1p, g1s, sh2p, sh2s, sc2p, sc2s,
                    wo_ref, gffn_ref, wr_ref, br_ref, x1_ref, h2g_ref, rinfo_ref, counts_ref, carry_ref):
    d = x1_ref.shape[1]

    @pl.when(pl.program_id(0) == 0)
    def _():
        carry_ref[...] = jnp.zeros_like(carry_ref)

    def body(att_ref, hm_ref, x_ref, g1_ref, sh2_ref, sc2_ref):
        mixed = (_dot(att_ref[...].astype(BF16), wo_ref[0:att_w, :])
                 + _dot(hm_ref[...].astype(BF16), wo_ref[att_w:, :]))
        x1 = x_ref[...] + g1_ref[...] * mixed
        x1_ref[...] = x1
        h2 = _rms(x1) * gffn_ref[...] * (1.0 + sc2_ref[...]) + sh2_ref[...]
        gate, group = _route(_dot3(h2, wr_ref[...]) + br_ref[...], n_groups, n_experts)
        h2g_ref[:, :d] = h2
        h2g_ref[:, d:] = gate
        tm = h2.shape[0]
        lane = lax.broadcasted_iota(jnp.int32, (tm, LANES), 1).astype(F32)
        onehot = (lane == group).astype(F32)
        r = lax.broadcasted_iota(jnp.int32, (tm, tm), 0)
        c = lax.broadcasted_iota(jnp.int32, (tm, tm), 1)
        before = _dot((r > c).astype(BF16), onehot.astype(BF16)) + carry_ref[...]
        rank = jnp.sum(onehot * before, axis=1, keepdims=True)
        rinfo_ref[...] = jnp.where(lane == 0.0, group, jnp.where(lane == 1.0, rank, 0.0))
        carry_ref[...] += jnp.sum(onehot, axis=0, keepdims=True)
        counts_ref[...] = carry_ref[...]

    _dual(n_p, body, (attp, hmp, xp, g1p, sh2p, sc2p), (atts, hms, xs, g1s, sh2s, sc2s))


def _outproj_stage(att_p, att_s, hm_p, hm_s, x_p, x_s, g1_p, g1_s, sh2_p, sh2_s, sc2_p, sc2_s,
                   w_out_bf, g_ffn, w_route, b_route, n_groups, n_experts, t_p):
    tm = TOKEN_TILE
    d = x_p.shape[1]
    att_w = att_p.shape[1]
    ml_w = hm_p.shape[1]
    n_p, n_s = x_p.shape[0] // tm, x_s.shape[0] // tm
    n_tok = x_p.shape[0] + x_s.shape[0]
    prow = lambda w: pl.BlockSpec((tm, w), _p_rows(n_p))
    srow = lambda w: pl.BlockSpec((tm, w), _s_rows(n_p))
    pbat = pl.BlockSpec((None, 1, d), _p_batch(n_p, t_p // tm))
    const = lambda shape: pl.BlockSpec(shape, lambda i: (0,) * len(shape))
    out_row = lambda w: pl.BlockSpec((tm, w), lambda i: (i, 0))
    return pl.pallas_call(
        functools.partial(_outproj_kernel, n_p, att_w, n_groups, n_experts),
        grid=(n_p + n_s,),
        in_specs=[prow(att_w), srow(att_w), prow(ml_w), srow(ml_w), prow(d), srow(d),
                  pbat, srow(d), pbat, srow(d), pbat, srow(d),
                  const((att_w + ml_w, d)), const((1, d)), const((d, LANES)), const((1, LANES))],
        out_specs=[out_row(d), out_row(d + LANES), out_row(LANES), const((1, LANES))],
        out_shape=[jax.ShapeDtypeStruct((n_tok, d), F32), jax.ShapeDtypeStruct((n_tok, d + LANES), F32),
                   jax.ShapeDtypeStruct((n_tok, LANES), F32), jax.ShapeDtypeStruct((1, LANES), F32)],
        scratch_shapes=[pltpu.VMEM((1, LANES), F32)],
        compiler_params=_cparams(("arbitrary",)),
        name="out_projection_router",
    )(att_p, att_s, hm_p, hm_s, x_p, x_s, g1_p, g1_s, sh2_p, sh2_s, sc2_p, sc2_s,
      w_out_bf, g_ffn, w_route, b_route)


MOE_TILE = 512
ROW_DMA_UNROLL = 8


class _RowGather:
    def __init__(self, idx_ref, src_hbm, buf_ref, sem):
        self.idx_ref, self.src_hbm, self.buf_ref, self.sem = idx_ref, src_hbm, buf_ref, sem
        self.tm = buf_ref.shape[1]

    def _copy(self, row_index, r, slot):
        return pltpu.make_async_copy(self.src_hbm.at[pl.ds(row_index, 1)],
                                     self.buf_ref.at[slot, pl.ds(r, 1)], self.sem.at[slot])

    def start(self, tile, slot):
        def body(r, carry):
            self._copy(self.idx_ref[tile * self.tm + r], r, slot).start()
            return carry

        lax.fori_loop(0, self.tm, body, 0, unroll=ROW_DMA_UNROLL)

    def start_inline(self, tile, slot, first_row, count):
        for u in range(count):
            r = first_row + u
            self._copy(self.idx_ref[tile * self.tm + r], r, slot).start()

    def wait(self, slot):
        def body(r, carry):
            self._copy(0, r, slot).wait()
            return carry

        lax.fori_loop(0, self.tm, body, 0, unroll=ROW_DMA_UNROLL)


def _moe_kernel(tg_ref, nu_ref, src_ref, h2g_hbm, wg_ref, wu_ref, wd_ref, o_ref, hbuf_ref, sem, *, epg):
    t = pl.program_id(0)
    n_tiles = pl.num_programs(0)
    tm, d = o_ref.shape
    n_used = nu_ref[0]
    used = t < n_used
    slot = t % 2
    rows = _RowGather(src_ref, h2g_hbm, hbuf_ref, sem)

    @pl.when(t == 0)
    def _():
        rows.start(0, 0)

    @pl.when(t <= n_used)
    def _():
        rows.wait(slot)

    @pl.when(jnp.logical_not(used))
    def _():
        o_ref[...] = jnp.zeros_like(o_ref)

    @pl.when(used)
    def _():
        rows.start_inline(jnp.minimum(t + 1, n_tiles - 1), 1 - slot, 0, tm)
        h2 = hbuf_ref[slot, :, :d].astype(BF16)
        gate = hbuf_ref[slot, :, d:]
        lane = lax.broadcasted_iota(jnp.int32, gate.shape, 1)
        for j in range(epg):
            hg = _dot(h2, wg_ref[j])
            hu = _dot(h2, wu_ref[j])
            ge = jnp.sum(jnp.where(lane == tg_ref[t] * epg + j, gate, 0.0), axis=1, keepdims=True)
            y = _dot((hg * _sigmoid(hg) * hu * ge).astype(BF16), wd_ref[j])
            if j == 0:
                o_ref[...] = y
            else:
                o_ref[...] += y

    @pl.when(used & (t == n_tiles - 1))
    def _():
        rows.wait(1 - slot)


def _moe_stage(h2g, rinfo, counts, w_gate_bf, w_up_bf, w_down_bf, n_groups):
    n_tok, dw = h2g.shape
    d = dw - LANES
    n_experts, _, ff = w_gate_bf.shape
    epg = n_experts // n_groups
    tm = MOE_TILE
    n_tiles = (n_tok + n_groups * (tm - 1)) // tm

    group = rinfo[:, 0].astype(jnp.int32)
    rank = rinfo[:, 1].astype(jnp.int32)
    cnt = counts[0, :n_groups].astype(jnp.int32)
    tiles_g = (cnt + tm - 1) // tm
    tile_end = jnp.cumsum(tiles_g)
    row_start = (tile_end - tiles_g) * tm
    dest = row_start[group] + rank
    tile_ids = jnp.arange(n_tiles, dtype=jnp.int32)
    tile_group = jnp.minimum(jnp.sum(tile_ids[:, None] >= tile_end[None, :], axis=1), n_groups - 1).astype(jnp.int32)
    n_used = tile_end[-1:].astype(jnp.int32)
    src = jnp.zeros((n_tiles * tm,), jnp.int32).at[dest].set(jnp.arange(n_tok, dtype=jnp.int32))

    def w_spec(rows_, cols_):
        return pl.BlockSpec((None, epg, rows_, cols_),
                            lambda t, tg, nu, sr: (jnp.where(t < nu[0], tg[t], n_groups - 1), 0, 0, 0),
                            pipeline_mode=pl.Buffered(1))

    by_group = lambda w: w.reshape((n_groups, epg) + w.shape[1:])
    grid_spec = pltpu.PrefetchScalarGridSpec(
        num_scalar_prefetch=3,
        grid=(n_tiles,),
        in_specs=[pl.BlockSpec(memory_space=pl.ANY), w_spec(d, ff), w_spec(d, ff), w_spec(ff, d)],
        out_specs=pl.BlockSpec((tm, d), lambda t, tg, nu, sr: (t, 0)),
        scratch_shapes=[pltpu.VMEM((2, tm, dw), F32), pltpu.SemaphoreType.DMA((2,))],
    )
    ys = pl.pallas_call(
        functools.partial(_moe_kernel, epg=epg),
        grid_spec=grid_spec,
        out_shape=jax.ShapeDtypeStruct((n_tiles * tm, d), F32),
        compiler_params=_cparams(("arbitrary",)),
        name="moe_experts",
    )(tile_group, n_used, src, h2g, by_group(w_gate_bf), by_group(w_up_bf), by_group(w_down_bf))
    return ys, dest


def _final_kernel(n_p, dest_ref, x1_ref, moe_hbm, g2p, g2s, gfin_ref, yp_ref, ys_ref, mbuf_ref, sem):
    i = pl.program_id(0)
    slot = i % 2
    rows = _RowGather(dest_ref, moe_hbm, mbuf_ref, sem)

    @pl.when(i == 0)
    def _():
        rows.start(0, 0)

    rows.wait(slot)
    n = pl.num_programs(0)

    def body(g2_ref, y_ref):
        rows.start_inline(jnp.minimum(i + 1, n - 1), 1 - slot, 0, rows.tm)
        x2 = x1_ref[...] + g2_ref[...] * mbuf_ref[slot]
        y_ref[...] = _rms(x2) * gfin_ref[...]

    _dual(n_p, body, (g2p, yp_ref), (g2s, ys_ref))

    @pl.when(i == n - 1)
    def _():
        rows.wait(1 - slot)


def _final_stage(x1, moe_rows, dest, g2_p, g2_s, g_final, n_tok_p, t_p):
    tm = TOKEN_TILE
    n_tok, d = x1.shape
    n_p = n_tok_p // tm
    n_s = (n_tok - n_tok_p) // tm
    grid_spec = pltpu.PrefetchScalarGridSpec(
        num_scalar_prefetch=1,
        grid=(n_p + n_s,),
        in_specs=[pl.BlockSpec((tm, d), lambda i, ds: (i, 0)), pl.BlockSpec(memory_space=pl.ANY),
                  pl.BlockSpec((None, 1, d), lambda i, ds: _p_batch(n_p, t_p // tm)(i)),
                  pl.BlockSpec((tm, d), lambda i, ds: _s_rows(n_p)(i)), pl.BlockSpec((1, d), lambda i, ds: (0, 0))],
        out_specs=[pl.BlockSpec((tm, d), lambda i, ds: _p_rows(n_p)(i)),
                   pl.BlockSpec((tm, d), lambda i, ds: _s_rows(n_p)(i))],
        scratch_shapes=[pltpu.VMEM((2, tm, d), F32), pltpu.SemaphoreType.DMA((2,))],
    )
    return pl.pallas_call(
        functools.partial(_final_kernel, n_p),
        grid_spec=grid_spec,
        out_shape=[jax.ShapeDtypeStruct((n_tok_p, d), F32), jax.ShapeDtypeStruct((n_tok - n_tok_p, d), F32)],
        compiler_params=_cparams(("arbitrary",)),
        name="final_norm",
    )(dest, x1, moe_rows, g2_p, g2_s, g_final)


def kernel(x_prompt, x_sample, cache_k, cache_v, page_table, state_C, state_n, state_m, c_prompt, c_sample,
           w_in, b_ig, b_fg, ml_gain, w_out, g_mix, g_ffn, w_mod, b_mod, w_grp, b_grp, w_exp, b_exp,
           w_gate, w_up, w_down, g_final):
    bp, tp, d = x_prompt.shape
    bd, td, _ = x_sample.shape
    depth = w_in.shape[0]
    assert depth == 1, "one trunk layer"
    n_heads, hd = cache_k.shape[-2:]
    att_w = n_heads * hd
    nh_ml, d_ml = state_n.shape[-2:]
    ml_w = nh_ml * d_ml
    assert att_w == ml_w and att_w + ml_w == d
    n_groups, n_experts = w_grp.shape[-1], w_exp.shape[-1]
    assert n_experts + n_groups <= LANES and 2 * nh_ml <= LANES
    main_w = 3 * att_w + 4 * ml_w
    ntp, nts = bp * tp, bd * td

    c_rows = bp + bd
    c_pad = -c_rows % 16
    c_all = jnp.concatenate([c_prompt, c_sample, jnp.zeros((c_pad, d), F32)], axis=0)
    mod = _modulation(c_all, w_mod[0], b_mod[0])
    mod_p = mod[:bp].reshape(bp, N_MOD, 1, d)
    mod_s = jnp.repeat(mod[bp:c_rows].reshape(bd, N_MOD, d), td, axis=0)
    shift1_p, scale1_p, gate1_p, shift2_p, scale2_p, gate2_p = (mod_p[:, i] for i in range(N_MOD))
    shift1_s, scale1_s, gate1_s, shift2_s, scale2_s, gate2_s = (mod_s[:, i] for i in range(N_MOD))

    x_p = x_prompt.reshape(ntp, d)
    x_s = x_sample.reshape(nts, d)

    b_gates = jnp.pad(jnp.concatenate([b_ig[0], b_fg[0]]), (0, LANES - 2 * nh_ml)).reshape(1, LANES)
    w_in_t = w_in[0].T
    h_all, gates = _norm_stage(x_p, x_s, shift1_p, shift1_s, scale1_p, scale1_s, g_mix[0].reshape(1, d),
                               w_in_t, main_w, b_gates, nh_ml, tp)

    rest_p, k_p, v_p = _in_projection(h_all, w_in_t, 0, ntp, main_w, att_w, BF16, "in_projection_prompt")
    rest_s, k_s, v_s = _in_projection(h_all, w_in_t, ntp, nts, main_w, att_w, F32, "in_projection_decode")

    slopes = 2.0 ** (-8.0 * jnp.arange(1, n_heads + 1, dtype=F32) / n_heads)
    att_p, att_s = _attention(rest_p, k_p, v_p, slopes, bp, tp, rest_s, k_s, v_s, cache_k, cache_v, page_table,
                              bd, td, n_heads, hd)

    def gate_layouts(g, b, t):
        g = g[:, :2 * nh_ml].reshape(b, t, 2, nh_ml)
        return g.transpose(0, 3, 1, 2), g.transpose(0, 3, 2, 1)

    gcol_p, grow_p = gate_layouts(gates[:ntp], bp, tp)
    gcol_s, grow_s = gate_layouts(gates[ntp:], bd, td)
    gain = ml_gain[0].reshape(nh_ml, 1, d_ml)
    zeros = lambda *s: jnp.zeros(s, F32)
    (hm_p, c_p, n_p, m_p), expert_w = _mlstm(
        rest_p, gcol_p, grow_p, zeros(bp, nh_ml, d_ml, d_ml), zeros(bp, nh_ml, 1, d_ml), zeros(bp, nh_ml, 1, 1),
        gain, bp, tp, nh_ml, d_ml, att_w, 1, BF16, to_bf16=(w_gate[0], w_up[0], w_down[0]))
    (hm_s, c_s, n_s, m_s), _ = _mlstm(
        rest_s, gcol_s, grow_s, state_C[0], state_n[0].reshape(bd, nh_ml, 1, d_ml),
        state_m[0].reshape(bd, nh_ml, 1, 1), gain, bd, td, nh_ml, d_ml, att_w, nh_ml, F32)

    w_route = jnp.pad(jnp.concatenate([w_exp[0], w_grp[0]], axis=1), ((0, 0), (0, LANES - n_experts - n_groups)))
    b_route = jnp.pad(jnp.concatenate([b_exp[0], b_grp[0]]), (0, LANES - n_experts - n_groups)).reshape(1, LANES)
    x1, h2g, rinfo, counts = _outproj_stage(att_p, att_s, hm_p, hm_s, x_p, x_s, gate1_p, gate1_s, shift2_p,
                                            shift2_s, scale2_p, scale2_s, w_out[0].astype(BF16),
                                            g_ffn[0].reshape(1, d), w_route, b_route, n_groups, n_experts, tp)

    moe_rows, dest = _moe_stage(h2g, rinfo, counts, *expert_w, n_groups)
    y_p, y_s = _final_stage(x1, moe_rows, dest, gate2_p, gate2_s, g_final.reshape(1, d), ntp, tp)

    kv_shape_p = (1, bp, tp, n_heads, hd)
    kv_shape_s = (1, bd, td, n_heads, hd)
    return (y_p.reshape(bp, tp, d), y_s.reshape(bd, td, d),
            k_p.reshape(kv_shape_p), v_p.reshape(kv_shape_p), k_s.reshape(kv_shape_s), v_s.reshape(kv_shape_s),
            c_p[None], n_p.reshape(1, bp, nh_ml, d_ml), m_p.reshape(1, bp, nh_ml),
            c_s[None], n_s.reshape(1, bd, nh_ml, d_ml), m_s.reshape(1, bd, nh_ml))
```

```python
import functools
import math

import jax
import jax.numpy as jnp
from jax import lax
from jax.experimental import pallas as pl
from jax.experimental.pallas import tpu as pltpu

F32 = jnp.float32
BF16 = jnp.bfloat16

MOBA_BLOCK = 256
MOBA_TOPK = 3
EXPERT_TOPK = 2
N_MOD = 6
EPS = 1e-6
NEG = -1e30

LANES = 128
V7X_VMEM_LIMIT = 56 * 1024 * 1024
TOKEN_TILE = 256
OUTPROJ_TILE = 512


def _cparams(sem):
    return pltpu.CompilerParams(dimension_semantics=sem, vmem_limit_bytes=V7X_VMEM_LIMIT)


def _pick_tile(n, candidates):
    for c in candidates:
        if n % c == 0:
            return c
    raise ValueError(f"no tile in {candidates} divides {n}")


def _dot(a, b):
    return jnp.dot(a, b, preferred_element_type=F32)


def _dot_nt(a, b):
    return lax.dot_general(a, b, (((1,), (1,)), ((), ())), preferred_element_type=F32)


def _dot_tn(a, b):
    return lax.dot_general(a, b, (((0,), (0,)), ((), ())), preferred_element_type=F32)


def _split(x):
    hi = x.astype(BF16)
    lo = (x - hi.astype(F32)).astype(BF16)
    return hi, lo


def _dot3(a, b, dot=_dot):
    a_hi, a_lo = _split(a)
    b_hi, b_lo = _split(b)
    return dot(a_hi, b_hi) + dot(a_lo, b_hi) + dot(a_hi, b_lo)


def _sigmoid(x):
    return 1.0 / (1.0 + jnp.exp(-x))


def _log_sigmoid(x):
    return jnp.minimum(x, 0.0) - jnp.log(1.0 + jnp.exp(-jnp.abs(x)))


def _rms(x):
    return x * lax.rsqrt(jnp.mean(x * x, axis=-1, keepdims=True) + EPS)


def _mod_kernel(c_ref, w_ref, b_ref, o_ref):
    o_ref[...] = _dot3(c_ref[...], w_ref[...]) + b_ref[...]


def _modulation(c_all, w_mod, b_mod):
    rows, d = c_all.shape
    n = w_mod.shape[1]
    tn = _pick_tile(n, (1024, 512, 256, 128))
    return pl.pallas_call(
        _mod_kernel,
        grid=(n // tn,),
        in_specs=[
            pl.BlockSpec((rows, d), lambda j: (0, 0)),
            pl.BlockSpec((d, tn), lambda j: (0, j)),
            pl.BlockSpec((1, tn), lambda j: (0, j)),
        ],
        out_specs=pl.BlockSpec((rows, tn), lambda j: (0, j)),
        out_shape=jax.ShapeDtypeStruct((rows, n), F32),
        compiler_params=_cparams(("arbitrary",)),
        name="modulation",
    )(c_all, w_mod, b_mod.reshape(1, n))


def _p_rows(n_p):
    return lambda i: (jnp.minimum(i, n_p - 1), 0)


def _p_batch(n_p, tiles_per_batch):
    return lambda i: (jnp.minimum(i, n_p - 1) // tiles_per_batch, 0, 0)


def _s_rows(n_p):
    return lambda i: (jnp.maximum(i - n_p, 0), 0)


def _dual(n_p, body, p_refs, s_refs):
    i = pl.program_id(0)

    @pl.when(i < n_p)
    def _():
        body(*p_refs)

    @pl.when(i >= n_p)
    def _():
        body(*s_refs)


def _norm_kernel(n_p, n_heads, xp, xs, shp, shs, scp, scs, g_ref, wg_ref, bg_ref, h_ref, gates_ref):
    def body(x_ref, sh_ref, sc_ref):
        h = _rms(x_ref[...]) * g_ref[...] * (1.0 + sc_ref[...]) + sh_ref[...]
        h_ref[...] = h.astype(BF16)
        wg = wg_ref[...]
        wg = jnp.concatenate([wg, jnp.zeros((LANES - wg.shape[0], wg.shape[1]), F32)], axis=0)
        g = _dot3(h, wg, _dot_nt) + bg_ref[...]
        lane = lax.broadcasted_iota(jnp.int32, g.shape, 1)
        gates_ref[...] = jnp.where(lane < n_heads, g, _log_sigmoid(g))

    _dual(n_p, body, (xp, shp, scp), (xs, shs, scs))


def _norm_stage(x_p, x_s, sh_p, sh_s, sc_p, sc_s, g_mix, w_in_t, gate_row0, b_gates, n_heads, t_p):
    tm = TOKEN_TILE
    d = x_p.shape[1]
    n_p, n_s = x_p.shape[0] // tm, x_s.shape[0] // tm
    n_tok = x_p.shape[0] + x_s.shape[0]
    n_gate = 2 * n_heads
    assert gate_row0 % n_gate == 0 and w_in_t.shape[0] - gate_row0 == n_gate and n_gate % 8 == 0
    row = pl.BlockSpec((tm, d), _p_rows(n_p))
    srow = pl.BlockSpec((tm, d), _s_rows(n_p))
    pbat = pl.BlockSpec((None, 1, d), _p_batch(n_p, t_p // tm))
    const = lambda shape: pl.BlockSpec(shape, lambda i: (0,) * len(shape))
    w_gate_cols = pl.BlockSpec((n_gate, d), lambda i: (gate_row0 // n_gate, 0))
    return pl.pallas_call(
        functools.partial(_norm_kernel, n_p, n_heads),
        grid=(n_p + n_s,),
        in_specs=[row, srow, pbat, srow, pbat, srow, const((1, d)), w_gate_cols, const((1, LANES))],
        out_specs=[pl.BlockSpec((tm, d), lambda i: (i, 0)), pl.BlockSpec((tm, LANES), lambda i: (i, 0))],
        out_shape=[jax.ShapeDtypeStruct((n_tok, d), BF16), jax.ShapeDtypeStruct((n_tok, LANES), F32)],
        compiler_params=_cparams(("arbitrary",)),
        name="norm_gates",
    )(x_p, x_s, sh_p, sh_s, sc_p, sc_s, g_mix, w_in_t, b_gates)


INPROJ_COL_TILE = 1024


def _inproj_kernel(h_ref, wt_ref, rest_ref, k_ref, v_ref, wbf_ref, *, k_tiles, v_tiles):
    j = pl.program_id(0)

    @pl.when(pl.program_id(1) == 0)
    def _():
        wbf_ref[...] = wt_ref[...].T.astype(BF16)

    acc = _dot(h_ref[...], wbf_ref[...])
    is_k = (j >= k_tiles[0]) & (j < k_tiles[1])
    is_v = (j >= v_tiles[0]) & (j < v_tiles[1])

    @pl.when(is_k)
    def _():
        k_ref[...] = acc

    @pl.when(is_v)
    def _():
        v_ref[...] = acc

    @pl.when(jnp.logical_not(is_k | is_v))
    def _():
        rest_ref[...] = acc.astype(rest_ref.dtype)


def _in_projection(h_all, w_in_t, row0, n_rows, width, att_w, rest_dtype, name):
    d = h_all.shape[1]
    tn = math.gcd(att_w, INPROJ_COL_TILE)
    tm = _pick_tile(n_rows, (1024, 512, 256))
    n_i = n_rows // tm
    rb = row0 // tm
    assert row0 % tm == 0
    last = n_i - 1
    per = att_w // tn
    k_tiles, v_tiles = (per, 2 * per), (2 * per, 3 * per)

    def rest_index(j, i):
        skipping = (j >= k_tiles[0]) & (j < v_tiles[1])
        return (jnp.where(skipping, last, i),
                jnp.where(j < k_tiles[0], j, jnp.where(skipping, k_tiles[0] - 1, j - 2 * per)))

    def own_index(tiles):
        lo, hi = tiles
        return lambda j, i: (jnp.where(j < lo, 0, jnp.where(j < hi, i, last)),
                             jnp.where(j < lo, 0, jnp.where(j < hi, j - lo, hi - lo - 1)))

    return pl.pallas_call(
        functools.partial(_inproj_kernel, k_tiles=k_tiles, v_tiles=v_tiles),
        grid=(width // tn, n_i),
        in_specs=[pl.BlockSpec((tm, d), lambda j, i: (rb + i, 0)), pl.BlockSpec((tn, d), lambda j, i: (j, 0))],
        out_specs=[pl.BlockSpec((tm, tn), rest_index), pl.BlockSpec((tm, tn), own_index(k_tiles)),
                   pl.BlockSpec((tm, tn), own_index(v_tiles))],
        out_shape=[jax.ShapeDtypeStruct((n_rows, width - 2 * att_w), rest_dtype),
                   jax.ShapeDtypeStruct((n_rows, att_w), F32), jax.ShapeDtypeStruct((n_rows, att_w), F32)],
        scratch_shapes=[pltpu.VMEM((d, tn), BF16)],
        compiler_params=_cparams(("arbitrary", "arbitrary")),
        name=name,
    )(h_all, w_in_t)


def _prompt_attention(slope, tiles, first, q_ref, k_ref, v_ref, o_ref, kbf_ref, vaug_ref, sel_ref, alibi_ref):
    t, hd = q_ref.shape
    blk = MOBA_BLOCK
    nb = t // blk
    scale = hd ** -0.5

    if first:
        k = k_ref[...]
        kbf_ref[...] = (k * scale).astype(BF16)
        vaug_ref[:, :hd] = v_ref[...].astype(BF16)
        vaug_ref[:, hd:] = jnp.ones((t, hd), BF16)

        kmean = jnp.mean(k.reshape(nb, blk, hd), axis=1)
        gate = _dot3(kmean, q_ref[...].astype(F32), _dot_nt)
        kb = lax.broadcasted_iota(jnp.int32, (nb, t), 0)
        qb = lax.broadcasted_iota(jnp.int32, (nb, t), 1) // blk
        valid = kb < qb
        g = jnp.where(valid, gate, -jnp.inf)
        rank = jnp.zeros((nb, t), jnp.int32)
        for n2 in range(nb):
            gn = g[n2:n2 + 1, :]
            rank += ((gn > g) | ((gn == g) & (n2 < kb))).astype(jnp.int32)
        allowed = (valid & (rank < MOBA_TOPK)).astype(F32)
        sel_ref[...] = jnp.concatenate([allowed, jnp.zeros((LANES - nb, t), F32)], axis=0).T.astype(BF16)
        alibi_ref[...] = slope * (lax.broadcasted_iota(jnp.int32, (blk, t), 0)
                                  - lax.broadcasted_iota(jnp.int32, (blk, t), 1)).astype(F32)

    causal = (lax.broadcasted_iota(jnp.int32, (blk, blk), 0) >= lax.broadcasted_iota(jnp.int32, (blk, blk), 1))
    for i in tiles:
        w = (i + 1) * blk
        rows = pl.ds(i * blk, blk)
        s = _dot_nt(q_ref[rows, :].astype(BF16), kbf_ref[0:w, :]) - alibi_ref[:, 0:w]
        ok = causal
        if i > 0:
            expand = (lax.broadcasted_iota(jnp.int32, (LANES, i * blk), 0)
                      == lax.broadcasted_iota(jnp.int32, (LANES, i * blk), 1) // blk).astype(BF16)
            past_ok = _dot(sel_ref[rows, :], expand) > 0.5
            ok = jnp.concatenate([past_ok, causal], axis=1)
        s = jnp.where(ok, s, NEG)
        p = jnp.exp(s - jnp.max(s, axis=1, keepdims=True)).astype(BF16)
        out = _dot(p, vaug_ref[0:w, :])
        o_ref[rows, :] = (out[:, :hd] / out[:, hd:]).astype(o_ref.dtype)


DECODE_BLOCKS_PER_STEP = 8
PACKED_ROWS = 16


def _decode_attention(step, n_steps, q_ref, k_refs, v_refs, kn_ref, vn_ref, o_ref,
                      qflat_ref, mb_ref, ksum_ref, m_ref, l_ref, acc_ref, *, n_heads, hd, past_len, bps, slopes):
    tq = q_ref.shape[0]
    blk = MOBA_BLOCK
    scale = hd ** -0.5
    n_past = acc_ref.shape[0]
    rowpad = jnp.zeros((PACKED_ROWS - tq, hd), F32)
    page = blk // 2
    n_pairs = n_heads // 2
    pr = 2 * tq
    pk = 2 * blk
    pair_heads = [(g, g + n_pairs) for g in range(n_pairs)]

    def slope_col(g):
        second = lax.broadcasted_iota(jnp.int32, (pr, 1), 0) >= tq
        return jnp.where(second, slopes[pair_heads[g][1]], slopes[pair_heads[g][0]])

    @pl.when(step == 0)
    def _():
        qflat_ref[...] = jnp.concatenate(
            [q_ref[:, h * hd:(h + 1) * hd] for pair in pair_heads for h in pair], axis=0) * scale
        ksum_ref[...] = jnp.zeros_like(ksum_ref)
        r = lax.broadcasted_iota(jnp.int32, (pr, pk), 0)
        c = lax.broadcasted_iota(jnp.int32, (pr, pk), 1)
        in_block_dist = (r % tq - c // 2).astype(F32)
        for g in range(n_pairs):
            mb_ref[g] = jnp.where(c % 2 == r // tq, -slope_col(g) * in_block_dist, NEG)

    for j in range(bps):
        n = step * bps + j
        k0, k1 = k_refs[2 * j], k_refs[2 * j + 1]
        ksum_ref[pl.ds(pl.multiple_of(n * n_heads, n_heads), n_heads), :] = (
            jnp.sum(k0[...].reshape(page, n_heads, hd), axis=0) + jnp.sum(k1[...].reshape(page, n_heads, hd), axis=0))

    for g in range(n_pairs):
        pair_rows = pl.ds(g, 2 * page, stride=n_pairs)
        rs = slice(g * pr, (g + 1) * pr)
        kp = jnp.concatenate([kr[pair_rows, :] for kr in k_refs], axis=0).astype(BF16)
        s_all = _dot_nt(qflat_ref[rs, :].astype(BF16), kp)
        for j in range(bps):
            n = step * bps + j
            v0, v1 = v_refs[2 * j], v_refs[2 * j + 1]
            vp = jnp.concatenate([v0[pair_rows, :], v1[pair_rows, :]], axis=0).astype(BF16)
            s = s_all[:, j * pk:(j + 1) * pk] + mb_ref[g]
            m = jnp.max(s, axis=1, keepdims=True)
            p = jnp.exp(s - m)
            l = jnp.sum(p, axis=1, keepdims=True)
            acc_ref[n, rs, :] = _dot(p.astype(BF16), vp)
            block_dist = (past_len - n * blk).astype(F32)
            m_ref[n, rs, :] = jnp.broadcast_to(m - slope_col(g) * block_dist, (pr, hd))
            l_ref[n, rs, :] = jnp.broadcast_to(l, (pr, hd))

    @pl.when(step == n_steps - 1)
    def _():
        lane = lax.broadcasted_iota(jnp.int32, (tq, LANES), 1)
        valid = lane < n_past
        orow = lax.broadcasted_iota(jnp.int32, (PACKED_ROWS, PACKED_ROWS), 0)
        ocol = lax.broadcasted_iota(jnp.int32, (PACKED_ROWS, PACKED_ROWS), 1)
        d_own = (orow - ocol).astype(F32)
        for h in range(n_heads):
            hs = slice(h * hd, (h + 1) * hd)
            row0 = (h % n_pairs) * pr + (h // n_pairs) * tq
            rs = slice(row0, row0 + tq)
            qf = jnp.concatenate([qflat_ref[rs, :], rowpad], axis=0)
            kmean = ksum_ref[pl.ds(h, LANES, stride=n_heads), :] * (1.0 / blk)
            gate = _dot3(qf[:tq], kmean, _dot_nt)
            g = jnp.where(valid, gate, -jnp.inf)
            rank = jnp.zeros((tq, LANES), jnp.int32)
            for n2 in range(n_past):
                gn = g[:, n2:n2 + 1]
                rank += ((gn > g) | ((gn == g) & (n2 < lane))).astype(jnp.int32)
            sel = (valid & (rank < MOBA_TOPK)).astype(F32)

            kn = jnp.concatenate([kn_ref[:, hs], rowpad], axis=0).astype(BF16)
            vn = jnp.concatenate([vn_ref[:, hs], rowpad], axis=0).astype(BF16)
            s_own = jnp.where(d_own >= 0.0, _dot_nt(qf.astype(BF16), kn) - slopes[h] * d_own, NEG)
            m_own = jnp.max(s_own, axis=1, keepdims=True)
            p_own = jnp.exp(s_own - m_own)
            l_own = jnp.sum(p_own, axis=1, keepdims=True)[:tq]
            acc_own = _dot(p_own.astype(BF16), vn)[:tq]
            m_own = m_own[:tq]

            picked = [sel[:, n2:n2 + 1] > 0.0 for n2 in range(n_past)]
            m_fin = jnp.broadcast_to(m_own, (tq, hd))
            for n2 in range(n_past):
                m_fin = jnp.maximum(m_fin, jnp.where(picked[n2], m_ref[n2, rs, :], NEG))
            w_own = jnp.exp(m_own - m_fin)
            l_fin = w_own * l_own
            out = w_own * acc_own
            for n2 in range(n_past):
                w = jnp.where(picked[n2], jnp.exp(m_ref[n2, rs, :] - m_fin), 0.0)
                l_fin += w * l_ref[n2, rs, :]
                out += w * acc_ref[n2, rs, :]
            o_ref[:, hs] = out / l_fin


def _attention_kernel(pt_ref, slopes_ref, qn_ref, *refs, n_pages_step, decode_cfg, n_heads, tiles_per_step,
                      steps_per_head, n_work):
    del pt_ref
    k_refs, v_refs = refs[:n_pages_step], refs[n_pages_step:2 * n_pages_step]
    (kn_ref, vn_ref, qp_ref, kp_ref, vp_ref, od_ref, op_ref,
     qflat_ref, mb_ref, ksum_ref, m_ref, l_ref, acc_ref, kbf_ref, vaug_ref, sel_ref, alibi_ref) = refs[2 * n_pages_step:]
    step, n_steps = pl.program_id(1), pl.num_programs(1)
    _decode_attention(step, n_steps, qn_ref, k_refs, v_refs, kn_ref, vn_ref, od_ref,
                      qflat_ref, mb_ref, ksum_ref, m_ref, l_ref, acc_ref, **decode_cfg)

    g = pl.program_id(0) * n_steps + step
    phase = g % steps_per_head
    slope = slopes_ref[(g // steps_per_head) % n_heads]
    for ph in range(steps_per_head):
        @pl.when((g < n_work) & (phase == ph))
        def _(ph=ph):
            tiles = range(ph * tiles_per_step, (ph + 1) * tiles_per_step)
            _prompt_attention(slope, tiles, ph == 0, qp_ref, kp_ref, vp_ref, op_ref,
                              kbf_ref, vaug_ref, sel_ref, alibi_ref)


def _attention(q_p, k_p, v_p, slopes, batch_p, t, q_new, k_new, v_new, cache_k, cache_v, page_table,
               batch_d, tq, n_heads, hd):
    _, n_phys, page, _, _ = cache_k.shape
    width = n_heads * hd
    assert 2 * page == MOBA_BLOCK, "a MoBA block must span exactly two cache pages"
    assert tq <= PACKED_ROWS
    n_pages = page_table.shape[1]
    n_past = n_pages // 2
    bps = math.gcd(n_past, DECODE_BLOCKS_PER_STEP)
    n_steps = n_past // bps
    assert n_past <= LANES
    slopes_py = tuple(2.0 ** (-8.0 * (h + 1) / n_heads) for h in range(n_heads))

    nb = t // MOBA_BLOCK
    heads_p = batch_p * n_heads
    total_steps = batch_d * n_steps
    tiles_per_step = next(c for c in range(1, nb + 1) if nb % c == 0 and heads_p * (nb // c) <= total_steps)
    steps_per_head = nb // tiles_per_step
    n_work = heads_p * steps_per_head

    ck = cache_k.reshape(n_phys, page * n_heads, hd)
    cv = cache_v.reshape(n_phys, page * n_heads, hd)

    def page_spec(p):
        return pl.BlockSpec((None, page * n_heads, hd),
                            lambda b, s, pt, sl: (pt[b * n_pages + 2 * bps * s + p], 0, 0))

    def head_index(b, s, pt, sl):
        bh = jnp.minimum((b * n_steps + s) // steps_per_head, heads_p - 1)
        return (bh // n_heads, bh % n_heads)

    new_spec = pl.BlockSpec((tq, width), lambda b, s, pt, sl: (b, 0))
    head_spec = pl.BlockSpec((t, hd), head_index)
    pages = [page_spec(p) for p in range(2 * bps)]
    rows = n_heads * tq
    per_block = (n_past, rows, hd)
    grid_spec = pltpu.PrefetchScalarGridSpec(
        num_scalar_prefetch=2,
        grid=(batch_d, n_steps),
        in_specs=[new_spec] + pages + pages + [new_spec, new_spec, head_spec, head_spec, head_spec],
        out_specs=[new_spec, head_spec],
        scratch_shapes=[pltpu.VMEM((rows, hd), F32), pltpu.VMEM((n_heads // 2, 2 * tq, 2 * MOBA_BLOCK), F32),
                        pltpu.VMEM((LANES * n_heads, hd), F32), pltpu.VMEM(per_block, F32),
                        pltpu.VMEM(per_block, F32), pltpu.VMEM(per_block, F32),
                        pltpu.VMEM((t, hd), BF16), pltpu.VMEM((t, 2 * hd), BF16), pltpu.VMEM((t, LANES), BF16),
                        pltpu.VMEM((MOBA_BLOCK, t), F32)],
    )
    decode_cfg = dict(n_heads=n_heads, hd=hd, past_len=n_pages * page, bps=bps, slopes=slopes_py)
    att_d, att_p = pl.pallas_call(
        functools.partial(_attention_kernel, n_pages_step=2 * bps, decode_cfg=decode_cfg, n_heads=n_heads,
                          tiles_per_step=tiles_per_step, steps_per_head=steps_per_head, n_work=n_work),
        grid_spec=grid_spec,
        out_shape=[jax.ShapeDtypeStruct((batch_d * tq, width), F32),
                   jax.ShapeDtypeStruct((batch_p * t, width), BF16)],
        compiler_params=_cparams(("arbitrary", "arbitrary")),
        name="attention",
    )(page_table.reshape(-1), slopes, q_new, *([ck] * (2 * bps)), *([cv] * (2 * bps)), k_new, v_new, q_p, k_p, v_p)
    return att_p, att_d


def _mlstm_kernel(q_ref, k_ref, v_ref, og_ref, gcol_ref, grow_ref, c0_ref, n0_ref, m0_ref, gain_ref, *rest,
                  chunk, hps, n_cast, cast_steps):
    cast_in, (hm_ref, c_ref, n_ref, m_ref), cast_out = rest[:n_cast], rest[n_cast:n_cast + 4], rest[n_cast + 4:]
    if n_cast:
        @pl.when(pl.program_id(0) * pl.num_programs(1) + pl.program_id(1) < cast_steps)
        def _():
            for src_ref, dst_ref in zip(cast_in, cast_out):
                dst_ref[...] = src_ref[...].astype(dst_ref.dtype)

    t = q_ref.shape[0]
    d = q_ref.shape[1] // hps
    lc = chunk
    c_ref[...] = c0_ref[...]
    n_ref[...] = n0_ref[...]
    m_ref[...] = m0_ref[...]
    r = lax.broadcasted_iota(jnp.int32, (lc, lc), 0)
    c = lax.broadcasted_iota(jnp.int32, (lc, lc), 1)
    lower = r >= c

    def one_chunk(hh, rows):
        cols = slice(hh * d, (hh + 1) * d)
        q = q_ref[rows, cols].astype(F32)
        k = k_ref[rows, cols].astype(F32) * (d ** -0.5)
        v = v_ref[rows, cols].astype(F32)
        gc = gcol_ref[hh, rows, :]
        ig_c, lf_c = gc[:, 0:1], gc[:, 1:2]
        gr = grow_ref[hh, :, rows]
        ig_r, lf_r = gr[0:1, :], gr[1:2, :]
        cmat, nvec, m0 = c_ref[hh], n_ref[hh], m_ref[hh]

        b_c = jnp.sum(jnp.where(lower, lf_r, 0.0), axis=1, keepdims=True)
        b_r = jnp.sum(jnp.where(r <= c, lf_c, 0.0), axis=0, keepdims=True)
        dmat = jnp.where(lower, b_c - b_r + ig_r, NEG)
        inter = b_c + m0
        mt = jnp.maximum(inter, jnp.max(dmat, axis=1, keepdims=True))
        w = jnp.exp(dmat - mt)
        a = jnp.exp(inter - mt)
        qb, kb, vb = q.astype(BF16), k.astype(BF16), v.astype(BF16)
        sc = _dot_nt(qb, kb) * w
        num = a * _dot_nt(qb, cmat.astype(BF16)) + _dot(sc.astype(BF16), vb)
        den = a * jnp.sum(q * nvec, axis=1, keepdims=True) + jnp.sum(sc, axis=1, keepdims=True)
        h = num / jnp.maximum(jnp.abs(den), jnp.exp(-mt))
        hn = _rms(h) * gain_ref[hh]
        hm_ref[rows, cols] = (hn * _sigmoid(og_ref[rows, cols].astype(F32))).astype(hm_ref.dtype)

        b_last = b_c[lc - 1:lc, :]
        g = b_last - b_c + ig_c
        m_new = jnp.maximum(b_last + m0, jnp.max(g, axis=0, keepdims=True))
        ws = jnp.exp(g - m_new)
        a_last = jnp.exp(b_last + m0 - m_new)
        c_ref[hh] = a_last * cmat + _dot_tn((v * ws).astype(BF16), kb)
        n_ref[hh] = a_last * nvec + jnp.sum(ws * k, axis=0, keepdims=True)
        m_ref[hh] = m_new

    for hh in range(hps):
        if t == lc:
            one_chunk(hh, pl.ds(0, lc))
        else:
            def body(ci, carry, hh=hh):
                one_chunk(hh, pl.ds(pl.multiple_of(ci * lc, lc), lc))
                return carry

            lax.fori_loop(0, t // lc, body, 0)


def _mlstm(proj, gcol, grow, c0, n0, m0, gain, batch, t, n_heads, d, col0, hps, out_dtype, to_bf16=()):
    lc = math.gcd(t, MOBA_BLOCK)
    wb = hps * d
    assert col0 % wb == 0 and n_heads % hps == 0
    h_steps = n_heads // hps
    n_steps = batch * h_steps

    def col(off):
        return pl.BlockSpec((t, wb), lambda b, h: (b, (col0 + off * n_heads * d) // wb + h))

    lead = to_bf16[0].shape[0] if to_bf16 else 0
    per_step = -(-lead // n_steps) if lead else 1
    assert all(w.shape[0] == lead for w in to_bf16) and lead % per_step == 0
    cast_steps = lead // per_step

    def slab(w):
        return pl.BlockSpec((per_step,) + w.shape[1:],
                            lambda b, h: (jnp.minimum(b * h_steps + h, cast_steps - 1),) + (0,) * (w.ndim - 1))

    per_head = lambda shape: pl.BlockSpec((None, hps) + shape, lambda b, h: (b, h, 0, 0))
    outs = pl.pallas_call(
        functools.partial(_mlstm_kernel, chunk=lc, hps=hps, n_cast=len(to_bf16), cast_steps=cast_steps),
        grid=(batch, h_steps),
        in_specs=[col(0), col(1), col(2), col(3), per_head((t, 2)), per_head((2, t)),
                  per_head((d, d)), per_head((1, d)), per_head((1, 1)),
                  pl.BlockSpec((hps, 1, d), lambda b, h: (h, 0, 0))] + [slab(w) for w in to_bf16],
        out_specs=[pl.BlockSpec((t, wb), lambda b, h: (b, h)), per_head((d, d)), per_head((1, d)),
                   per_head((1, 1))] + [slab(w) for w in to_bf16],
        out_shape=[jax.ShapeDtypeStruct((batch * t, n_heads * d), out_dtype),
                   jax.ShapeDtypeStruct((batch, n_heads, d, d), F32),
                   jax.ShapeDtypeStruct((batch, n_heads, 1, d), F32),
                   jax.ShapeDtypeStruct((batch, n_heads, 1, 1), F32)]
                  + [jax.ShapeDtypeStruct(w.shape, BF16) for w in to_bf16],
        compiler_params=_cparams(("arbitrary", "arbitrary")),
        name=f"mlstm_t{t}",
    )(proj, proj, proj, proj, gcol, grow, c0, n0, m0, gain, *to_bf16)
    return outs[:4], outs[4:]


def _route(logits, n_groups, n_experts):
    epg = n_experts // n_groups
    lane = lax.broadcasted_iota(jnp.int32, logits.shape, 1).astype(F32)
    big = float(LANES)

    def rmax(x):
        return jnp.max(x, axis=1, keepdims=True)

    def first_lane(mask):
        return jnp.min(jnp.where(mask, lane, big), axis=1, keepdims=True)

    is_g = (lane >= n_experts) & (lane < n_experts + n_groups)
    gmax = rmax(jnp.where(is_g, logits, -jnp.inf))
    g_w = 1.0 / jnp.sum(jnp.where(is_g, jnp.exp(logits - gmax), 0.0), axis=1, keepdims=True)
    group = first_lane(is_g & (logits == gmax)) - n_experts
    lo = group * epg
    in_grp = (lane >= lo) & (lane < lo + epg)
    emax = rmax(jnp.where(in_grp, logits, -jnp.inf))
    pe = jnp.where(in_grp, jnp.exp(logits - emax), 0.0)
    prob = pe / jnp.sum(pe, axis=1, keepdims=True)
    p1 = rmax(jnp.where(in_grp, prob, -1.0))
    i1 = first_lane(in_grp & (prob == p1))
    rest = in_grp & (lane != i1)
    p2 = rmax(jnp.where(rest, prob, -1.0))
    i2 = first_lane(rest & (prob == p2))
    den = p1 + p2
    gate = g_w * jnp.where(lane == i1, p1 / den, jnp.where(lane == i2, p2 / den, 0.0))
    return gate, group


def _outproj_kernel(n_p, att_w, n_groups, n_experts,
                    attp, atts, hmp, hms, xp, xs, g1p, g1s, sh2p, sh2s, sc2p, sc2s,
                    wo_ref, gffn_ref, wr_ref, br_ref, x1_ref, h2g_ref, rinfo_ref, counts_ref, carry_ref):
    d = x1_ref.shape[1]

    @pl.when(pl.program_id(0) == 0)
    def _():
        carry_ref[...] = jnp.zeros_like(carry_ref)

    def body(att_ref, hm_ref, x_ref, g1_ref, sh2_ref, sc2_ref):
        mixed = (_dot(att_ref[...].astype(BF16), wo_ref[0:att_w, :])
                 + _dot(hm_ref[...].astype(BF16), wo_ref[att_w:, :]))
        tm = x_ref.shape[0]
        x1 = x_ref[...] + g1_ref[...] * mixed
        x1_ref[0:tm, :] = x1
        h2 = _rms(x1) * gffn_ref[...] * (1.0 + sc2_ref[...]) + sh2_ref[...]
        gate, group = _route(_dot3(h2, wr_ref[...]) + br_ref[...], n_groups, n_experts)
        h2g_ref[0:tm, :d] = h2
        h2g_ref[0:tm, d:] = gate
        lane = lax.broadcasted_iota(jnp.int32, (tm, LANES), 1).astype(F32)
        onehot = (lane == group).astype(F32)
        r = lax.broadcasted_iota(jnp.int32, (tm, tm), 0)
        c = lax.broadcasted_iota(jnp.int32, (tm, tm), 1)
        before = _dot((r > c).astype(BF16), onehot.astype(BF16)) + carry_ref[...]
        rank = jnp.sum(onehot * before, axis=1, keepdims=True)
        rinfo_ref[0:tm, :] = jnp.where(lane == 0.0, group, jnp.where(lane == 1.0, rank, 0.0))
        carry_ref[...] += jnp.sum(onehot, axis=0, keepdims=True)
        counts_ref[...] = carry_ref[...]

    _dual(n_p, body, (attp, hmp, xp, g1p, sh2p, sc2p), (atts, hms, xs, g1s, sh2s, sc2s))


def _outproj_stage(att_p, att_s, hm_p, hm_s, x_p, x_s, g1_p, g1_s, sh2_p, sh2_s, sc2_p, sc2_s,
                   w_out_bf, g_ffn, w_route, b_route, n_groups, n_experts, t_p):
    tm = OUTPROJ_TILE
    ts = x_s.shape[0]
    d = x_p.shape[1]
    att_w = att_p.shape[1]
    ml_w = hm_p.shape[1]
    assert x_p.shape[0] % tm == 0 and t_p % tm == 0 and ts <= tm and ts % 8 == 0
    n_p = x_p.shape[0] // tm
    n_tok = x_p.shape[0] + ts
    prow = lambda w: pl.BlockSpec((tm, w), _p_rows(n_p))
    srow = lambda w: pl.BlockSpec((ts, w), lambda i: (0, 0))
    pbat = pl.BlockSpec((None, 1, d), _p_batch(n_p, t_p // tm))
    const = lambda shape: pl.BlockSpec(shape, lambda i: (0,) * len(shape))
    out_row = lambda w: pl.BlockSpec((tm, w), lambda i: (i, 0))
    return pl.pallas_call(
        functools.partial(_outproj_kernel, n_p, att_w, n_groups, n_experts),
        grid=(n_p + 1,),
        in_specs=[prow(att_w), srow(att_w), prow(ml_w), srow(ml_w), prow(d), srow(d),
                  pbat, srow(d), pbat, srow(d), pbat, srow(d),
                  const((att_w + ml_w, d)), const((1, d)), const((d, LANES)), const((1, LANES))],
        out_specs=[out_row(d), out_row(d + LANES), out_row(LANES), const((1, LANES))],
        out_shape=[jax.ShapeDtypeStruct((n_tok, d), F32), jax.ShapeDtypeStruct((n_tok, d + LANES), F32),
                   jax.ShapeDtypeStruct((n_tok, LANES), F32), jax.ShapeDtypeStruct((1, LANES), F32)],
        scratch_shapes=[pltpu.VMEM((1, LANES), F32)],
        compiler_params=_cparams(("arbitrary",)),
        name="out_projection_router",
    )(att_p, att_s, hm_p, hm_s, x_p, x_s, g1_p, g1_s, sh2_p, sh2_s, sc2_p, sc2_s,
      w_out_bf, g_ffn, w_route, b_route)


MOE_TILE = 512
ROW_DMA_UNROLL = 8


class _RowGather:
    def __init__(self, idx_ref, src_hbm, buf_ref, sem):
        self.idx_ref, self.src_hbm, self.buf_ref, self.sem = idx_ref, src_hbm, buf_ref, sem
        self.tm = buf_ref.shape[1]

    def _copy(self, row_index, r, slot):
        return pltpu.make_async_copy(self.src_hbm.at[pl.ds(row_index, 1)],
                                     self.buf_ref.at[slot, pl.ds(r, 1)], self.sem.at[slot])

    def start(self, tile, slot):
        def body(r, carry):
            self._copy(self.idx_ref[tile * self.tm + r], r, slot).start()
            return carry

        lax.fori_loop(0, self.tm, body, 0, unroll=ROW_DMA_UNROLL)

    def start_inline(self, tile, slot, first_row, count):
        for u in range(count):
            r = first_row + u
            self._copy(self.idx_ref[tile * self.tm + r], r, slot).start()

    def wait(self, slot):
        def body(r, carry):
            self._copy(0, r, slot).wait()
            return carry

        lax.fori_loop(0, self.tm, body, 0, unroll=ROW_DMA_UNROLL)


def _moe_kernel(tg_ref, nu_ref, src_ref, h2g_hbm, wg_ref, wu_ref, wd_ref, o_ref, hbuf_ref, sem, *, epg):
    t = pl.program_id(0)
    n_tiles = pl.num_programs(0)
    tm, d = o_ref.shape
    n_used = nu_ref[0]
    used = t < n_used
    slot = t % 2
    rows = _RowGather(src_ref, h2g_hbm, hbuf_ref, sem)

    @pl.when(t == 0)
    def _():
        rows.start(0, 0)

    @pl.when(t <= n_used)
    def _():
        rows.wait(slot)

    @pl.when(jnp.logical_not(used))
    def _():
        o_ref[...] = jnp.zeros_like(o_ref)

    @pl.when(used)
    def _():
        rows.start_inline(jnp.minimum(t + 1, n_tiles - 1), 1 - slot, 0, tm)
        h2 = hbuf_ref[slot, :, :d].astype(BF16)
        gate = hbuf_ref[slot, :, d:]
        lane = lax.broadcasted_iota(jnp.int32, gate.shape, 1)
        for j in range(epg):
            hg = _dot(h2, wg_ref[j])
            hu = _dot(h2, wu_ref[j])
            ge = jnp.sum(jnp.where(lane == tg_ref[t] * epg + j, gate, 0.0), axis=1, keepdims=True)
            y = _dot((hg * _sigmoid(hg) * hu * ge).astype(BF16), wd_ref[j])
            if j == 0:
                o_ref[...] = y
            else:
                o_ref[...] += y

    @pl.when(used & (t == n_tiles - 1))
    def _():
        rows.wait(1 - slot)


def _moe_stage(h2g, rinfo, counts, w_gate_bf, w_up_bf, w_down_bf, n_groups):
    n_tok, dw = h2g.shape
    d = dw - LANES
    n_experts, _, ff = w_gate_bf.shape
    epg = n_experts // n_groups
    tm = MOE_TILE
    n_tiles = (n_tok + n_groups * (tm - 1)) // tm

    group = rinfo[:, 0].astype(jnp.int32)
    rank = rinfo[:, 1].astype(jnp.int32)
    cnt = counts[0, :n_groups].astype(jnp.int32)
    tiles_g = (cnt + tm - 1) // tm
    tile_end = jnp.cumsum(tiles_g)
    row_start = (tile_end - tiles_g) * tm
    dest = row_start[group] + rank
    tile_ids = jnp.arange(n_tiles, dtype=jnp.int32)
    tile_group = jnp.minimum(jnp.sum(tile_ids[:, None] >= tile_end[None, :], axis=1), n_groups - 1).astype(jnp.int32)
    n_used = tile_end[-1:].astype(jnp.int32)
    src = jnp.zeros((n_tiles * tm,), jnp.int32).at[dest].set(jnp.arange(n_tok, dtype=jnp.int32))

    def w_spec(rows_, cols_):
        return pl.BlockSpec((None, epg, rows_, cols_),
                            lambda t, tg, nu, sr: (jnp.where(t < nu[0], tg[t], n_groups - 1), 0, 0, 0),
                            pipeline_mode=pl.Buffered(1))

    by_group = lambda w: w.reshape((n_groups, epg) + w.shape[1:])
    grid_spec = pltpu.PrefetchScalarGridSpec(
        num_scalar_prefetch=3,
        grid=(n_tiles,),
        in_specs=[pl.BlockSpec(memory_space=pl.ANY), w_spec(d, ff), w_spec(d, ff), w_spec(ff, d)],
        out_specs=pl.BlockSpec((tm, d), lambda t, tg, nu, sr: (t, 0)),
        scratch_shapes=[pltpu.VMEM((2, tm, dw), F32), pltpu.SemaphoreType.DMA((2,))],
    )
    ys = pl.pallas_call(
        functools.partial(_moe_kernel, epg=epg),
        grid_spec=grid_spec,
        out_shape=jax.ShapeDtypeStruct((n_tiles * tm, d), F32),
        compiler_params=_cparams(("arbitrary",)),
        name="moe_experts",
    )(tile_group, n_used, src, h2g, by_group(w_gate_bf), by_group(w_up_bf), by_group(w_down_bf))
    return ys, dest


def _final_kernel(n_p, dest_ref, x1_ref, moe_hbm, g2p, g2s, gfin_ref, yp_ref, ys_ref, mbuf_ref, sem):
    i = pl.program_id(0)
    slot = i % 2
    rows = _RowGather(dest_ref, moe_hbm, mbuf_ref, sem)

    @pl.when(i == 0)
    def _():
        rows.start(0, 0)

    rows.wait(slot)
    n = pl.num_programs(0)

    def body(g2_ref, y_ref):
        rows.start_inline(jnp.minimum(i + 1, n - 1), 1 - slot, 0, rows.tm)
        x2 = x1_ref[...] + g2_ref[...] * mbuf_ref[slot]
        y_ref[...] = _rms(x2) * gfin_ref[...]

    _dual(n_p, body, (g2p, yp_ref), (g2s, ys_ref))

    @pl.when(i == n - 1)
    def _():
        rows.wait(1 - slot)


def _final_stage(x1, moe_rows, dest, g2_p, g2_s, g_final, n_tok_p, t_p):
    tm = TOKEN_TILE
    n_tok, d = x1.shape
    n_p = n_tok_p // tm
    n_s = (n_tok - n_tok_p) // tm
    grid_spec = pltpu.PrefetchScalarGridSpec(
        num_scalar_prefetch=1,
        grid=(n_p + n_s,),
        in_specs=[pl.BlockSpec((tm, d), lambda i, ds: (i, 0)), pl.BlockSpec(memory_space=pl.ANY),
                  pl.BlockSpec((None, 1, d), lambda i, ds: _p_batch(n_p, t_p // tm)(i)),
                  pl.BlockSpec((tm, d), lambda i, ds: _s_rows(n_p)(i)), pl.BlockSpec((1, d), lambda i, ds: (0, 0))],
        out_specs=[pl.BlockSpec((tm, d), lambda i, ds: _p_rows(n_p)(i)),
                   pl.BlockSpec((tm, d), lambda i, ds: _s_rows(n_p)(i))],
        scratch_shapes=[pltpu.VMEM((2, tm, d), F32), pltpu.SemaphoreType.DMA((2,))],
    )
    return pl.pallas_call(
        functools.partial(_final_kernel, n_p),
        grid_spec=grid_spec,
        out_shape=[jax.ShapeDtypeStruct((n_tok_p, d), F32), jax.ShapeDtypeStruct((n_tok - n_tok_p, d), F32)],
        compiler_params=_cparams(("arbitrary",)),
        name="final_norm",
    )(dest, x1, moe_rows, g2_p, g2_s, g_final)


def kernel(x_prompt, x_sample, cache_k, cache_v, page_table, state_C, state_n, state_m, c_prompt, c_sample,
           w_in, b_ig, b_fg, ml_gain, w_out, g_mix, g_ffn, w_mod, b_mod, w_grp, b_grp, w_exp, b_exp,
           w_gate, w_up, w_down, g_final):
    bp, tp, d = x_prompt.shape
    bd, td, _ = x_sample.shape
    depth = w_in.shape[0]
    assert depth == 1, "one trunk layer"
    n_heads, hd = cache_k.shape[-2:]
    att_w = n_heads * hd
    nh_ml, d_ml = state_n.shape[-2:]
    ml_w = nh_ml * d_ml
    assert att_w == ml_w and att_w + ml_w == d
    n_groups, n_experts = w_grp.shape[-1], w_exp.shape[-1]
    assert n_experts + n_groups <= LANES and 2 * nh_ml <= LANES
    main_w = 3 * att_w + 4 * ml_w
    ntp, nts = bp * tp, bd * td

    c_rows = bp + bd
    c_pad = -c_rows % 16
    c_all = jnp.concatenate([c_prompt, c_sample, jnp.zeros((c_pad, d), F32)], axis=0)
    mod = _modulation(c_all, w_mod[0], b_mod[0])
    mod_p = mod[:bp].reshape(bp, N_MOD, 1, d)
    mod_s = jnp.repeat(mod[bp:c_rows].reshape(bd, N_MOD, d), td, axis=0)
    shift1_p, scale1_p, gate1_p, shift2_p, scale2_p, gate2_p = (mod_p[:, i] for i in range(N_MOD))
    shift1_s, scale1_s, gate1_s, shift2_s, scale2_s, gate2_s = (mod_s[:, i] for i in range(N_MOD))

    x_p = x_prompt.reshape(ntp, d)
    x_s = x_sample.reshape(nts, d)

    b_gates = jnp.pad(jnp.concatenate([b_ig[0], b_fg[0]]), (0, LANES - 2 * nh_ml)).reshape(1, LANES)
    w_in_t = w_in[0].T
    h_all, gates = _norm_stage(x_p, x_s, shift1_p, shift1_s, scale1_p, scale1_s, g_mix[0].reshape(1, d),
                               w_in_t, main_w, b_gates, nh_ml, tp)

    rest_p, k_p, v_p = _in_projection(h_all, w_in_t, 0, ntp, main_w, att_w, BF16, "in_projection_prompt")
    rest_s, k_s, v_s = _in_projection(h_all, w_in_t, ntp, nts, main_w, att_w, F32, "in_projection_decode")

    slopes = 2.0 ** (-8.0 * jnp.arange(1, n_heads + 1, dtype=F32) / n_heads)
    att_p, att_s = _attention(rest_p, k_p, v_p, slopes, bp, tp, rest_s, k_s, v_s, cache_k, cache_v, page_table,
                              bd, td, n_heads, hd)

    def gate_layouts(g, b, t):
        g = g[:, :2 * nh_ml].reshape(b, t, 2, nh_ml)
        return g.transpose(0, 3, 1, 2), g.transpose(0, 3, 2, 1)

    gcol_p, grow_p = gate_layouts(gates[:ntp], bp, tp)
    gcol_s, grow_s = gate_layouts(gates[ntp:], bd, td)
    gain = ml_gain[0].reshape(nh_ml, 1, d_ml)
    zeros = lambda *s: jnp.zeros(s, F32)
    (hm_p, c_p, n_p, m_p), expert_w = _mlstm(
        rest_p, gcol_p, grow_p, zeros(bp, nh_ml, d_ml, d_ml), zeros(bp, nh_ml, 1, d_ml), zeros(bp, nh_ml, 1, 1),
        gain, bp, tp, nh_ml, d_ml, att_w, 1, BF16, to_bf16=(w_gate[0], w_up[0], w_down[0]))
    (hm_s, c_s, n_s, m_s), _ = _mlstm(
        rest_s, gcol_s, grow_s, state_C[0], state_n[0].reshape(bd, nh_ml, 1, d_ml),
        state_m[0].reshape(bd, nh_ml, 1, 1), gain, bd, td, nh_ml, d_ml, att_w, nh_ml, F32)

    w_route = jnp.pad(jnp.concatenate([w_exp[0], w_grp[0]], axis=1), ((0, 0), (0, LANES - n_experts - n_groups)))
    b_route = jnp.pad(jnp.concatenate([b_exp[0], b_grp[0]]), (0, LANES - n_experts - n_groups)).reshape(1, LANES)
    x1, h2g, rinfo, counts = _outproj_stage(att_p, att_s, hm_p, hm_s, x_p, x_s, gate1_p, gate1_s, shift2_p,
                                            shift2_s, scale2_p, scale2_s, w_out[0].astype(BF16),
                                            g_ffn[0].reshape(1, d), w_route, b_route, n_groups, n_experts, tp)

    moe_rows, dest = _moe_stage(h2g, rinfo, counts, *expert_w, n_groups)
    y_p, y_s = _final_stage(x1, moe_rows, dest, gate2_p, gate2_s, g_final.reshape(1, d), ntp, tp)

    kv_shape_p = (1, bp, tp, n_heads, hd)
    kv_shape_s = (1, bd, td, n_heads, hd)
    return (y_p.reshape(bp, tp, d), y_s.reshape(bd, td, d),
            k_p.reshape(kv_shape_p), v_p.reshape(kv_shape_p), k_s.reshape(kv_shape_s), v_s.reshape(kv_shape_s),
            c_p[None], n_p.reshape(1, bp, nh_ml, d_ml), m_p.reshape(1, bp, nh_ml),
            c_s[None], n_s.reshape(1, bd, nh_ml, d_ml), m_s.reshape(1, bd, nh_ml))
```
